```python
import math
import jax
import jax.numpy as jnp
from jax import lax
import numpy as np


D_MODEL = 1024
BATCH = 4
SEQ = 8192
DEPTH = 4

N_MIXERS = 3
N_HYENA = (DEPTH + 2) // 3
N_ATTN = (DEPTH + 1) // 3
N_POOL = DEPTH // 3
EPS = 1e-6

HY_ORDER = 2
HY_SHORT = 3
HY_EMB = 33
HY_BANDS = (HY_EMB - 1) // 2
HY_FW = 64
HY_INNER = 2
HY_TARGET = 1e-2
HY_FAST_PCT = 0.3
HY_SLOW_PCT = 1.5
HY_MIN_DECAY = math.log(HY_TARGET) / HY_SLOW_PCT
HY_MAX_DECAY = math.log(HY_TARGET) / HY_FAST_PCT

HEAD_DIM = 64
N_HEADS = D_MODEL // HEAD_DIM
N_KV = N_HEADS // 4
GROUP = N_HEADS // N_KV
WINDOW = 128
BLOCK = 128
REL_BUCKETS = 32
REL_MAX_DIST = 128

POOL_WINDOWS = (2, 4, 8, 16)
POOL_GROUP = D_MODEL // len(POOL_WINDOWS)

D_FF = -(-8 * D_MODEL // (3 * 256)) * 256

kernel_name = 'hybrid_hyena_swa_pool_encoder'


def rmsnorm(x, g):
    xf = x.astype(jnp.float32)
    y = xf * lax.rsqrt(jnp.mean(xf * xf, axis=-1, keepdims=True) + EPS)
    return (y * g.astype(jnp.float32)).astype(x.dtype)


def hyena_filters(L, w1, b1, wi, bi, w3, freq):
    f32 = jnp.float32
    D = D_MODEL
    t = jnp.linspace(0.0, 1.0, L, dtype=f32)[:, None]
    pos = jnp.arange(L, dtype=f32)[:, None]
    f = jnp.linspace(1e-4, HY_BANDS - 1, HY_BANDS, dtype=f32)[None, :]
    ang = (2.0 * math.pi / L) * pos * f
    z = jnp.concatenate([t, jnp.cos(ang), -jnp.sin(ang)], axis=-1)
    fr = freq.astype(f32)
    h = jnp.sin(fr * (z @ w1.astype(f32) + b1.astype(f32)))
    for j in range(HY_INNER):
        h = jnp.sin(fr * (h @ wi[j].astype(f32) + bi[j].astype(f32)))
    h = h @ w3.astype(f32)
    deltas = jnp.abs(jnp.linspace(HY_MIN_DECAY, HY_MAX_DECAY, D, dtype=f32))
    decay = jnp.exp(-t * deltas[None, :])
    h = h.reshape(L, 2, HY_ORDER, D) * decay[:, None, None, :]
    fwd, bwd = h[:, 0], h[:, 1]
    k = jnp.concatenate([fwd, jnp.zeros((1, HY_ORDER, D), f32), bwd[1:][::-1]], axis=0)
    return k / jnp.sum(jnp.abs(k), axis=0, keepdims=True)


def long_conv(u, kf):
    L = u.shape[1]
    U = jnp.fft.rfft(u, n=2 * L, axis=1)
    return jnp.fft.irfft(U * kf[None], n=2 * L, axis=1)[:, :L]


def hyena_mixer(xn, w_in, b_in, conv_w, conv_b, fw1, fb1, fwi, fbi, fw3, freq, fbias, w_out, b_out):
    B, L, D = xn.shape
    f32 = jnp.float32
    p = xn @ w_in + b_in
    r = HY_SHORT // 2
    pp = jnp.pad(p, ((0, 0), (r, r), (0, 0)))
    p = conv_b + sum(pp[:, j:j + L] * conv_w[j] for j in range(HY_SHORT))
    x1, x2, v = jnp.split(p, 3, axis=-1)
    kf = jnp.fft.rfft(hyena_filters(L, fw1, fb1, fwi, fbi, fw3, freq), axis=0)
    z = v.astype(f32)
    for o, gate in enumerate((x1, x2)):
        z = gate.astype(f32) * (long_conv(z, kf[:, o]) + fbias[o].astype(f32) * z)
    return z.astype(xn.dtype) @ w_out + b_out


def t5_bucket(rel):
    half = REL_BUCKETS // 2
    exact = half // 2
    n = jnp.abs(rel)
    nf = jnp.maximum(n, 1).astype(jnp.float32)
    large = exact + (jnp.log(nf / exact) / math.log(REL_MAX_DIST / exact) * (half - exact)).astype(jnp.int32)
    large = jnp.minimum(large, half - 1)
    return jnp.where(rel > 0, half, 0) + jnp.where(n < exact, n, large)


def window_attention(xn, w_qkv, q_gain, k_gain, sink, w_o, rel_table):
    B, L, D = xn.shape
    f32 = jnp.float32
    nb = L // BLOCK
    qkv = xn @ w_qkv
    q, k, v = jnp.split(qkv, [N_HEADS * HEAD_DIM, (N_HEADS + N_KV) * HEAD_DIM], axis=-1)
    q = rmsnorm(q.reshape(B, L, N_HEADS, HEAD_DIM), q_gain)
    k = rmsnorm(k.reshape(B, L, N_KV, HEAD_DIM), k_gain)
    v = v.reshape(B, L, N_KV, HEAD_DIM)
    q = q.reshape(B, nb, BLOCK, N_KV, GROUP, HEAD_DIM)

    def windows(t):
        tp = jnp.pad(t, ((0, 0), (BLOCK, BLOCK), (0, 0), (0, 0))).reshape(B, nb + 2, BLOCK, N_KV, HEAD_DIM)
        return jnp.concatenate([tp[:, 0:nb], tp[:, 1:nb + 1], tp[:, 2:nb + 2]], axis=2)

    kw, vw = windows(k), windows(v)
    logits = jnp.einsum('bnqkgd,bnskd->bnkgqs', q, kw, preferred_element_type=f32) * (HEAD_DIM ** -0.5)
    a = jnp.arange(BLOCK)[:, None]
    j = jnp.arange(3 * BLOCK)[None, :]
    rel = j - BLOCK - a
    bias = rel_table[t5_bucket(rel)].astype(f32)
    bias = bias.transpose(2, 0, 1).reshape(N_KV, GROUP, BLOCK, 3 * BLOCK)
    kpos = (jnp.arange(nb)[:, None, None] - 1) * BLOCK + j[None]
    mask = (jnp.abs(rel)[None] <= WINDOW) & (kpos >= 0) & (kpos < L)
    logits = jnp.where(mask[None, :, None, None], logits + bias, -1e30)
    sink_l = sink.astype(f32).reshape(1, 1, N_KV, GROUP, 1, 1)
    m = jnp.maximum(jnp.max(logits, axis=-1, keepdims=True), sink_l)
    pr = jnp.exp(logits - m)
    pr = pr / (jnp.sum(pr, axis=-1, keepdims=True) + jnp.exp(sink_l - m))
    o = jnp.einsum('bnkgqs,bnskd->bnqkgd', pr.astype(vw.dtype), vw)
    return o.reshape(B, L, N_HEADS * HEAD_DIM) @ w_o


def pool_mixer(xn, w_grp, b, scale):
    B, L, D = xn.shape
    f32 = jnp.float32
    xf = xn.astype(f32)
    cs = jnp.pad(jnp.cumsum(xf, axis=1), ((0, 0), (1, 0), (0, 0)))
    t = jnp.arange(L)
    outs = []
    for g, w in enumerate(POOL_WINDOWS):
        r = w // 2
        lo = jnp.clip(t - r, 0, L)
        hi = jnp.clip(t + r + 1, 0, L)
        c = cs[..., g * POOL_GROUP:(g + 1) * POOL_GROUP]
        mean = (jnp.take(c, hi, axis=1) - jnp.take(c, lo, axis=1)) / (hi - lo).astype(f32)[:, None]
        outs.append(mean - xf[..., g * POOL_GROUP:(g + 1) * POOL_GROUP])
    d = jnp.stack(outs, axis=2).astype(xn.dtype)
    y = jnp.einsum('blgc,gcd->blgd', d, w_grp).reshape(B, L, D) + b
    return y * scale


def swiglu(xn, wg, wu, wd):
    return (jax.nn.silu(xn @ wg) * (xn @ wu)) @ wd


def _normal(key, shape, scale):
    return jax.random.normal(key, shape, jnp.float32) * scale


def setup_inputs(seed: int = 0) -> dict:
    key = jax.random.key(seed)
    k = jax.random.split(key, 28)
    D = D_MODEL
    QKV = (N_HEADS + 2 * N_KV) * HEAD_DIM
    return {
        'x': _normal(k[0], (BATCH, SEQ, D), 1.0),
        'norm_mix': 1.0 + _normal(k[1], (DEPTH, D), 0.02),
        'norm_ffn': 1.0 + _normal(k[2], (DEPTH, D), 0.02),
        'hy_w_in': _normal(k[3], (N_HYENA, D, 3 * D), D ** -0.5),
        'hy_b_in': _normal(k[4], (N_HYENA, 3 * D), 0.02),
        'hy_conv_w': _normal(k[5], (N_HYENA, HY_SHORT, 3 * D), HY_SHORT ** -0.5),
        'hy_conv_b': _normal(k[6], (N_HYENA, 3 * D), 0.02),
        'hy_f_w1': _normal(k[7], (N_HYENA, HY_EMB, HY_FW), HY_EMB ** -0.5),
        'hy_f_b1': _normal(k[8], (N_HYENA, HY_FW), 0.1),
        'hy_f_wi': _normal(k[9], (N_HYENA, HY_INNER, HY_FW, HY_FW), HY_FW ** -0.5),
        'hy_f_bi': _normal(k[10], (N_HYENA, HY_INNER, HY_FW), 0.1),
        'hy_f_w3': _normal(k[11], (N_HYENA, HY_FW, 2 * HY_ORDER * D), HY_FW ** -0.5),
        'hy_f_freq': 1.0 + _normal(k[12], (N_HYENA, HY_FW), 0.02),
        'hy_f_bias': _normal(k[13], (N_HYENA, HY_ORDER, D), 0.5),
        'hy_w_out': _normal(k[14], (N_HYENA, D, D), D ** -0.5),
        'hy_b_out': _normal(k[15], (N_HYENA, D), 0.02),
        'at_w_qkv': _normal(k[16], (N_ATTN, D, QKV), D ** -0.5),
        'at_q_gain': 1.0 + _normal(k[17], (N_ATTN, HEAD_DIM), 0.02),
        'at_k_gain': 1.0 + _normal(k[18], (N_ATTN, HEAD_DIM), 0.02),
        'at_sink': _normal(k[19], (N_ATTN, N_HEADS), 0.5),
        'at_w_o': _normal(k[20], (N_ATTN, N_HEADS * HEAD_DIM, D), (N_HEADS * HEAD_DIM) ** -0.5),
        'rel_table': _normal(k[21], (REL_BUCKETS, N_HEADS), 0.5),
        'pl_w': _normal(k[22], (N_POOL, len(POOL_WINDOWS), POOL_GROUP, POOL_GROUP), POOL_GROUP ** -0.5),
        'pl_b': _normal(k[23], (N_POOL, D), 0.02),
        'pl_scale': 1.0 + _normal(k[24], (N_POOL, D), 0.02),
        'ff_w_gate': _normal(k[25], (DEPTH, D, D_FF), D ** -0.5),
        'ff_w_up': _normal(k[26], (DEPTH, D, D_FF), D ** -0.5),
        'ff_w_down': _normal(k[27], (DEPTH, D_FF, D), D_FF ** -0.5),
    }


def reference(x, norm_mix, norm_ffn, hy_w_in, hy_b_in, hy_conv_w, hy_conv_b, hy_f_w1, hy_f_b1,
              hy_f_wi, hy_f_bi, hy_f_w3, hy_f_freq, hy_f_bias, hy_w_out, hy_b_out,
              at_w_qkv, at_q_gain, at_k_gain, at_sink, at_w_o, rel_table,
              pl_w, pl_b, pl_scale, ff_w_gate, ff_w_up, ff_w_down):
    for i in range(DEPTH):
        kind, s = i % N_MIXERS, i // N_MIXERS
        h = rmsnorm(x, norm_mix[i])
        if kind == 0:
            y = hyena_mixer(h, hy_w_in[s], hy_b_in[s], hy_conv_w[s], hy_conv_b[s], hy_f_w1[s], hy_f_b1[s],
                            hy_f_wi[s], hy_f_bi[s], hy_f_w3[s], hy_f_freq[s], hy_f_bias[s],
                            hy_w_out[s], hy_b_out[s])
        elif kind == 1:
            y = window_attention(h, at_w_qkv[s], at_q_gain[s], at_k_gain[s], at_sink[s], at_w_o[s], rel_table)
        else:
            y = pool_mixer(h, pl_w[s], pl_b[s], pl_scale[s])
        x = x + y
        x = x + swiglu(rmsnorm(x, norm_ffn[i]), ff_w_gate[i], ff_w_up[i], ff_w_down[i])
    return x
```

```python
import functools
import math

import jax
import jax.numpy as jnp
from jax import lax
from jax.experimental import pallas as pl
from jax.experimental.pallas import tpu as pltpu

F32 = jnp.float32
BF16 = jnp.bfloat16

D_MODEL = 1024
BATCH = 4
SEQ = 8192
DEPTH = 4
EPS = 1e-6

HY_SHORT = 3
HY_EMB = 33
HY_BANDS = 16
HY_FW = 64
HY_INNER = 2
HY_MIN_DECAY = math.log(1e-2) / 1.5
HY_MAX_DECAY = math.log(1e-2) / 0.3

HEAD_DIM = 64
N_HEADS = 16
N_KV = 4
GROUP = 4
WINDOW = 128
BLOCK = 128
REL_BUCKETS = 32
REL_MAX_DIST = 128

POOL_WINDOWS = (2, 4, 8, 16)
POOL_GROUP = 256
D_FF = 2816

SUBLANES = 8
LANES = 128
VMEM_LIMIT = 56 * 1024 * 1024

FFT_N = 2 * SEQ
FFT_N1 = 256
FFT_N2 = 64
FFT_K1 = FFT_N1 // 2
SLAB = 2 * FFT_K1 + SUBLANES
CONV_DT = 128


def _cparams(sem):
    return pltpu.CompilerParams(dimension_semantics=sem, vmem_limit_bytes=VMEM_LIMIT)


def _rms(x, g):
    return x * lax.rsqrt(jnp.mean(x * x, axis=-1, keepdims=True) + EPS) * g


def _norm_matmul_kernel(x_ref, g_ref, w_ref, b_ref, o_ref, xn_ref):
    @pl.when(pl.program_id(1) == 0)
    def _():
        xn_ref[...] = _rms(x_ref[...], g_ref[...]).astype(BF16)

    acc = jnp.dot(xn_ref[...], w_ref[...], preferred_element_type=F32)
    o_ref[...] = (acc + b_ref[...]).astype(o_ref.dtype)


def _norm_matmul(x2, g, w, b, tm=1024, tn=512, out_dtype=F32):
    m, k = x2.shape
    n = w.shape[1]
    return pl.pallas_call(
        _norm_matmul_kernel,
        grid=(m // tm, n // tn),
        in_specs=[
            pl.BlockSpec((tm, k), lambda i, j: (i, 0)),
            pl.BlockSpec((1, k), lambda i, j: (0, 0)),
            pl.BlockSpec((k, tn), lambda i, j: (0, j)),
            pl.BlockSpec((1, tn), lambda i, j: (0, j)),
        ],
        out_specs=pl.BlockSpec((tm, tn), lambda i, j: (i, j)),
        out_shape=jax.ShapeDtypeStruct((m, n), out_dtype),
        scratch_shapes=[pltpu.VMEM((tm, k), BF16)],
        compiler_params=_cparams(("parallel", "arbitrary")),
        name="norm_matmul",
    )(x2, g, w, b)


def _matmul_res_kernel(a_ref, w_ref, b_ref, x_ref, o_ref):
    acc = jnp.dot(a_ref[...].astype(BF16), w_ref[...], preferred_element_type=F32)
    o_ref[...] = x_ref[...] + acc + b_ref[...]


def _matmul_residual(a2, w, b, x2, tm=1024):
    m, k = a2.shape
    n = w.shape[1]
    return pl.pallas_call(
        _matmul_res_kernel,
        grid=(m // tm,),
        in_specs=[
            pl.BlockSpec((tm, k), lambda i: (i, 0)),
            pl.BlockSpec((k, n), lambda i: (0, 0)),
            pl.BlockSpec((1, n), lambda i: (0, 0)),
            pl.BlockSpec((tm, n), lambda i: (i, 0)),
        ],
        out_specs=pl.BlockSpec((tm, n), lambda i: (i, 0)),
        out_shape=jax.ShapeDtypeStruct((m, n), F32),
        compiler_params=_cparams(("parallel",)),
        name="matmul_residual",
    )(a2, w, b, x2)


def _ffn_kernel(x_ref, g_ref, wg_ref, wu_ref, wd_ref, o_ref, xn_ref, acc_ref):
    j = pl.program_id(1)

    @pl.when(j == 0)
    def _():
        xn_ref[...] = _rms(x_ref[...], g_ref[...]).astype(BF16)

    xn = xn_ref[...]
    gate = jnp.dot(xn, wg_ref[...], preferred_element_type=F32)
    up = jnp.dot(xn, wu_ref[...], preferred_element_type=F32)
    h = (gate * jax.nn.sigmoid(gate) * up).astype(BF16)
    part = jnp.dot(h, wd_ref[...], preferred_element_type=F32)

    @pl.when(j == 0)
    def _():
        acc_ref[...] = part

    @pl.when(j > 0)
    def _():
        acc_ref[...] += part

    @pl.when(j == pl.num_programs(1) - 1)
    def _():
        o_ref[...] = x_ref[...] + acc_ref[...]


def _ffn(x2, g, wg, wu, wd, tm=1024, tf=256):
    m, d = x2.shape
    f = wg.shape[1]
    return pl.pallas_call(
        _ffn_kernel,
        grid=(m // tm, f // tf),
        in_specs=[
            pl.BlockSpec((tm, d), lambda i, j: (i, 0)),
            pl.BlockSpec((1, d), lambda i, j: (0, 0)),
            pl.BlockSpec((d, tf), lambda i, j: (0, j)),
            pl.BlockSpec((d, tf), lambda i, j: (0, j)),
            pl.BlockSpec((tf, d), lambda i, j: (j, 0)),
        ],
        out_specs=pl.BlockSpec((tm, d), lambda i, j: (i, 0)),
        out_shape=jax.ShapeDtypeStruct((m, d), F32),
        scratch_shapes=[pltpu.VMEM((tm, d), BF16), pltpu.VMEM((tm, d), F32)],
        compiler_params=_cparams(("parallel", "arbitrary")),
        name="ffn",
    )(x2, g, wg, wu, wd)


def _shift_rows(cat, tm):
    rows = cat.shape[0]
    dn = pltpu.roll(cat, 1, 0)[SUBLANES:SUBLANES + tm]
    up = pltpu.roll(cat, rows - 1, 0)[SUBLANES:SUBLANES + tm]
    return dn, cat[SUBLANES:SUBLANES + tm], up


def _inproj_kernel(xp_ref, xc_ref, xn_ref, g_ref, w_ref, b_ref, cw_ref, cb_ref, o_ref, xs_ref, *, tm):
    i = pl.program_id(1)

    @pl.when(pl.program_id(2) == 0)
    def _():
        g = g_ref[...]
        xs_ref[0:SUBLANES, :] = _rms(xp_ref[...], g).astype(BF16)
        xs_ref[SUBLANES:SUBLANES + tm, :] = _rms(xc_ref[...], g).astype(BF16)
        xs_ref[SUBLANES + tm:, :] = _rms(xn_ref[...], g).astype(BF16)

    p = jnp.dot(xs_ref[...], w_ref[...], preferred_element_type=F32) + b_ref[...]
    t = i * tm - SUBLANES + lax.broadcasted_iota(jnp.int32, (tm + 2 * SUBLANES, 1), 0)
    p = jnp.where((t >= 0) & (t < SEQ), p, 0.0)
    dn, cur, up = _shift_rows(p, tm)
    cw = cw_ref[...]
    o_ref[...] = cb_ref[...] + cw[0:1] * dn + cw[1:2] * cur + cw[2:3] * up


def _inproj_conv(x, g, w, b, cw, cb, tm=512, tn=1024):
    bsz, seq, d = x.shape
    n = w.shape[1]
    hb = tm // SUBLANES
    last = seq // SUBLANES - 1
    return pl.pallas_call(
        functools.partial(_inproj_kernel, tm=tm),
        grid=(bsz, seq // tm, n // tn),
        in_specs=[
            pl.BlockSpec((None, SUBLANES, d), lambda bi, i, j: (bi, jnp.maximum(i * hb - 1, 0), 0)),
            pl.BlockSpec((None, tm, d), lambda bi, i, j: (bi, i, 0)),
            pl.BlockSpec((None, SUBLANES, d), lambda bi, i, j: (bi, jnp.minimum((i + 1) * hb, last), 0)),
            pl.BlockSpec((1, d), lambda bi, i, j: (0, 0)),
            pl.BlockSpec((d, tn), lambda bi, i, j: (0, j)),
            pl.BlockSpec((1, tn), lambda bi, i, j: (0, j)),
            pl.BlockSpec((HY_SHORT, tn), lambda bi, i, j: (0, j)),
            pl.BlockSpec((1, tn), lambda bi, i, j: (0, j)),
        ],
        out_specs=pl.BlockSpec((None, tm, tn), lambda bi, i, j: (bi, i, j)),
        out_shape=jax.ShapeDtypeStruct((bsz, seq, n), F32),
        scratch_shapes=[pltpu.VMEM((tm + 2 * SUBLANES, d), BF16)],
        compiler_params=_cparams(("parallel", "parallel", "arbitrary")),
        name="hyena_inproj",
    )(x, x, x, g, w, b, cw, cb)


def _dft_mats():
    i32 = jnp.int32
    k1 = jnp.arange(FFT_K1, dtype=i32)[:, None]
    n1 = jnp.arange(FFT_N1, dtype=i32)[None, :]
    m = ((2 * k1 + 1) * n1) % (2 * FFT_N1)
    ang = m.astype(F32) * (math.pi / FFT_N1)
    f1 = jnp.concatenate([jnp.cos(ang), -jnp.sin(ang)], axis=0)
    k1 = jnp.arange(FFT_K1, dtype=i32)[:, None, None]
    k2 = jnp.arange(FFT_N2, dtype=i32)[None, :, None]
    n2 = jnp.arange(FFT_N2, dtype=i32)[None, None, :]
    m = ((2 * (k1 + FFT_N1 * k2) + 1) * n2) % (2 * FFT_N)
    ang = m.astype(F32) * (math.pi / FFT_N)
    c, s = jnp.cos(ang), jnp.sin(ang)
    gm = jnp.concatenate(
        [jnp.concatenate([c, s], axis=2), jnp.concatenate([-s, c], axis=2)], axis=1)
    return f1, gm


def _hdot(a, b):
    return jnp.dot(a, b, preferred_element_type=F32, precision=lax.Precision.HIGHEST)


def _filtgen_kernel(fq_ref, w1t_ref, w1c_ref, w1s_ref, b1_ref, wi_ref, bi_ref, fr_ref, w3_ref, dl_ref,
                    taps_ref, l1_ref, *, tr):
    i = pl.program_id(1)
    n = i * tr + lax.broadcasted_iota(jnp.int32, (tr, 1), 0)
    back = n >= SEQ
    pos = jnp.where(back, FFT_N - n, n).astype(F32)
    t = pos * (1.0 / (SEQ - 1))
    ang = (2.0 * math.pi / SEQ) * pos * fq_ref[...]
    fr = fr_ref[...]
    pre = (t * w1t_ref[...] + _hdot(jnp.cos(ang), w1c_ref[...])
           + _hdot(-jnp.sin(ang), w1s_ref[...]))
    h = jnp.sin(fr * (pre + b1_ref[...]))
    for l in range(HY_INNER):
        h = jnp.sin(fr * (_hdot(h, wi_ref[l]) + bi_ref[l]))
    taps = _hdot(h, w3_ref[...]) * jnp.exp(-t * dl_ref[...])
    sign = jnp.where(back, -1.0, 1.0)
    taps = jnp.where(n == SEQ, 0.0, taps * sign)
    taps_ref[...] = taps
    part = jnp.sum(jnp.abs(taps), axis=0, keepdims=True)

    @pl.when(i == 0)
    def _():
        l1_ref[...] = part

    @pl.when(i > 0)
    def _():
        l1_ref[...] += part


def _filter_taps(fq, w1, b1, wi, bi, fr, w3, dl2, tr=1024, tc=512):
    ncol = 2 * D_MODEL
    nct = ncol // tc
    half_tiles = (FFT_N // tr) // 2
    w1t, w1c, w1s = w1[0:1], w1[1:1 + HY_BANDS], w1[1 + HY_BANDS:]
    bi = bi[:, None, :]
    return pl.pallas_call(
        functools.partial(_filtgen_kernel, tr=tr),
        grid=(nct, FFT_N // tr),
        in_specs=[
            pl.BlockSpec((1, HY_BANDS), lambda j, i: (0, 0)),
            pl.BlockSpec((1, HY_FW), lambda j, i: (0, 0)),
            pl.BlockSpec((HY_BANDS, HY_FW), lambda j, i: (0, 0)),
            pl.BlockSpec((HY_BANDS, HY_FW), lambda j, i: (0, 0)),
            pl.BlockSpec((1, HY_FW), lambda j, i: (0, 0)),
            pl.BlockSpec((HY_INNER, HY_FW, HY_FW), lambda j, i: (0, 0, 0)),
            pl.BlockSpec((HY_INNER, 1, HY_FW), lambda j, i: (0, 0, 0)),
            pl.BlockSpec((1, HY_FW), lambda j, i: (0, 0)),
            pl.BlockSpec((HY_FW, tc), lambda j, i: (0, (i // half_tiles) * nct + j)),
            pl.BlockSpec((1, tc), lambda j, i: (0, j)),
        ],
        out_specs=[
            pl.BlockSpec((tr, tc), lambda j, i: (i, j)),
            pl.BlockSpec((1, tc), lambda j, i: (0, j)),
        ],
        out_shape=[
            jax.ShapeDtypeStruct((FFT_N, ncol), F32),
            jax.ShapeDtypeStruct((1, ncol), F32),
        ],
        compiler_params=_cparams(("parallel", "arbitrary")),
        name="hyena_filter_taps",
    )(fq, w1t, w1c, w1s, b1, wi, bi, fr, w3, dl2)


def _stage1(src_ref, f1_ref, s_ref, rows):
    def body(n2, c):
        zr = src_ref[pl.ds(n2, rows, stride=FFT_N2), :].astype(BF16)
        a = jnp.dot(f1_ref[...], zr, preferred_element_type=F32)
        s_ref[pl.ds(pl.multiple_of(n2 * SLAB, SUBLANES), 2 * FFT_K1), :] = a
        return c
    lax.fori_loop(0, FFT_N2, body, 0)


def _stage2_rhs(s_ref, k1):
    re = s_ref[pl.ds(k1, FFT_N2, stride=SLAB), :]
    im = s_ref[pl.ds(FFT_K1 + k1, FFT_N2, stride=SLAB), :]
    return jnp.concatenate([re, im], axis=0).astype(BF16)


def _filt_dft_kernel(taps_ref, l1_ref, f1_ref, gm_ref, kf_ref, s_ref):
    _stage1(taps_ref, f1_ref, s_ref, FFT_N1)
    inv = 1.0 / l1_ref[...]

    def body(k1, c):
        x = jnp.dot(gm_ref[k1], _stage2_rhs(s_ref, k1), preferred_element_type=F32)
        kf_ref[pl.ds(pl.multiple_of(k1 * 2 * FFT_N2, 2 * FFT_N2), 2 * FFT_N2), :] = x * inv
        return c
    lax.fori_loop(0, FFT_K1, body, 0)


def _filter_spectrum(taps, l1, f1_full, gm):
    ncol = taps.shape[1]
    dt = CONV_DT
    return pl.pallas_call(
        _filt_dft_kernel,
        grid=(ncol // dt,),
        in_specs=[
            pl.BlockSpec((FFT_N, dt), lambda j: (0, j)),
            pl.BlockSpec((1, dt), lambda j: (0, j)),
            pl.BlockSpec((2 * FFT_K1, FFT_N1), lambda j: (0, 0)),
            pl.BlockSpec((FFT_K1, 2 * FFT_N2, 2 * FFT_N2), lambda j: (0, 0, 0),
                         pipeline_mode=pl.Buffered(1)),
        ],
        out_specs=pl.BlockSpec((FFT_K1 * 2 * FFT_N2, dt), lambda j: (0, j)),
        out_shape=jax.ShapeDtypeStruct((FFT_K1 * 2 * FFT_N2, ncol), F32),
        scratch_shapes=[pltpu.VMEM((FFT_N2 * SLAB, dt), F32)],
        compiler_params=_cparams(("parallel",)),
        name="hyena_filter_spectrum",
    )(taps, l1, f1_full, gm)


def _longconv_kernel(z_ref, g_ref, fb_ref, kf_ref, f1_ref, gm_ref, gh_ref, fi_ref, o_ref, s_ref):
    _stage1(z_ref, f1_ref, s_ref, FFT_N1 // 2)

    def mid(k1, c):
        x = jnp.dot(gm_ref[k1], _stage2_rhs(s_ref, k1), preferred_element_type=F32)
        kf = kf_ref[pl.ds(pl.multiple_of(k1 * 2 * FFT_N2, 2 * FFT_N2), 2 * FFT_N2), :]
        xr, xi = x[:FFT_N2], x[FFT_N2:]
        kr, ki = kf[:FFT_N2], kf[FFT_N2:]
        y = jnp.concatenate([xr * kr - xi * ki, xr * ki + xi * kr], axis=0).astype(BF16)
        b = jnp.dot(gh_ref[k1], y, preferred_element_type=F32)
        s_ref[pl.ds(k1, FFT_N2, stride=SLAB), :] = b[:FFT_N2]
        s_ref[pl.ds(FFT_K1 + k1, FFT_N2, stride=SLAB), :] = b[FFT_N2:]
        return c
    lax.fori_loop(0, FFT_K1, mid, 0)

    fb = fb_ref[...]

    def last(n2, c):
        rhs = s_ref[pl.ds(pl.multiple_of(n2 * SLAB, SUBLANES), 2 * FFT_K1), :].astype(BF16)
        y = jnp.dot(fi_ref[...], rhs, preferred_element_type=F32)
        idx = pl.ds(n2, FFT_N1 // 2, stride=FFT_N2)
        zt = z_ref[idx, :]
        o_ref[idx, :] = g_ref[idx, :] * (y + fb * zt)
        return c
    lax.fori_loop(0, FFT_N2, last, 0)


def _long_conv_gate(z, zcol, g, gcol, fb, kf, kcol, f1_half, gm, gh, fi):
    bsz = z.shape[0]
    dt = CONV_DT
    nt = D_MODEL // dt
    const3 = lambda j, bi: (0, 0, 0)
    return pl.pallas_call(
        _longconv_kernel,
        grid=(nt, bsz),
        in_specs=[
            pl.BlockSpec((None, SEQ, dt), lambda j, bi: (bi, 0, zcol + j)),
            pl.BlockSpec((None, SEQ, dt), lambda j, bi: (bi, 0, gcol + j)),
            pl.BlockSpec((1, dt), lambda j, bi: (0, j)),
            pl.BlockSpec((FFT_K1 * 2 * FFT_N2, dt), lambda j, bi: (0, kcol + j),
                         pipeline_mode=pl.Buffered(1)),
            pl.BlockSpec((2 * FFT_K1, FFT_N1 // 2), lambda j, bi: (0, 0)),
            pl.BlockSpec((FFT_K1, 2 * FFT_N2, 2 * FFT_N2), const3, pipeline_mode=pl.Buffered(1)),
            pl.BlockSpec((FFT_K1, 2 * FFT_N2, 2 * FFT_N2), const3, pipeline_mode=pl.Buffered(1)),
            pl.BlockSpec((FFT_N1 // 2, 2 * FFT_K1), lambda j, bi: (0, 0)),
        ],
        out_specs=pl.BlockSpec((None, SEQ, dt), lambda j, bi: (bi, 0, j)),
        out_shape=jax.ShapeDtypeStruct((bsz, SEQ, D_MODEL), F32),
        scratch_shapes=[pltpu.VMEM((FFT_N2 * SLAB, dt), F32)],
        compiler_params=_cparams(("parallel", "arbitrary")),
        name="hyena_long_conv",
    )(z, g, fb, kf, f1_half, gm, gh, fi)


def _hyena_layer(x, g_mix, w_in, b_in, cw, cb, fw1, fb1, fwi, fbi, fw3, freq, fbias, w_out, b_out, consts):
    f1_full, f1_half, gm, gh, fi, fq, dl2 = consts
    bsz, seq, d = x.shape
    pc = _inproj_conv(x, g_mix[None], w_in.astype(BF16), b_in[None], cw, cb[None])
    taps, l1 = _filter_taps(fq, fw1, fb1[None], fwi, fbi, freq[None], fw3, dl2)
    kf = _filter_spectrum(taps, l1, f1_full, gm)
    nt = D_MODEL // CONV_DT
    z1 = _long_conv_gate(pc, 2 * nt, pc, 0, fbias[0:1], kf, 0, f1_half, gm, gh, fi)
    z2 = _long_conv_gate(z1, 0, pc, nt, fbias[1:2], kf, nt, f1_half, gm, gh, fi)
    y = _matmul_residual(z2.reshape(bsz * seq, d), w_out.astype(BF16), b_out[None],
                         x.reshape(bsz * seq, d))
    return y.reshape(bsz, seq, d)


def _relbias_kernel(tab_ref, bucket_ref, band_ref, o_ref):
    bucket = bucket_ref[...]
    band = band_ref[...]
    for h in range(N_HEADS):
        acc = jnp.zeros(bucket.shape, F32)
        for k in range(REL_BUCKETS):
            acc = jnp.where(bucket == k, tab_ref[k, h], acc)
        o_ref[h] = jnp.where(band > 0, acc, -1e30)


def _rel_bias(rel_table):
    a = jnp.arange(BLOCK)[:, None]
    j = jnp.arange(3 * BLOCK)[None, :]
    rel = j - BLOCK - a
    half = REL_BUCKETS // 2
    exact = half // 2
    n = jnp.abs(rel)
    nf = jnp.maximum(n, 1).astype(F32)
    large = exact + (jnp.log(nf / exact) / math.log(REL_MAX_DIST / exact) * (half - exact)).astype(jnp.int32)
    large = jnp.minimum(large, half - 1)
    bucket = (jnp.where(rel > 0, half, 0) + jnp.where(n < exact, n, large)).astype(jnp.int32)
    band = (n <= WINDOW).astype(jnp.int32)
    return pl.pallas_call(
        _relbias_kernel,
        in_specs=[
            pl.BlockSpec(memory_space=pltpu.SMEM),
            pl.BlockSpec((BLOCK, 3 * BLOCK), lambda: (0, 0)),
            pl.BlockSpec((BLOCK, 3 * BLOCK), lambda: (0, 0)),
        ],
        out_specs=pl.BlockSpec((N_HEADS, BLOCK, 3 * BLOCK), lambda: (0, 0, 0)),
        out_shape=jax.ShapeDtypeStruct((N_HEADS, BLOCK, 3 * BLOCK), F32),
        name="rel_bias",
    )(rel_table, bucket, band)


def _attn_kernel(q_ref, kp_ref, kc_ref, kn_ref, bias_ref, sink_ref, qg_ref, kg_ref, o_ref):
    nb = pl.num_programs(1)
    n = pl.program_id(1)
    q = q_ref[...]
    kv = jnp.concatenate([kp_ref[...], kc_ref[...], kn_ref[...]], axis=0)
    col = lax.broadcasted_iota(jnp.int32, (1, 3 * BLOCK), 1)
    valid = ((col >= BLOCK) | (n > 0)) & ((col < 2 * BLOCK) | (n < nb - 1))
    qg = qg_ref[...] * (HEAD_DIM ** -0.5)
    kg = kg_ref[...]
    outs = []
    for kh in range(N_KV):
        kblk = _rms(kv[:, kh * HEAD_DIM:(kh + 1) * HEAD_DIM], kg).astype(BF16)
        voff = N_KV * HEAD_DIM + kh * HEAD_DIM
        vblk = kv[:, voff:voff + HEAD_DIM].astype(BF16)
        for gi in range(GROUP):
            h = kh * GROUP + gi
            qh = _rms(q[:, h * HEAD_DIM:(h + 1) * HEAD_DIM], qg).astype(BF16)
            logits = lax.dot_general(qh, kblk, (((1,), (1,)), ((), ())), preferred_element_type=F32)
            logits = jnp.where(valid, logits + bias_ref[h], -1e30)
            sink = sink_ref[h:h + 1, 0:1]
            m = jnp.maximum(jnp.max(logits, axis=-1, keepdims=True), sink)
            p = jnp.exp(logits - m)
            den = jnp.sum(p, axis=-1, keepdims=True) + jnp.exp(sink - m)
            o = jnp.dot(p.astype(BF16), vblk, preferred_element_type=F32)
            outs.append(o * (1.0 / den))
    o_ref[...] = jnp.concatenate(outs, axis=-1)


def _attention(qkv, bias, sink_b, q_gain, k_gain):
    bsz, seq, _ = qkv.shape
    nb = seq // BLOCK
    dq = N_HEADS * HEAD_DIM
    dkv = 2 * N_KV * HEAD_DIM
    kvb = dq // dkv
    return pl.pallas_call(
        _attn_kernel,
        grid=(bsz, nb),
        in_specs=[
            pl.BlockSpec((None, BLOCK, dq), lambda b, n: (b, n, 0)),
            pl.BlockSpec((None, BLOCK, dkv), lambda b, n: (b, jnp.maximum(n - 1, 0), kvb)),
            pl.BlockSpec((None, BLOCK, dkv), lambda b, n: (b, n, kvb)),
            pl.BlockSpec((None, BLOCK, dkv), lambda b, n: (b, jnp.minimum(n + 1, nb - 1), kvb)),
            pl.BlockSpec((N_HEADS, BLOCK, 3 * BLOCK), lambda b, n: (0, 0, 0)),
            pl.BlockSpec((N_HEADS, LANES), lambda b, n: (0, 0)),
            pl.BlockSpec((1, HEAD_DIM), lambda b, n: (0, 0)),
            pl.BlockSpec((1, HEAD_DIM), lambda b, n: (0, 0)),
        ],
        out_specs=pl.BlockSpec((None, BLOCK, dq), lambda b, n: (b, n, 0)),
        out_shape=jax.ShapeDtypeStruct((bsz, seq, dq), F32),
        compiler_params=_cparams(("parallel", "parallel")),
        name="window_attention",
    )(qkv, qkv, qkv, qkv, bias, sink_b, q_gain, k_gain)


def _attn_layer(x, g_mix, w_qkv, q_gain, k_gain, sink, w_o, rel_table):
    bsz, seq, d = x.shape
    x2 = x.reshape(bsz * seq, d)
    nq = w_qkv.shape[1]
    qkv = _norm_matmul(x2, g_mix[None], w_qkv.astype(BF16), jnp.zeros((1, nq), F32))
    bias = _rel_bias(rel_table)
    sink_b = jnp.broadcast_to(sink[:, None], (N_HEADS, LANES))
    o = _attention(qkv.reshape(bsz, seq, nq), bias, sink_b, q_gain[None], k_gain[None])
    y = _matmul_residual(o.reshape(bsz * seq, d), w_o.astype(BF16), jnp.zeros((1, d), F32), x2)
    return y.reshape(bsz, seq, d)


def _pool_kernel(xp_ref, xc_ref, xn_ref, g_ref, w_ref, b_ref, sc_ref, o_ref, *, tm):
    i = pl.program_id(1)
    g = g_ref[...]
    rows = tm + 2 * SUBLANES
    t = i * tm - SUBLANES + lax.broadcasted_iota(jnp.int32, (rows, 1), 0)
    inside = (t >= 0) & (t < SEQ)
    xa = jnp.concatenate([_rms(xp_ref[...], g), _rms(xc_ref[...], g), _rms(xn_ref[...], g)], axis=0)
    xa = jnp.where(inside, xa, 0.0)
    tc = t[SUBLANES:SUBLANES + tm]
    outs = []
    for gi, wdt in enumerate(POOL_WINDOWS):
        r = wdt // 2
        xg = xa[:, gi * POOL_GROUP:(gi + 1) * POOL_GROUP]
        run = xg
        span = 1
        while span < 2 * r:
            run = run + pltpu.roll(run, rows - span, 0)
            span *= 2
        win = (pltpu.roll(run, r, 0) + pltpu.roll(xg, rows - r, 0))[SUBLANES:SUBLANES + tm]
        cnt = (jnp.minimum(tc + r + 1, SEQ) - jnp.maximum(tc - r, 0)).astype(F32)
        dlt = win / cnt - xg[SUBLANES:SUBLANES + tm]
        outs.append(jnp.dot(dlt.astype(BF16), w_ref[gi], preferred_element_type=F32))
    y = (jnp.concatenate(outs, axis=-1) + b_ref[...]) * sc_ref[...]
    o_ref[...] = xc_ref[...] + y


def _pool_layer(x, g_mix, w_grp, b, scale, tm=512):
    bsz, seq, d = x.shape
    hb = tm // SUBLANES
    last = seq // SUBLANES - 1
    ng = len(POOL_WINDOWS)
    return pl.pallas_call(
        functools.partial(_pool_kernel, tm=tm),
        grid=(bsz, seq // tm),
        in_specs=[
            pl.BlockSpec((None, SUBLANES, d), lambda bi, i: (bi, jnp.maximum(i * hb - 1, 0), 0)),
            pl.BlockSpec((None, tm, d), lambda bi, i: (bi, i, 0)),
            pl.BlockSpec((None, SUBLANES, d), lambda bi, i: (bi, jnp.minimum((i + 1) * hb, last), 0)),
            pl.BlockSpec((1, d), lambda bi, i: (0, 0)),
            pl.BlockSpec((ng, POOL_GROUP, POOL_GROUP), lambda bi, i: (0, 0, 0)),
            pl.BlockSpec((1, d), lambda bi, i: (0, 0)),
            pl.BlockSpec((1, d), lambda bi, i: (0, 0)),
        ],
        out_specs=pl.BlockSpec((None, tm, d), lambda bi, i: (bi, i, 0)),
        out_shape=jax.ShapeDtypeStruct((bsz, seq, d), F32),
        compiler_params=_cparams(("parallel", "parallel")),
        name="pool_mixer",
    )(x, x, x, g_mix[None], w_grp.astype(BF16), b[None], scale[None])


def kernel(x, norm_mix, norm_ffn, hy_w_in, hy_b_in, hy_conv_w, hy_conv_b, hy_f_w1, hy_f_b1, hy_f_wi, hy_f_bi, hy_f_w3, hy_f_freq, hy_f_bias, hy_w_out, hy_b_out, at_w_qkv, at_q_gain, at_k_gain, at_sink, at_w_o, rel_table, pl_w, pl_b, pl_scale, ff_w_gate, ff_w_up, ff_w_down):
    bsz, seq, d = x.shape
    f1, gm = _dft_mats()
    f1_full = f1.astype(BF16)
    f1_half = f1[:, :FFT_N1 // 2].astype(BF16)
    fi = (f1[:, :FFT_N1 // 2].T * (2.0 / FFT_N)).astype(BF16)
    gh = jnp.swapaxes(gm, 1, 2).astype(BF16)
    gm = gm.astype(BF16)
    fq = jnp.linspace(1e-4, HY_BANDS - 1, HY_BANDS, dtype=F32)[None, :]
    deltas = jnp.abs(jnp.linspace(HY_MIN_DECAY, HY_MAX_DECAY, D_MODEL, dtype=F32))
    dl2 = jnp.concatenate([deltas, deltas])[None, :]
    consts = (f1_full, f1_half, gm, gh, fi, fq, dl2)

    for i in range(DEPTH):
        kind, s = i % 3, i // 3
        if kind == 0:
            x = _hyena_layer(x, norm_mix[i], hy_w_in[s], hy_b_in[s], hy_conv_w[s], hy_conv_b[s],
                             hy_f_w1[s], hy_f_b1[s], hy_f_wi[s], hy_f_bi[s], hy_f_w3[s],
                             hy_f_freq[s], hy_f_bias[s], hy_w_out[s], hy_b_out[s], consts)
        elif kind == 1:
            x = _attn_layer(x, norm_mix[i], at_w_qkv[s], at_q_gain[s], at_k_gain[s], at_sink[s],
                            at_w_o[s], rel_table)
        else:
            x = _pool_layer(x, norm_mix[i], pl_w[s], pl_b[s], pl_scale[s])
        x2 = _ffn(x.reshape(bsz * seq, d), norm_ffn[i][None], ff_w_gate[i].astype(BF16),
                  ff_w_up[i].astype(BF16), ff_w_down[i].astype(BF16))
        x = x2.reshape(bsz, seq, d)
    return x
```

```python
import functools
import math

import jax
import jax.numpy as jnp
from jax import lax
from jax.experimental import pallas as pl
from jax.experimental.pallas import tpu as pltpu

F32 = jnp.float32
BF16 = jnp.bfloat16

D_MODEL = 1024
BATCH = 4
SEQ = 8192
DEPTH = 4
EPS = 1e-6

HY_SHORT = 3
HY_EMB = 33
HY_BANDS = 16
HY_FW = 64
HY_INNER = 2
HY_MIN_DECAY = math.log(1e-2) / 1.5
HY_MAX_DECAY = math.log(1e-2) / 0.3

HEAD_DIM = 64
N_HEADS = 16
N_KV = 4
GROUP = 4
WINDOW = 128
BLOCK = 128
REL_BUCKETS = 32
REL_MAX_DIST = 128

POOL_WINDOWS = (2, 4, 8, 16)
POOL_GROUP = 256
D_FF = 2816

SUBLANES = 8
LANES = 128
VMEM_LIMIT = 56 * 1024 * 1024

FFT_N = 2 * SEQ
FFT_N1 = 256
FFT_N2 = 64
FFT_K1 = FFT_N1 // 2
SLAB = 2 * FFT_K1 + SUBLANES
CONV_DT = 128
DFT_UNROLL = 8


def _cparams(sem):
    return pltpu.CompilerParams(dimension_semantics=sem, vmem_limit_bytes=VMEM_LIMIT)


def _rms(x, g):
    return x * lax.rsqrt(jnp.mean(x * x, axis=-1, keepdims=True) + EPS) * g


def _norm_matmul_kernel(x_ref, g_ref, w_ref, b_ref, o_ref, xn_ref):
    @pl.when(pl.program_id(1) == 0)
    def _():
        xn_ref[...] = _rms(x_ref[...], g_ref[...]).astype(BF16)

    acc = jnp.dot(xn_ref[...], w_ref[...], preferred_element_type=F32)
    o_ref[...] = (acc + b_ref[...]).astype(o_ref.dtype)


def _norm_matmul(x2, g, w, b, tm=1024, tn=512, out_dtype=F32):
    m, k = x2.shape
    n = w.shape[1]
    return pl.pallas_call(
        _norm_matmul_kernel,
        grid=(m // tm, n // tn),
        in_specs=[
            pl.BlockSpec((tm, k), lambda i, j: (i, 0)),
            pl.BlockSpec((1, k), lambda i, j: (0, 0)),
            pl.BlockSpec((k, tn), lambda i, j: (0, j)),
            pl.BlockSpec((1, tn), lambda i, j: (0, j)),
        ],
        out_specs=pl.BlockSpec((tm, tn), lambda i, j: (i, j)),
        out_shape=jax.ShapeDtypeStruct((m, n), out_dtype),
        scratch_shapes=[pltpu.VMEM((tm, k), BF16)],
        compiler_params=_cparams(("parallel", "arbitrary")),
        name="norm_matmul",
    )(x2, g, w, b)


def _matmul_res_kernel(a_ref, w_ref, b_ref, x_ref, o_ref):
    acc = jnp.dot(a_ref[...].astype(BF16), w_ref[...], preferred_element_type=F32)
    o_ref[...] = x_ref[...] + acc + b_ref[...]


def _matmul_residual(a2, w, b, x2, tm=1024):
    m, k = a2.shape
    n = w.shape[1]
    return pl.pallas_call(
        _matmul_res_kernel,
        grid=(m // tm,),
        in_specs=[
            pl.BlockSpec((tm, k), lambda i: (i, 0)),
            pl.BlockSpec((k, n), lambda i: (0, 0)),
            pl.BlockSpec((1, n), lambda i: (0, 0)),
            pl.BlockSpec((tm, n), lambda i: (i, 0)),
        ],
        out_specs=pl.BlockSpec((tm, n), lambda i: (i, 0)),
        out_shape=jax.ShapeDtypeStruct((m, n), F32),
        compiler_params=_cparams(("parallel",)),
        name="matmul_residual",
    )(a2, w, b, x2)


def _ffn_kernel(x_ref, g_ref, wg_ref, wu_ref, wd_ref, o_ref, xn_ref, acc_ref):
    j = pl.program_id(1)

    @pl.when(j == 0)
    def _():
        xn_ref[...] = _rms(x_ref[...], g_ref[...]).astype(BF16)

    xn = xn_ref[...]
    gate = jnp.dot(xn, wg_ref[...], preferred_element_type=F32)
    up = jnp.dot(xn, wu_ref[...], preferred_element_type=F32)
    h = (gate * jax.nn.sigmoid(gate) * up).astype(BF16)
    part = jnp.dot(h, wd_ref[...], preferred_element_type=F32)

    @pl.when(j == 0)
    def _():
        acc_ref[...] = part

    @pl.when(j > 0)
    def _():
        acc_ref[...] += part

    @pl.when(j == pl.num_programs(1) - 1)
    def _():
        o_ref[...] = x_ref[...] + acc_ref[...]


def _ffn(x2, g, wg, wu, wd, tm=1024, tf=256):
    m, d = x2.shape
    f = wg.shape[1]
    return pl.pallas_call(
        _ffn_kernel,
        grid=(m // tm, f // tf),
        in_specs=[
            pl.BlockSpec((tm, d), lambda i, j: (i, 0)),
            pl.BlockSpec((1, d), lambda i, j: (0, 0)),
            pl.BlockSpec((d, tf), lambda i, j: (0, j)),
            pl.BlockSpec((d, tf), lambda i, j: (0, j)),
            pl.BlockSpec((tf, d), lambda i, j: (j, 0)),
        ],
        out_specs=pl.BlockSpec((tm, d), lambda i, j: (i, 0)),
        out_shape=jax.ShapeDtypeStruct((m, d), F32),
        scratch_shapes=[pltpu.VMEM((tm, d), BF16), pltpu.VMEM((tm, d), F32)],
        compiler_params=_cparams(("parallel", "arbitrary")),
        name="ffn",
    )(x2, g, wg, wu, wd)


def _shift_rows(cat, tm):
    rows = cat.shape[0]
    dn = pltpu.roll(cat, 1, 0)[SUBLANES:SUBLANES + tm]
    up = pltpu.roll(cat, rows - 1, 0)[SUBLANES:SUBLANES + tm]
    return dn, cat[SUBLANES:SUBLANES + tm], up


def _inproj_kernel(xp_ref, xc_ref, xn_ref, g_ref, w_ref, b_ref, cw_ref, cb_ref, o_ref, xs_ref, *, tm):
    i = pl.program_id(1)

    @pl.when(pl.program_id(2) == 0)
    def _():
        g = g_ref[...]
        xs_ref[0:SUBLANES, :] = _rms(xp_ref[...], g).astype(BF16)
        xs_ref[SUBLANES:SUBLANES + tm, :] = _rms(xc_ref[...], g).astype(BF16)
        xs_ref[SUBLANES + tm:, :] = _rms(xn_ref[...], g).astype(BF16)

    p = jnp.dot(xs_ref[...], w_ref[...], preferred_element_type=F32) + b_ref[...]
    t = i * tm - SUBLANES + lax.broadcasted_iota(jnp.int32, (tm + 2 * SUBLANES, 1), 0)
    p = jnp.where((t >= 0) & (t < SEQ), p, 0.0)
    dn, cur, up = _shift_rows(p, tm)
    cw = cw_ref[...]
    o_ref[...] = cb_ref[...] + cw[0:1] * dn + cw[1:2] * cur + cw[2:3] * up


def _inproj_conv(x, g, w, b, cw, cb, tm=512, tn=1024):
    bsz, seq, d = x.shape
    n = w.shape[1]
    hb = tm // SUBLANES
    last = seq // SUBLANES - 1
    return pl.pallas_call(
        functools.partial(_inproj_kernel, tm=tm),
        grid=(bsz, seq // tm, n // tn),
        in_specs=[
            pl.BlockSpec((None, SUBLANES, d), lambda bi, i, j: (bi, jnp.maximum(i * hb - 1, 0), 0)),
            pl.BlockSpec((None, tm, d), lambda bi, i, j: (bi, i, 0)),
            pl.BlockSpec((None, SUBLANES, d), lambda bi, i, j: (bi, jnp.minimum((i + 1) * hb, last), 0)),
            pl.BlockSpec((1, d), lambda bi, i, j: (0, 0)),
            pl.BlockSpec((d, tn), lambda bi, i, j: (0, j)),
            pl.BlockSpec((1, tn), lambda bi, i, j: (0, j)),
            pl.BlockSpec((HY_SHORT, tn), lambda bi, i, j: (0, j)),
            pl.BlockSpec((1, tn), lambda bi, i, j: (0, j)),
        ],
        out_specs=pl.BlockSpec((None, tm, tn), lambda bi, i, j: (bi, i, j)),
        out_shape=jax.ShapeDtypeStruct((bsz, seq, n), F32),
        scratch_shapes=[pltpu.VMEM((tm + 2 * SUBLANES, d), BF16)],
        compiler_params=_cparams(("parallel", "parallel", "arbitrary")),
        name="hyena_inproj",
    )(x, x, x, g, w, b, cw, cb)


def _dft_mats():
    i32 = jnp.int32
    k1 = jnp.arange(FFT_K1, dtype=i32)[:, None]
    n1 = jnp.arange(FFT_N1, dtype=i32)[None, :]
    m = ((2 * k1 + 1) * n1) % (2 * FFT_N1)
    ang = m.astype(F32) * (math.pi / FFT_N1)
    f1 = jnp.concatenate([jnp.cos(ang), -jnp.sin(ang)], axis=0)
    k1 = jnp.arange(FFT_K1, dtype=i32)[:, None, None]
    k2 = jnp.arange(FFT_N2, dtype=i32)[None, :, None]
    n2 = jnp.arange(FFT_N2, dtype=i32)[None, None, :]
    m = ((2 * (k1 + FFT_N1 * k2) + 1) * n2) % (2 * FFT_N)
    ang = m.astype(F32) * (math.pi / FFT_N)
    c, s = jnp.cos(ang), jnp.sin(ang)
    gm = jnp.concatenate(
        [jnp.concatenate([c, s], axis=2), jnp.concatenate([-s, c], axis=2)], axis=1)
    return f1, gm


def _hdot(a, b):
    return jnp.dot(a, b, preferred_element_type=F32, precision=lax.Precision.HIGHEST)


def _filtgen_kernel(fq_ref, w1t_ref, w1c_ref, w1s_ref, b1_ref, wi_ref, bi_ref, fr_ref, w3_ref, dl_ref,
                    taps_ref, l1_ref, *, tr):
    i = pl.program_id(1)
    n = i * tr + lax.broadcasted_iota(jnp.int32, (tr, 1), 0)
    back = n >= SEQ
    pos = jnp.where(back, FFT_N - n, n).astype(F32)
    t = pos * (1.0 / (SEQ - 1))
    ang = (2.0 * math.pi / SEQ) * pos * fq_ref[...]
    fr = fr_ref[...]
    pre = (t * w1t_ref[...] + _hdot(jnp.cos(ang), w1c_ref[...])
           + _hdot(-jnp.sin(ang), w1s_ref[...]))
    h = jnp.sin(fr * (pre + b1_ref[...]))
    for l in range(HY_INNER):
        h = jnp.sin(fr * (_hdot(h, wi_ref[l]) + bi_ref[l]))
    taps = _hdot(h, w3_ref[...]) * jnp.exp(-t * dl_ref[...])
    sign = jnp.where(back, -1.0, 1.0)
    taps = jnp.where(n == SEQ, 0.0, taps * sign)
    taps_ref[...] = taps
    part = jnp.sum(jnp.abs(taps), axis=0, keepdims=True)

    @pl.when(i == 0)
    def _():
        l1_ref[...] = part

    @pl.when(i > 0)
    def _():
        l1_ref[...] += part


def _filter_taps(fq, w1, b1, wi, bi, fr, w3, dl2, tr=1024, tc=512):
    ncol = 2 * D_MODEL
    nct = ncol // tc
    half_tiles = (FFT_N // tr) // 2
    w1t, w1c, w1s = w1[0:1], w1[1:1 + HY_BANDS], w1[1 + HY_BANDS:]
    bi = bi[:, None, :]
    return pl.pallas_call(
        functools.partial(_filtgen_kernel, tr=tr),
        grid=(nct, FFT_N // tr),
        in_specs=[
            pl.BlockSpec((1, HY_BANDS), lambda j, i: (0, 0)),
            pl.BlockSpec((1, HY_FW), lambda j, i: (0, 0)),
            pl.BlockSpec((HY_BANDS, HY_FW), lambda j, i: (0, 0)),
            pl.BlockSpec((HY_BANDS, HY_FW), lambda j, i: (0, 0)),
            pl.BlockSpec((1, HY_FW), lambda j, i: (0, 0)),
            pl.BlockSpec((HY_INNER, HY_FW, HY_FW), lambda j, i: (0, 0, 0)),
            pl.BlockSpec((HY_INNER, 1, HY_FW), lambda j, i: (0, 0, 0)),
            pl.BlockSpec((1, HY_FW), lambda j, i: (0, 0)),
            pl.BlockSpec((HY_FW, tc), lambda j, i: (0, (i // half_tiles) * nct + j)),
            pl.BlockSpec((1, tc), lambda j, i: (0, j)),
        ],
        out_specs=[
            pl.BlockSpec((tr, tc), lambda j, i: (i, j)),
            pl.BlockSpec((1, tc), lambda j, i: (0, j)),
        ],
        out_shape=[
            jax.ShapeDtypeStruct((FFT_N, ncol), F32),
            jax.ShapeDtypeStruct((1, ncol), F32),
        ],
        compiler_params=_cparams(("parallel", "arbitrary")),
        name="hyena_filter_taps",
    )(fq, w1t, w1c, w1s, b1, wi, bi, fr, w3, dl2)


def _stage1(src_ref, f1_ref, s_ref, rows):
    def body(n2, c):
        zr = src_ref[pl.ds(n2, rows, stride=FFT_N2), :].astype(BF16)
        a = jnp.dot(f1_ref[...], zr, preferred_element_type=F32)
        s_ref[pl.ds(pl.multiple_of(n2 * SLAB, SUBLANES), 2 * FFT_K1), :] = a
        return c
    lax.fori_loop(0, FFT_N2, body, 0, unroll=DFT_UNROLL)


def _stage2_rhs(s_ref, k1):
    re = s_ref[pl.ds(k1, FFT_N2, stride=SLAB), :]
    im = s_ref[pl.ds(FFT_K1 + k1, FFT_N2, stride=SLAB), :]
    return jnp.concatenate([re, im], axis=0).astype(BF16)


def _filt_dft_kernel(taps_ref, l1_ref, f1_ref, gm_ref, kf_ref, s_ref):
    _stage1(taps_ref, f1_ref, s_ref, FFT_N1)
    inv = 1.0 / l1_ref[...]

    def body(k1, c):
        x = jnp.dot(gm_ref[k1], _stage2_rhs(s_ref, k1), preferred_element_type=F32)
        kf_ref[pl.ds(pl.multiple_of(k1 * 2 * FFT_N2, 2 * FFT_N2), 2 * FFT_N2), :] = x * inv
        return c
    lax.fori_loop(0, FFT_K1, body, 0, unroll=DFT_UNROLL)


def _filter_spectrum(taps, l1, f1_full, gm):
    ncol = taps.shape[1]
    dt = CONV_DT
    return pl.pallas_call(
        _filt_dft_kernel,
        grid=(ncol // dt,),
        in_specs=[
            pl.BlockSpec((FFT_N, dt), lambda j: (0, j)),
            pl.BlockSpec((1, dt), lambda j: (0, j)),
            pl.BlockSpec((2 * FFT_K1, FFT_N1), lambda j: (0, 0)),
            pl.BlockSpec((FFT_K1, 2 * FFT_N2, 2 * FFT_N2), lambda j: (0, 0, 0),
                         pipeline_mode=pl.Buffered(1)),
        ],
        out_specs=pl.BlockSpec((FFT_K1 * 2 * FFT_N2, dt), lambda j: (0, j)),
        out_shape=jax.ShapeDtypeStruct((FFT_K1 * 2 * FFT_N2, ncol), F32),
        scratch_shapes=[pltpu.VMEM((FFT_N2 * SLAB, dt), F32)],
        compiler_params=_cparams(("parallel",)),
        name="hyena_filter_spectrum",
    )(taps, l1, f1_full, gm)


def _longconv_kernel(z_ref, g_ref, fb_ref, kf_ref, f1_ref, gm_ref, gh_ref, fi_ref, o_ref, s_ref):
    _stage1(z_ref, f1_ref, s_ref, FFT_N1 // 2)

    def mid(k1, c):
        x = jnp.dot(gm_ref[k1], _stage2_rhs(s_ref, k1), preferred_element_type=F32)
        kf = kf_ref[pl.ds(pl.multiple_of(k1 * 2 * FFT_N2, 2 * FFT_N2), 2 * FFT_N2), :]
        xr, xi = x[:FFT_N2], x[FFT_N2:]
        kr, ki = kf[:FFT_N2], kf[FFT_N2:]
        y = jnp.concatenate([xr * kr - xi * ki, xr * ki + xi * kr], axis=0).astype(BF16)
        b = jnp.dot(gh_ref[k1], y, preferred_element_type=F32)
        s_ref[pl.ds(k1, FFT_N2, stride=SLAB), :] = b[:FFT_N2]
        s_ref[pl.ds(FFT_K1 + k1, FFT_N2, stride=SLAB), :] = b[FFT_N2:]
        return c
    lax.fori_loop(0, FFT_K1, mid, 0, unroll=DFT_UNROLL)

    fb = fb_ref[...]

    def last(n2, c):
        rhs = s_ref[pl.ds(pl.multiple_of(n2 * SLAB, SUBLANES), 2 * FFT_K1), :].astype(BF16)
        y = jnp.dot(fi_ref[...], rhs, preferred_element_type=F32)
        idx = pl.ds(n2, FFT_N1 // 2, stride=FFT_N2)
        zt = z_ref[idx, :]
        o_ref[idx, :] = g_ref[idx, :] * (y + fb * zt)
        return c
    lax.fori_loop(0, FFT_N2, last, 0, unroll=DFT_UNROLL)


def _long_conv_gate(z, zcol, g, gcol, fb, kf, kcol, f1_half, gm, gh, fi):
    bsz = z.shape[0]
    dt = CONV_DT
    nt = D_MODEL // dt
    const3 = lambda j, bi: (0, 0, 0)
    return pl.pallas_call(
        _longconv_kernel,
        grid=(nt, bsz),
        in_specs=[
            pl.BlockSpec((None, SEQ, dt), lambda j, bi: (bi, 0, zcol + j)),
            pl.BlockSpec((None, SEQ, dt), lambda j, bi: (bi, 0, gcol + j)),
            pl.BlockSpec((1, dt), lambda j, bi: (0, j)),
            pl.BlockSpec((FFT_K1 * 2 * FFT_N2, dt), lambda j, bi: (0, kcol + j),
                         pipeline_mode=pl.Buffered(1)),
            pl.BlockSpec((2 * FFT_K1, FFT_N1 // 2), lambda j, bi: (0, 0)),
            pl.BlockSpec((FFT_K1, 2 * FFT_N2, 2 * FFT_N2), const3, pipeline_mode=pl.Buffered(1)),
            pl.BlockSpec((FFT_K1, 2 * FFT_N2, 2 * FFT_N2), const3, pipeline_mode=pl.Buffered(1)),
            pl.BlockSpec((FFT_N1 // 2, 2 * FFT_K1), lambda j, bi: (0, 0)),
        ],
        out_specs=pl.BlockSpec((None, SEQ, dt), lambda j, bi: (bi, 0, j)),
        out_shape=jax.ShapeDtypeStruct((bsz, SEQ, D_MODEL), F32),
        scratch_shapes=[pltpu.VMEM((FFT_N2 * SLAB, dt), F32)],
        compiler_params=_cparams(("parallel", "arbitrary")),
        name="hyena_long_conv",
    )(z, g, fb, kf, f1_half, gm, gh, fi)


def _hyena_layer(x, g_mix, w_in, b_in, cw, cb, fw1, fb1, fwi, fbi, fw3, freq, fbias, w_out, b_out, consts):
    f1_full, f1_half, gm, gh, fi, fq, dl2 = consts
    bsz, seq, d = x.shape
    pc = _inproj_conv(x, g_mix[None], w_in.astype(BF16), b_in[None], cw, cb[None])
    taps, l1 = _filter_taps(fq, fw1, fb1[None], fwi, fbi, freq[None], fw3, dl2)
    kf = _filter_spectrum(taps, l1, f1_full, gm)
    nt = D_MODEL // CONV_DT
    z1 = _long_conv_gate(pc, 2 * nt, pc, 0, fbias[0:1], kf, 0, f1_half, gm, gh, fi)
    z2 = _long_conv_gate(z1, 0, pc, nt, fbias[1:2], kf, nt, f1_half, gm, gh, fi)
    y = _matmul_residual(z2.reshape(bsz * seq, d), w_out.astype(BF16), b_out[None],
                         x.reshape(bsz * seq, d))
    return y.reshape(bsz, seq, d)


def _relbias_kernel(tab_ref, bucket_ref, band_ref, o_ref):
    bucket = bucket_ref[...]
    band = band_ref[...]
    for h in range(N_HEADS):
        acc = jnp.zeros(bucket.shape, F32)
        for k in range(REL_BUCKETS):
            acc = jnp.where(bucket == k, tab_ref[k, h], acc)
        o_ref[h] = jnp.where(band > 0, acc, -1e30)


def _rel_bias(rel_table):
    a = jnp.arange(BLOCK)[:, None]
    j = jnp.arange(3 * BLOCK)[None, :]
    rel = j - BLOCK - a
    half = REL_BUCKETS // 2
    exact = half // 2
    n = jnp.abs(rel)
    nf = jnp.maximum(n, 1).astype(F32)
    large = exact + (jnp.log(nf / exact) / math.log(REL_MAX_DIST / exact) * (half - exact)).astype(jnp.int32)
    large = jnp.minimum(large, half - 1)
    bucket = (jnp.where(rel > 0, half, 0) + jnp.where(n < exact, n, large)).astype(jnp.int32)
    band = (n <= WINDOW).astype(jnp.int32)
    return pl.pallas_call(
        _relbias_kernel,
        in_specs=[
            pl.BlockSpec(memory_space=pltpu.SMEM),
            pl.BlockSpec((BLOCK, 3 * BLOCK), lambda: (0, 0)),
            pl.BlockSpec((BLOCK, 3 * BLOCK), lambda: (0, 0)),
        ],
        out_specs=pl.BlockSpec((N_HEADS, BLOCK, 3 * BLOCK), lambda: (0, 0, 0)),
        out_shape=jax.ShapeDtypeStruct((N_HEADS, BLOCK, 3 * BLOCK), F32),
        name="rel_bias",
    )(rel_table, bucket, band)


def _attn_kernel(q_ref, kp_ref, kc_ref, kn_ref, bias_ref, sink_ref, qg_ref, kg_ref, o_ref):
    nb = pl.num_programs(1)
    n = pl.program_id(1)
    q = q_ref[...]
    kv = jnp.concatenate([kp_ref[...], kc_ref[...], kn_ref[...]], axis=0)
    col = lax.broadcasted_iota(jnp.int32, (1, 3 * BLOCK), 1)
    valid = ((col >= BLOCK) | (n > 0)) & ((col < 2 * BLOCK) | (n < nb - 1))
    qg = qg_ref[...] * (HEAD_DIM ** -0.5)
    kg = kg_ref[...]
    outs = []
    for kh in range(N_KV):
        kblk = _rms(kv[:, kh * HEAD_DIM:(kh + 1) * HEAD_DIM], kg).astype(BF16)
        voff = N_KV * HEAD_DIM + kh * HEAD_DIM
        vblk = kv[:, voff:voff + HEAD_DIM].astype(BF16)
        for gi in range(GROUP):
            h = kh * GROUP + gi
            qh = _rms(q[:, h * HEAD_DIM:(h + 1) * HEAD_DIM], qg).astype(BF16)
            logits = lax.dot_general(qh, kblk, (((1,), (1,)), ((), ())), preferred_element_type=F32)
            logits = jnp.where(valid, logits + bias_ref[h], -1e30)
            sink = sink_ref[h:h + 1, 0:1]
            m = jnp.maximum(jnp.max(logits, axis=-1, keepdims=True), sink)
            p = jnp.exp(logits - m)
            den = jnp.sum(p, axis=-1, keepdims=True) + jnp.exp(sink - m)
            o = jnp.dot(p.astype(BF16), vblk, preferred_element_type=F32)
            outs.append(o * (1.0 / den))
    o_ref[...] = jnp.concatenate(outs, axis=-1)


def _attention(qkv, bias, sink_b, q_gain, k_gain):
    bsz, seq, _ = qkv.shape
    nb = seq // BLOCK
    dq = N_HEADS * HEAD_DIM
    dkv = 2 * N_KV * HEAD_DIM
    kvb = dq // dkv
    return pl.pallas_call(
        _attn_kernel,
        grid=(bsz, nb),
        in_specs=[
            pl.BlockSpec((None, BLOCK, dq), lambda b, n: (b, n, 0)),
            pl.BlockSpec((None, BLOCK, dkv), lambda b, n: (b, jnp.maximum(n - 1, 0), kvb)),
            pl.BlockSpec((None, BLOCK, dkv), lambda b, n: (b, n, kvb)),
            pl.BlockSpec((None, BLOCK, dkv), lambda b, n: (b, jnp.minimum(n + 1, nb - 1), kvb)),
            pl.BlockSpec((N_HEADS, BLOCK, 3 * BLOCK), lambda b, n: (0, 0, 0)),
            pl.BlockSpec((N_HEADS, LANES), lambda b, n: (0, 0)),
            pl.BlockSpec((1, HEAD_DIM), lambda b, n: (0, 0)),
            pl.BlockSpec((1, HEAD_DIM), lambda b, n: (0, 0)),
        ],
        out_specs=pl.BlockSpec((None, BLOCK, dq), lambda b, n: (b, n, 0)),
        out_shape=jax.ShapeDtypeStruct((bsz, seq, dq), F32),
        compiler_params=_cparams(("parallel", "parallel")),
        name="window_attention",
    )(qkv, qkv, qkv, qkv, bias, sink_b, q_gain, k_gain)


def _attn_layer(x, g_mix, w_qkv, q_gain, k_gain, sink, w_o, rel_table):
    bsz, seq, d = x.shape
    x2 = x.reshape(bsz * seq, d)
    nq = w_qkv.shape[1]
    qkv = _norm_matmul(x2, g_mix[None], w_qkv.astype(BF16), jnp.zeros((1, nq), F32))
    bias = _rel_bias(rel_table)
    sink_b = jnp.broadcast_to(sink[:, None], (N_HEADS, LANES))
    o = _attention(qkv.reshape(bsz, seq, nq), bias, sink_b, q_gain[None], k_gain[None])
    y = _matmul_residual(o.reshape(bsz * seq, d), w_o.astype(BF16), jnp.zeros((1, d), F32), x2)
    return y.reshape(bsz, seq, d)


def _pool_kernel(xp_ref, xc_ref, xn_ref, g_ref, w_ref, b_ref, sc_ref, o_ref, *, tm):
    i = pl.program_id(1)
    g = g_ref[...]
    rows = tm + 2 * SUBLANES
    t = i * tm - SUBLANES + lax.broadcasted_iota(jnp.int32, (rows, 1), 0)
    inside = (t >= 0) & (t < SEQ)
    xa = jnp.concatenate([_rms(xp_ref[...], g), _rms(xc_ref[...], g), _rms(xn_ref[...], g)], axis=0)
    xa = jnp.where(inside, xa, 0.0)
    tc = t[SUBLANES:SUBLANES + tm]
    outs = []
    for gi, wdt in enumerate(POOL_WINDOWS):
        r = wdt // 2
        xg = xa[:, gi * POOL_GROUP:(gi + 1) * POOL_GROUP]
        run = xg
        span = 1
        while span < 2 * r:
            run = run + pltpu.roll(run, rows - span, 0)
            span *= 2
        win = (pltpu.roll(run, r, 0) + pltpu.roll(xg, rows - r, 0))[SUBLANES:SUBLANES + tm]
        cnt = (jnp.minimum(tc + r + 1, SEQ) - jnp.maximum(tc - r, 0)).astype(F32)
        dlt = win / cnt - xg[SUBLANES:SUBLANES + tm]
        outs.append(jnp.dot(dlt.astype(BF16), w_ref[gi], preferred_element_type=F32))
    y = (jnp.concatenate(outs, axis=-1) + b_ref[...]) * sc_ref[...]
    o_ref[...] = xc_ref[...] + y


def _pool_layer(x, g_mix, w_grp, b, scale, tm=512):
    bsz, seq, d = x.shape
    hb = tm // SUBLANES
    last = seq // SUBLANES - 1
    ng = len(POOL_WINDOWS)
    return pl.pallas_call(
        functools.partial(_pool_kernel, tm=tm),
        grid=(bsz, seq // tm),
        in_specs=[
            pl.BlockSpec((None, SUBLANES, d), lambda bi, i: (bi, jnp.maximum(i * hb - 1, 0), 0)),
            pl.BlockSpec((None, tm, d), lambda bi, i: (bi, i, 0)),
            pl.BlockSpec((None, SUBLANES, d), lambda bi, i: (bi, jnp.minimum((i + 1) * hb, last), 0)),
            pl.BlockSpec((1, d), lambda bi, i: (0, 0)),
            pl.BlockSpec((ng, POOL_GROUP, POOL_GROUP), lambda bi, i: (0, 0, 0)),
            pl.BlockSpec((1, d), lambda bi, i: (0, 0)),
            pl.BlockSpec((1, d), lambda bi, i: (0, 0)),
        ],
        out_specs=pl.BlockSpec((None, tm, d), lambda bi, i: (bi, i, 0)),
        out_shape=jax.ShapeDtypeStruct((bsz, seq, d), F32),
        compiler_params=_cparams(("parallel", "parallel")),
        name="pool_mixer",
    )(x, x, x, g_mix[None], w_grp.astype(BF16), b[None], scale[None])


def kernel(x, norm_mix, norm_ffn, hy_w_in, hy_b_in, hy_conv_w, hy_conv_b, hy_f_w1, hy_f_b1, hy_f_wi, hy_f_bi, hy_f_w3, hy_f_freq, hy_f_bias, hy_w_out, hy_b_out, at_w_qkv, at_q_gain, at_k_gain, at_sink, at_w_o, rel_table, pl_w, pl_b, pl_scale, ff_w_gate, ff_w_up, ff_w_down):
    bsz, seq, d = x.shape
    f1, gm = _dft_mats()
    f1_full = f1.astype(BF16)
    f1_half = f1[:, :FFT_N1 // 2].astype(BF16)
    fi = (f1[:, :FFT_N1 // 2].T * (2.0 / FFT_N)).astype(BF16)
    gh = jnp.swapaxes(gm, 1, 2).astype(BF16)
    gm = gm.astype(BF16)
    fq = jnp.linspace(1e-4, HY_BANDS - 1, HY_BANDS, dtype=F32)[None, :]
    deltas = jnp.abs(jnp.linspace(HY_MIN_DECAY, HY_MAX_DECAY, D_MODEL, dtype=F32))
    dl2 = jnp.concatenate([deltas, deltas])[None, :]
    consts = (f1_full, f1_half, gm, gh, fi, fq, dl2)

    for i in range(DEPTH):
        kind, s = i % 3, i // 3
        if kind == 0:
            x = _hyena_layer(x, norm_mix[i], hy_w_in[s], hy_b_in[s], hy_conv_w[s], hy_conv_b[s],
                             hy_f_w1[s], hy_f_b1[s], hy_f_wi[s], hy_f_bi[s], hy_f_w3[s],
                             hy_f_freq[s], hy_f_bias[s], hy_w_out[s], hy_b_out[s], consts)
        elif kind == 1:
            x = _attn_layer(x, norm_mix[i], at_w_qkv[s], at_q_gain[s], at_k_gain[s], at_sink[s],
                            at_w_o[s], rel_table)
        else:
            x = _pool_layer(x, norm_mix[i], pl_w[s], pl_b[s], pl_scale[s])
        x2 = _ffn(x.reshape(bsz * seq, d), norm_ffn[i][None], ff_w_gate[i].astype(BF16),
                  ff_w_up[i].astype(BF16), ff_w_down[i].astype(BF16))
        x = x2.reshape(bsz, seq, d)
    return x
```

```python
import functools
import math

import jax
import jax.numpy as jnp
from jax import lax
from jax.experimental import pallas as pl
from jax.experimental.pallas import tpu as pltpu

F32 = jnp.float32
BF16 = jnp.bfloat16

D_MODEL = 1024
BATCH = 4
SEQ = 8192
DEPTH = 4
EPS = 1e-6

HY_SHORT = 3
HY_EMB = 33
HY_BANDS = 16
HY_FW = 64
HY_INNER = 2
HY_MIN_DECAY = math.log(1e-2) / 1.5
HY_MAX_DECAY = math.log(1e-2) / 0.3

HEAD_DIM = 64
N_HEADS = 16
N_KV = 4
GROUP = 4
WINDOW = 128
BLOCK = 128
REL_BUCKETS = 32
REL_MAX_DIST = 128

POOL_WINDOWS = (2, 4, 8, 16)
POOL_GROUP = 256
D_FF = 2816

SUBLANES = 8
LANES = 128
VMEM_LIMIT = 56 * 1024 * 1024

FFT_N = 2 * SEQ
FFT_N1 = 256
FFT_N2 = 64
FFT_K1 = FFT_N1 // 2
SLAB = 2 * FFT_K1 + SUBLANES
CONV_DT = 128
DFT_UNROLL = 8


def _cparams(sem):
    return pltpu.CompilerParams(dimension_semantics=sem, vmem_limit_bytes=VMEM_LIMIT)


def _rms(x, g):
    return x * lax.rsqrt(jnp.mean(x * x, axis=-1, keepdims=True) + EPS) * g


def _matmul_res_kernel(a_ref, w_ref, b_ref, x_ref, o_ref):
    acc = jnp.dot(a_ref[...].astype(BF16), w_ref[...], preferred_element_type=F32)
    o_ref[...] = x_ref[...] + acc + b_ref[...]


def _matmul_residual(a2, w, b, x2, tm=1024):
    m, k = a2.shape
    n = w.shape[1]
    return pl.pallas_call(
        _matmul_res_kernel,
        grid=(m // tm,),
        in_specs=[
            pl.BlockSpec((tm, k), lambda i: (i, 0)),
            pl.BlockSpec((k, n), lambda i: (0, 0)),
            pl.BlockSpec((1, n), lambda i: (0, 0)),
            pl.BlockSpec((tm, n), lambda i: (i, 0)),
        ],
        out_specs=pl.BlockSpec((tm, n), lambda i: (i, 0)),
        out_shape=jax.ShapeDtypeStruct((m, n), F32),
        compiler_params=_cparams(("parallel",)),
        name="matmul_residual",
    )(a2, w, b, x2)


FFN_CHUNK = 256


def _ffn_kernel(x_ref, g_ref, wg_ref, wu_ref, wd_ref, o_ref, h_ref):
    x = x_ref[...]
    xn = _rms(x, g_ref[...]).astype(BF16)
    for c in range(D_FF // FFN_CHUNK):
        cols = slice(c * FFN_CHUNK, (c + 1) * FFN_CHUNK)
        gate = jnp.dot(xn, wg_ref[:, cols], preferred_element_type=F32)
        up = jnp.dot(xn, wu_ref[:, cols], preferred_element_type=F32)
        h_ref[:, cols] = (gate * jax.nn.sigmoid(gate) * up).astype(BF16)
    o_ref[...] = x + jnp.dot(h_ref[...], wd_ref[...], preferred_element_type=F32)


def _ffn(x2, g, wg, wu, wd, tm=512):
    m, d = x2.shape
    f = wg.shape[1]
    resident = dict(pipeline_mode=pl.Buffered(1))
    return pl.pallas_call(
        _ffn_kernel,
        grid=(m // tm,),
        in_specs=[
            pl.BlockSpec((tm, d), lambda i: (i, 0)),
            pl.BlockSpec((1, d), lambda i: (0, 0)),
            pl.BlockSpec((d, f), lambda i: (0, 0), **resident),
            pl.BlockSpec((d, f), lambda i: (0, 0), **resident),
            pl.BlockSpec((f, d), lambda i: (0, 0), **resident),
        ],
        out_specs=pl.BlockSpec((tm, d), lambda i: (i, 0)),
        out_shape=jax.ShapeDtypeStruct((m, d), F32),
        scratch_shapes=[pltpu.VMEM((tm, f), BF16)],
        compiler_params=_cparams(("parallel",)),
        name="ffn",
    )(x2, g, wg, wu, wd)


def _shift_rows(cat, tm):
    rows = cat.shape[0]
    dn = pltpu.roll(cat, 1, 0)[SUBLANES:SUBLANES + tm]
    up = pltpu.roll(cat, rows - 1, 0)[SUBLANES:SUBLANES + tm]
    return dn, cat[SUBLANES:SUBLANES + tm], up


def _inproj_kernel(xp_ref, xc_ref, xn_ref, g_ref, w_ref, b_ref, cw_ref, cb_ref, o_ref, xs_ref, *, tm):
    i = pl.program_id(1)

    @pl.when(pl.program_id(2) == 0)
    def _():
        g = g_ref[...]
        xs_ref[0:SUBLANES, :] = _rms(xp_ref[...], g).astype(BF16)
        xs_ref[SUBLANES:SUBLANES + tm, :] = _rms(xc_ref[...], g).astype(BF16)
        xs_ref[SUBLANES + tm:, :] = _rms(xn_ref[...], g).astype(BF16)

    p = jnp.dot(xs_ref[...], w_ref[...], preferred_element_type=F32) + b_ref[...]
    t = i * tm - SUBLANES + lax.broadcasted_iota(jnp.int32, (tm + 2 * SUBLANES, 1), 0)
    p = jnp.where((t >= 0) & (t < SEQ), p, 0.0)
    dn, cur, up = _shift_rows(p, tm)
    cw = cw_ref[...]
    o_ref[...] = cb_ref[...] + cw[0:1] * dn + cw[1:2] * cur + cw[2:3] * up


def _inproj_conv(x, g, w, b, cw, cb, tm=512, tn=1024):
    bsz, seq, d = x.shape
    n = w.shape[1]
    hb = tm // SUBLANES
    last = seq // SUBLANES - 1
    return pl.pallas_call(
        functools.partial(_inproj_kernel, tm=tm),
        grid=(bsz, seq // tm, n // tn),
        in_specs=[
            pl.BlockSpec((None, SUBLANES, d), lambda bi, i, j: (bi, jnp.maximum(i * hb - 1, 0), 0)),
            pl.BlockSpec((None, tm, d), lambda bi, i, j: (bi, i, 0)),
            pl.BlockSpec((None, SUBLANES, d), lambda bi, i, j: (bi, jnp.minimum((i + 1) * hb, last), 0)),
            pl.BlockSpec((1, d), lambda bi, i, j: (0, 0)),
            pl.BlockSpec((d, tn), lambda bi, i, j: (0, j)),
            pl.BlockSpec((1, tn), lambda bi, i, j: (0, j)),
            pl.BlockSpec((HY_SHORT, tn), lambda bi, i, j: (0, j)),
            pl.BlockSpec((1, tn), lambda bi, i, j: (0, j)),
        ],
        out_specs=pl.BlockSpec((None, tm, tn), lambda bi, i, j: (bi, i, j)),
        out_shape=jax.ShapeDtypeStruct((bsz, seq, n), F32),
        scratch_shapes=[pltpu.VMEM((tm + 2 * SUBLANES, d), BF16)],
        compiler_params=_cparams(("parallel", "parallel", "arbitrary")),
        name="hyena_inproj",
    )(x, x, x, g, w, b, cw, cb)


def _dft_mats():
    i32 = jnp.int32
    k1 = jnp.arange(FFT_K1, dtype=i32)[:, None]
    n1 = jnp.arange(FFT_N1, dtype=i32)[None, :]
    m = ((2 * k1 + 1) * n1) % (2 * FFT_N1)
    ang = m.astype(F32) * (math.pi / FFT_N1)
    f1 = jnp.concatenate([jnp.cos(ang), -jnp.sin(ang)], axis=0)
    k1 = jnp.arange(FFT_K1, dtype=i32)[:, None, None]
    k2 = jnp.arange(FFT_N2, dtype=i32)[None, :, None]
    n2 = jnp.arange(FFT_N2, dtype=i32)[None, None, :]
    m = ((2 * (k1 + FFT_N1 * k2) + 1) * n2) % (2 * FFT_N)
    ang = m.astype(F32) * (math.pi / FFT_N)
    c, s = jnp.cos(ang), jnp.sin(ang)
    gm = jnp.concatenate(
        [jnp.concatenate([c, s], axis=2), jnp.concatenate([-s, c], axis=2)], axis=1)
    return f1, gm


def _hdot(a, b):
    return jnp.dot(a, b, preferred_element_type=F32, precision=lax.Precision.HIGHEST)


def _tap_lag(n):
    return jnp.where(n >= SEQ, FFT_N - n, n).astype(F32)


def _split_bf16(a):
    hi = a.astype(BF16)
    return hi, (a - hi.astype(F32)).astype(BF16)


def _filter_mlp_kernel(fq_ref, w1t_ref, w1c_ref, w1s_ref, b1_ref, wi_ref, bi_ref, fr_ref,
                       hi_ref, lo_ref, *, slabs):
    cols = slabs * FFT_N1
    c = lax.broadcasted_iota(jnp.int32, (1, cols), 1)
    n2 = pl.program_id(0) * slabs + c // FFT_N1
    pos = _tap_lag(FFT_N2 * (c % FFT_N1) + n2)
    t = pos * (1.0 / (SEQ - 1))
    ang = (2.0 * math.pi / SEQ) * pos * fq_ref[...]
    fr = fr_ref[...]
    pre = (w1t_ref[...] * t + _hdot(w1c_ref[...], jnp.cos(ang))
           + _hdot(w1s_ref[...], -jnp.sin(ang)))
    h = jnp.sin(fr * (pre + b1_ref[...]))
    for l in range(HY_INNER):
        h = jnp.sin(fr * (_hdot(wi_ref[l], h) + bi_ref[l]))
    hi, lo = _split_bf16(h)
    for k in range(slabs):
        hi_ref[k] = hi[:, k * FFT_N1:(k + 1) * FFT_N1]
        lo_ref[k] = lo[:, k * FFT_N1:(k + 1) * FFT_N1]


def _filter_mlp(fq, w1, b1, wi, bi, fr, slabs=8):
    w1t, w1c, w1s = w1[0:1].T, w1[1:1 + HY_BANDS].T, w1[1 + HY_BANDS:].T
    col = lambda a: a.reshape(-1, 1)
    const2 = lambda i: (0, 0)
    const3 = lambda i: (0, 0, 0)
    out = jax.ShapeDtypeStruct((FFT_N2, HY_FW, FFT_N1), BF16)
    return pl.pallas_call(
        functools.partial(_filter_mlp_kernel, slabs=slabs),
        grid=(FFT_N2 // slabs,),
        in_specs=[
            pl.BlockSpec((HY_BANDS, 1), const2),
            pl.BlockSpec((HY_FW, 1), const2),
            pl.BlockSpec((HY_FW, HY_BANDS), const2),
            pl.BlockSpec((HY_FW, HY_BANDS), const2),
            pl.BlockSpec((HY_FW, 1), const2),
            pl.BlockSpec((HY_INNER, HY_FW, HY_FW), const3),
            pl.BlockSpec((HY_INNER, HY_FW, 1), const3),
            pl.BlockSpec((HY_FW, 1), const2),
        ],
        out_specs=[
            pl.BlockSpec((slabs, HY_FW, FFT_N1), lambda i: (i, 0, 0)),
            pl.BlockSpec((slabs, HY_FW, FFT_N1), lambda i: (i, 0, 0)),
        ],
        out_shape=[out, out],
        compiler_params=_cparams(("parallel",)),
        name="hyena_filter_mlp",
    )(col(fq), w1t, w1c, w1s, col(b1), jnp.swapaxes(wi, 1, 2), bi[:, :, None], col(fr))


def _stage1(src_ref, f1_ref, s_ref, rows):
    def body(n2, c):
        zr = src_ref[pl.ds(n2, rows, stride=FFT_N2), :].astype(BF16)
        a = jnp.dot(f1_ref[...], zr, preferred_element_type=F32)
        s_ref[pl.ds(pl.multiple_of(n2 * SLAB, SUBLANES), 2 * FFT_K1), :] = a
        return c
    lax.fori_loop(0, FFT_N2, body, 0, unroll=DFT_UNROLL)


def _stage2_rhs(s_ref, k1):
    re = s_ref[pl.ds(k1, FFT_N2, stride=SLAB), :]
    im = s_ref[pl.ds(FFT_K1 + k1, FFT_N2, stride=SLAB), :]
    return jnp.concatenate([re, im], axis=0).astype(BF16)


def _filt_dft_kernel(hi_ref, lo_ref, wfh_ref, wfl_ref, wbh_ref, wbl_ref, dl_ref, f1_ref, gm_ref,
                     kf_ref, s_ref):
    half = FFT_N1 // 2
    n1 = lax.broadcasted_iota(jnp.int32, (FFT_N1, 1), 0)
    dl = dl_ref[...]
    contract0 = (((0,), (0,)), ((), ()))

    def taps_of(h_hi, h_lo, wh_ref, wl_ref):
        dot = lambda a, b: lax.dot_general(a, b, contract0, preferred_element_type=F32)
        return dot(h_hi, wh_ref[...]) + dot(h_lo, wh_ref[...]) + dot(h_hi, wl_ref[...])

    def body(n2, l1):
        h_hi, h_lo = hi_ref[n2], lo_ref[n2]
        fwd = taps_of(h_hi[:, :half], h_lo[:, :half], wfh_ref, wfl_ref)
        bwd = taps_of(h_hi[:, half:], h_lo[:, half:], wbh_ref, wbl_ref)
        n = FFT_N2 * n1 + n2
        t = _tap_lag(n) * (1.0 / (SEQ - 1))
        taps = jnp.concatenate([fwd, -bwd], axis=0) * jnp.exp(-t * dl)
        taps = jnp.where(n == SEQ, 0.0, taps)
        a = jnp.dot(f1_ref[...], taps.astype(BF16), preferred_element_type=F32)
        s_ref[pl.ds(pl.multiple_of(n2 * SLAB, SUBLANES), 2 * FFT_K1), :] = a
        return l1 + jnp.sum(jnp.abs(taps), axis=0, keepdims=True)
    l1 = lax.fori_loop(0, FFT_N2, body, jnp.zeros(dl.shape, F32), unroll=DFT_UNROLL)
    inv = 1.0 / l1

    def stage2(k1, c):
        x = jnp.dot(gm_ref[k1], _stage2_rhs(s_ref, k1), preferred_element_type=F32)
        kf_ref[pl.ds(pl.multiple_of(k1 * 2 * FFT_N2, 2 * FFT_N2), 2 * FFT_N2), :] = x * inv
        return c
    lax.fori_loop(0, FFT_K1, stage2, 0, unroll=DFT_UNROLL)


def _filter_spectrum(h_hi, h_lo, w3, dl2, f1_full, gm):
    ncol = 2 * D_MODEL
    dt = CONV_DT
    nct = ncol // dt
    w3h, w3l = _split_bf16(w3)
    hspec = pl.BlockSpec((FFT_N2, HY_FW, FFT_N1), lambda j: (0, 0, 0))
    fwd = pl.BlockSpec((HY_FW, dt), lambda j: (0, j))
    bwd = pl.BlockSpec((HY_FW, dt), lambda j: (0, nct + j))
    return pl.pallas_call(
        _filt_dft_kernel,
        grid=(nct,),
        in_specs=[
            hspec, hspec, fwd, fwd, bwd, bwd,
            pl.BlockSpec((1, dt), lambda j: (0, j)),
            pl.BlockSpec((2 * FFT_K1, FFT_N1), lambda j: (0, 0)),
            pl.BlockSpec((FFT_K1, 2 * FFT_N2, 2 * FFT_N2), lambda j: (0, 0, 0),
                         pipeline_mode=pl.Buffered(1)),
        ],
        out_specs=pl.BlockSpec((FFT_K1 * 2 * FFT_N2, dt), lambda j: (0, j)),
        out_shape=jax.ShapeDtypeStruct((FFT_K1 * 2 * FFT_N2, ncol), F32),
        scratch_shapes=[pltpu.VMEM((FFT_N2 * SLAB, dt), F32)],
        compiler_params=_cparams(("parallel",)),
        name="hyena_filter_spectrum",
    )(h_hi, h_lo, w3h, w3l, w3h, w3l, dl2, f1_full, gm)


def _longconv_kernel(z_ref, g_ref, fb_ref, kf_ref, f1_ref, gm_ref, gh_ref, fi_ref, o_ref, s_ref):
    _stage1(z_ref, f1_ref, s_ref, FFT_N1 // 2)

    def mid(k1, c):
        x = jnp.dot(gm_ref[k1], _stage2_rhs(s_ref, k1), preferred_element_type=F32)
        kf = kf_ref[pl.ds(pl.multiple_of(k1 * 2 * FFT_N2, 2 * FFT_N2), 2 * FFT_N2), :]
        xr, xi = x[:FFT_N2], x[FFT_N2:]
        kr, ki = kf[:FFT_N2], kf[FFT_N2:]
        y = jnp.concatenate([xr * kr - xi * ki, xr * ki + xi * kr], axis=0).astype(BF16)
        b = jnp.dot(gh_ref[k1], y, preferred_element_type=F32)
        s_ref[pl.ds(k1, FFT_N2, stride=SLAB), :] = b[:FFT_N2]
        s_ref[pl.ds(FFT_K1 + k1, FFT_N2, stride=SLAB), :] = b[FFT_N2:]
        return c
    lax.fori_loop(0, FFT_K1, mid, 0, unroll=DFT_UNROLL)

    fb = fb_ref[...]

    def last(n2, c):
        rhs = s_ref[pl.ds(pl.multiple_of(n2 * SLAB, SUBLANES), 2 * FFT_K1), :].astype(BF16)
        y = jnp.dot(fi_ref[...], rhs, preferred_element_type=F32)
        idx = pl.ds(n2, FFT_N1 // 2, stride=FFT_N2)
        zt = z_ref[idx, :]
        o_ref[idx, :] = g_ref[idx, :] * (y + fb * zt)
        return c
    lax.fori_loop(0, FFT_N2, last, 0, unroll=DFT_UNROLL)


def _long_conv_gate(z, zcol, g, gcol, fb, kf, kcol, f1_half, gm, gh, fi):
    bsz = z.shape[0]
    dt = CONV_DT
    nt = D_MODEL // dt
    const3 = lambda j, bi: (0, 0, 0)
    return pl.pallas_call(
        _longconv_kernel,
        grid=(nt, bsz),
        in_specs=[
            pl.BlockSpec((None, SEQ, dt), lambda j, bi: (bi, 0, zcol + j)),
            pl.BlockSpec((None, SEQ, dt), lambda j, bi: (bi, 0, gcol + j)),
            pl.BlockSpec((1, dt), lambda j, bi: (0, j)),
            pl.BlockSpec((FFT_K1 * 2 * FFT_N2, dt), lambda j, bi: (0, kcol + j),
                         pipeline_mode=pl.Buffered(1)),
            pl.BlockSpec((2 * FFT_K1, FFT_N1 // 2), lambda j, bi: (0, 0)),
            pl.BlockSpec((FFT_K1, 2 * FFT_N2, 2 * FFT_N2), const3, pipeline_mode=pl.Buffered(1)),
            pl.BlockSpec((FFT_K1, 2 * FFT_N2, 2 * FFT_N2), const3, pipeline_mode=pl.Buffered(1)),
            pl.BlockSpec((FFT_N1 // 2, 2 * FFT_K1), lambda j, bi: (0, 0)),
        ],
        out_specs=pl.BlockSpec((None, SEQ, dt), lambda j, bi: (bi, 0, j)),
        out_shape=jax.ShapeDtypeStruct((bsz, SEQ, D_MODEL), F32),
        scratch_shapes=[pltpu.VMEM((FFT_N2 * SLAB, dt), F32)],
        compiler_params=_cparams(("parallel", "arbitrary")),
        name="hyena_long_conv",
    )(z, g, fb, kf, f1_half, gm, gh, fi)


def _hyena_layer(x, g_mix, w_in, b_in, cw, cb, fw1, fb1, fwi, fbi, fw3, freq, fbias, w_out, b_out, consts):
    f1_full, f1_half, gm, gh, fi, fq, dl2 = consts
    bsz, seq, d = x.shape
    pc = _inproj_conv(x, g_mix[None], w_in.astype(BF16), b_in[None], cw, cb[None])
    h_hi, h_lo = _filter_mlp(fq, fw1, fb1, fwi, fbi, freq)
    kf = _filter_spectrum(h_hi, h_lo, fw3, dl2, f1_full, gm)
    nt = D_MODEL // CONV_DT
    z1 = _long_conv_gate(pc, 2 * nt, pc, 0, fbias[0:1], kf, 0, f1_half, gm, gh, fi)
    z2 = _long_conv_gate(z1, 0, pc, nt, fbias[1:2], kf, nt, f1_half, gm, gh, fi)
    y = _matmul_residual(z2.reshape(bsz * seq, d), w_out.astype(BF16), b_out[None],
                         x.reshape(bsz * seq, d))
    return y.reshape(bsz, seq, d)


def _relbias_kernel(tab_ref, bucket_ref, band_ref, o_ref):
    bucket = bucket_ref[...]
    band = band_ref[...]
    for h in range(N_HEADS):
        acc = jnp.zeros(bucket.shape, F32)
        for k in range(REL_BUCKETS):
            acc = jnp.where(bucket == k, tab_ref[k, h], acc)
        o_ref[h] = jnp.where(band > 0, acc, -1e30)


def _rel_bias(rel_table):
    a = jnp.arange(BLOCK)[:, None]
    j = jnp.arange(3 * BLOCK)[None, :]
    rel = j - BLOCK - a
    half = REL_BUCKETS // 2
    exact = half // 2
    n = jnp.abs(rel)
    nf = jnp.maximum(n, 1).astype(F32)
    large = exact + (jnp.log(nf / exact) / math.log(REL_MAX_DIST / exact) * (half - exact)).astype(jnp.int32)
    large = jnp.minimum(large, half - 1)
    bucket = (jnp.where(rel > 0, half, 0) + jnp.where(n < exact, n, large)).astype(jnp.int32)
    band = (n <= WINDOW).astype(jnp.int32)
    return pl.pallas_call(
        _relbias_kernel,
        in_specs=[
            pl.BlockSpec(memory_space=pltpu.SMEM),
            pl.BlockSpec((BLOCK, 3 * BLOCK), lambda: (0, 0)),
            pl.BlockSpec((BLOCK, 3 * BLOCK), lambda: (0, 0)),
        ],
        out_specs=pl.BlockSpec((N_HEADS, BLOCK, 3 * BLOCK), lambda: (0, 0, 0)),
        out_shape=jax.ShapeDtypeStruct((N_HEADS, BLOCK, 3 * BLOCK), F32),
        name="rel_bias",
    )(rel_table, bucket, band)


def _qkv_kernel(x_ref, g_ref, w_ref, gain_ref, flag_ref, bd_ref, o_ref, xn_ref):
    @pl.when(pl.program_id(1) == 0)
    def _():
        xn_ref[...] = _rms(x_ref[...], g_ref[...]).astype(BF16)

    acc = jnp.dot(xn_ref[...], w_ref[...], preferred_element_type=F32)
    hi, lo = _split_bf16(acc * acc)
    ssq = (jnp.dot(hi, bd_ref[...], preferred_element_type=F32)
           + jnp.dot(lo, bd_ref[...], preferred_element_type=F32))
    normed = acc * lax.rsqrt(ssq * (1.0 / HEAD_DIM) + EPS) * gain_ref[...]
    o_ref[...] = jnp.where(flag_ref[...] > 0, normed, acc).astype(o_ref.dtype)


def _qkv_proj(x2, g, w, gain, flag, tm=1024, tn=512):
    m, k = x2.shape
    n = w.shape[1]
    eye = jnp.arange(tn)[:, None] // HEAD_DIM == jnp.arange(tn)[None, :] // HEAD_DIM
    return pl.pallas_call(
        _qkv_kernel,
        grid=(m // tm, n // tn),
        in_specs=[
            pl.BlockSpec((tm, k), lambda i, j: (i, 0)),
            pl.BlockSpec((1, k), lambda i, j: (0, 0)),
            pl.BlockSpec((k, tn), lambda i, j: (0, j)),
            pl.BlockSpec((1, tn), lambda i, j: (0, j)),
            pl.BlockSpec((1, tn), lambda i, j: (0, j)),
            pl.BlockSpec((tn, tn), lambda i, j: (0, 0)),
        ],
        out_specs=pl.BlockSpec((tm, tn), lambda i, j: (i, j)),
        out_shape=jax.ShapeDtypeStruct((m, n), BF16),
        scratch_shapes=[pltpu.VMEM((tm, k), BF16)],
        compiler_params=_cparams(("parallel", "arbitrary")),
        name="qkv_proj",
    )(x2, g, w, gain, flag, eye.astype(BF16))


def _attn_kernel(q_ref, kp_ref, kc_ref, kn_ref, bias_ref, sink_ref, o_ref):
    nb = pl.num_programs(1)
    n = pl.program_id(1)
    kv = jnp.concatenate([kp_ref[...], kc_ref[...], kn_ref[...]], axis=0)
    key = lax.broadcasted_iota(jnp.int32, (3 * BLOCK, GROUP * BLOCK), 0)
    valid = ((key >= BLOCK) | (n > 0)) & ((key < 2 * BLOCK) | (n < nb - 1))
    lane = lax.broadcasted_iota(jnp.int32, (1, LANES), 1)
    keep = [(lane < HEAD_DIM).astype(F32).astype(BF16), (lane >= HEAD_DIM).astype(F32).astype(BF16)]
    ntile = N_KV // 2
    for j in range(ntile):
        q4 = jnp.concatenate(
            [q_ref[:, (GROUP * j + g) * LANES:(GROUP * j + g + 1) * LANES] for g in range(GROUP)], axis=0)
        kt = kv[:, j * LANES:(j + 1) * LANES]
        vt = kv[:, (ntile + j) * LANES:(ntile + j + 1) * LANES]
        acc = None
        for half in range(2):
            kh = 2 * j + half
            logits = lax.dot_general(kt * keep[half], q4, (((1,), (1,)), ((), ())),
                                     preferred_element_type=F32)
            logits = jnp.where(valid, logits + bias_ref[kh], -1e30)
            sink = sink_ref[kh]
            m = jnp.maximum(jnp.max(logits, axis=0, keepdims=True), sink)
            p = jnp.exp(logits - m)
            den = jnp.sum(p, axis=0, keepdims=True) + jnp.exp(sink - m)
            o = lax.dot_general(vt * keep[half], p.astype(BF16), (((0,), (0,)), ((), ())),
                                preferred_element_type=F32) * (1.0 / den)
            acc = o if acc is None else acc + o
        for g in range(GROUP):
            o_ref[(GROUP * j + g) * LANES:(GROUP * j + g + 1) * LANES, :] = (
                acc[:, g * BLOCK:(g + 1) * BLOCK].astype(o_ref.dtype))


def _attention(qkv, bias_t, sink_row):
    bsz, seq, _ = qkv.shape
    nb = seq // BLOCK
    dq = N_HEADS * HEAD_DIM
    dkv = 2 * N_KV * HEAD_DIM
    kvb = dq // dkv
    return pl.pallas_call(
        _attn_kernel,
        grid=(bsz, nb),
        in_specs=[
            pl.BlockSpec((None, BLOCK, dq), lambda b, n: (b, n, 0)),
            pl.BlockSpec((None, BLOCK, dkv), lambda b, n: (b, jnp.maximum(n - 1, 0), kvb)),
            pl.BlockSpec((None, BLOCK, dkv), lambda b, n: (b, n, kvb)),
            pl.BlockSpec((None, BLOCK, dkv), lambda b, n: (b, jnp.minimum(n + 1, nb - 1), kvb)),
            pl.BlockSpec((N_KV, 3 * BLOCK, GROUP * BLOCK), lambda b, n: (0, 0, 0)),
            pl.BlockSpec((N_KV, 1, GROUP * BLOCK), lambda b, n: (0, 0, 0)),
        ],
        out_specs=pl.BlockSpec((None, None, dq, BLOCK), lambda b, n: (b, n, 0, 0)),
        out_shape=jax.ShapeDtypeStruct((bsz, nb, dq, BLOCK), BF16),
        compiler_params=_cparams(("parallel", "parallel")),
        name="window_attention",
    )(qkv, qkv, qkv, qkv, bias_t, sink_row)


ATTN_OUT_BLOCKS = 8


def _attn_out_kernel(ot_ref, w_ref, x_ref, o_ref):
    for i in range(ATTN_OUT_BLOCKS):
        rows = slice(i * BLOCK, (i + 1) * BLOCK)
        y = lax.dot_general(ot_ref[i], w_ref[...], (((0,), (0,)), ((), ())),
                            preferred_element_type=F32)
        o_ref[rows, :] = x_ref[rows, :] + y


def _attn_out_proj(o_t, w, x2):
    nblk, dq, _ = o_t.shape
    m, d = x2.shape
    tm = ATTN_OUT_BLOCKS * BLOCK
    return pl.pallas_call(
        _attn_out_kernel,
        grid=(nblk // ATTN_OUT_BLOCKS,),
        in_specs=[
            pl.BlockSpec((ATTN_OUT_BLOCKS, dq, BLOCK), lambda i: (i, 0, 0)),
            pl.BlockSpec((dq, d), lambda i: (0, 0)),
            pl.BlockSpec((tm, d), lambda i: (i, 0)),
        ],
        out_specs=pl.BlockSpec((tm, d), lambda i: (i, 0)),
        out_shape=jax.ShapeDtypeStruct((m, d), F32),
        compiler_params=_cparams(("parallel",)),
        name="attn_out_proj",
    )(o_t, w, x2)


def _head_tile_perm():
    cols = []
    for tile in range(N_HEADS // 2):
        j, g = tile // GROUP, tile % GROUP
        for half in range(2):
            h = (2 * j + half) * GROUP + g
            cols.extend(range(h * HEAD_DIM, (h + 1) * HEAD_DIM))
    return jnp.asarray(cols, jnp.int32)


def _attn_layer(x, g_mix, w_qkv, q_gain, k_gain, sink, w_o, rel_table):
    bsz, seq, d = x.shape
    x2 = x.reshape(bsz * seq, d)
    dq = N_HEADS * HEAD_DIM
    dk = N_KV * HEAD_DIM
    perm = _head_tile_perm()
    w = jnp.concatenate([w_qkv[:, :dq][:, perm], w_qkv[:, dq:]], axis=1).astype(BF16)
    gain = jnp.concatenate([jnp.tile(q_gain * (HEAD_DIM ** -0.5), N_HEADS), jnp.tile(k_gain, N_KV),
                            jnp.ones((dk,), F32)])[None]
    flag = jnp.concatenate([jnp.ones((dq + dk,), F32), jnp.zeros((dk,), F32)])[None]
    qkv = _qkv_proj(x2, g_mix[None], w, gain, flag)
    bias_t = _rel_bias(rel_table).reshape(N_KV, GROUP * BLOCK, 3 * BLOCK).transpose(0, 2, 1)
    sink_row = jnp.repeat(sink, BLOCK).reshape(N_KV, 1, GROUP * BLOCK)
    o_t = _attention(qkv.reshape(bsz, seq, dq + 2 * dk), bias_t, sink_row)
    y = _attn_out_proj(o_t.reshape(bsz * (seq // BLOCK), dq, BLOCK), w_o[perm, :].astype(BF16), x2)
    return y.reshape(bsz, seq, d)


def _pool_kernel(xp_ref, xc_ref, xn_ref, g_ref, w_ref, b_ref, sc_ref, o_ref, *, tm):
    i = pl.program_id(1)
    g = g_ref[...]
    rows = tm + 2 * SUBLANES
    t = i * tm - SUBLANES + lax.broadcasted_iota(jnp.int32, (rows, 1), 0)
    inside = (t >= 0) & (t < SEQ)
    xa = jnp.concatenate([_rms(xp_ref[...], g), _rms(xc_ref[...], g), _rms(xn_ref[...], g)], axis=0)
    xa = jnp.where(inside, xa, 0.0)
    tc = t[SUBLANES:SUBLANES + tm]
    outs = []
    for gi, wdt in enumerate(POOL_WINDOWS):
        r = wdt // 2
        xg = xa[:, gi * POOL_GROUP:(gi + 1) * POOL_GROUP]
        run = xg
        span = 1
        while span < 2 * r:
            run = run + pltpu.roll(run, rows - span, 0)
            span *= 2
        win = (pltpu.roll(run, r, 0) + pltpu.roll(xg, rows - r, 0))[SUBLANES:SUBLANES + tm]
        cnt = (jnp.minimum(tc + r + 1, SEQ) - jnp.maximum(tc - r, 0)).astype(F32)
        dlt = win / cnt - xg[SUBLANES:SUBLANES + tm]
        outs.append(jnp.dot(dlt.astype(BF16), w_ref[gi], preferred_element_type=F32))
    y = (jnp.concatenate(outs, axis=-1) + b_ref[...]) * sc_ref[...]
    o_ref[...] = xc_ref[...] + y


def _pool_layer(x, g_mix, w_grp, b, scale, tm=512):
    bsz, seq, d = x.shape
    hb = tm // SUBLANES
    last = seq // SUBLANES - 1
    ng = len(POOL_WINDOWS)
    return pl.pallas_call(
        functools.partial(_pool_kernel, tm=tm),
        grid=(bsz, seq // tm),
        in_specs=[
            pl.BlockSpec((None, SUBLANES, d), lambda bi, i: (bi, jnp.maximum(i * hb - 1, 0), 0)),
            pl.BlockSpec((None, tm, d), lambda bi, i: (bi, i, 0)),
            pl.BlockSpec((None, SUBLANES, d), lambda bi, i: (bi, jnp.minimum((i + 1) * hb, last), 0)),
            pl.BlockSpec((1, d), lambda bi, i: (0, 0)),
            pl.BlockSpec((ng, POOL_GROUP, POOL_GROUP), lambda bi, i: (0, 0, 0)),
            pl.BlockSpec((1, d), lambda bi, i: (0, 0)),
            pl.BlockSpec((1, d), lambda bi, i: (0, 0)),
        ],
        out_specs=pl.BlockSpec((None, tm, d), lambda bi, i: (bi, i, 0)),
        out_shape=jax.ShapeDtypeStruct((bsz, seq, d), F32),
        compiler_params=_cparams(("parallel", "parallel")),
        name="pool_mixer",
    )(x, x, x, g_mix[None], w_grp.astype(BF16), b[None], scale[None])


def kernel(x, norm_mix, norm_ffn, hy_w_in, hy_b_in, hy_conv_w, hy_conv_b, hy_f_w1, hy_f_b1, hy_f_wi, hy_f_bi, hy_f_w3, hy_f_freq, hy_f_bias, hy_w_out, hy_b_out, at_w_qkv, at_q_gain, at_k_gain, at_sink, at_w_o, rel_table, pl_w, pl_b, pl_scale, ff_w_gate, ff_w_up, ff_w_down):
    bsz, seq, d = x.shape
    f1, gm = _dft_mats()
    f1_full = f1.astype(BF16)
    f1_half = f1[:, :FFT_N1 // 2].astype(BF16)
    fi = (f1[:, :FFT_N1 // 2].T * (2.0 / FFT_N)).astype(BF16)
    gh = jnp.swapaxes(gm, 1, 2).astype(BF16)
    gm = gm.astype(BF16)
    fq = jnp.linspace(1e-4, HY_BANDS - 1, HY_BANDS, dtype=F32)[None, :]
    deltas = jnp.abs(jnp.linspace(HY_MIN_DECAY, HY_MAX_DECAY, D_MODEL, dtype=F32))
    dl2 = jnp.concatenate([deltas, deltas])[None, :]
    consts = (f1_full, f1_half, gm, gh, fi, fq, dl2)

    for i in range(DEPTH):
        kind, s = i % 3, i // 3
        if kind == 0:
            x = _hyena_layer(x, norm_mix[i], hy_w_in[s], hy_b_in[s], hy_conv_w[s], hy_conv_b[s],
                             hy_f_w1[s], hy_f_b1[s], hy_f_wi[s], hy_f_bi[s], hy_f_w3[s],
                             hy_f_freq[s], hy_f_bias[s], hy_w_out[s], hy_b_out[s], consts)
        elif kind == 1:
            x = _attn_layer(x, norm_mix[i], at_w_qkv[s], at_q_gain[s], at_k_gain[s], at_sink[s],
                            at_w_o[s], rel_table)
        else:
            x = _pool_layer(x, norm_mix[i], pl_w[s], pl_b[s], pl_scale[s])
        x2 = _ffn(x.reshape(bsz * seq, d), norm_ffn[i][None], ff_w_gate[i].astype(BF16),
                  ff_w_up[i].astype(BF16), ff_w_down[i].astype(BF16))
        x = x2.reshape(bsz, seq, d)
    return x
```

```python
import functools
import math

import jax
import jax.numpy as jnp
from jax import lax
from jax.experimental import pallas as pl
from jax.experimental.pallas import tpu as pltpu

F32 = jnp.float32
BF16 = jnp.bfloat16

D_MODEL = 1024
BATCH = 4
SEQ = 8192
DEPTH = 4
EPS = 1e-6

HY_SHORT = 3
HY_EMB = 33
HY_BANDS = 16
HY_FW = 64
HY_INNER = 2
HY_MIN_DECAY = math.log(1e-2) / 1.5
HY_MAX_DECAY = math.log(1e-2) / 0.3

HEAD_DIM = 64
N_HEADS = 16
N_KV = 4
GROUP = 4
WINDOW = 128
BLOCK = 128
REL_BUCKETS = 32
REL_MAX_DIST = 128

POOL_WINDOWS = (2, 4, 8, 16)
POOL_GROUP = 256
D_FF = 2816

SUBLANES = 8
LANES = 128
VMEM_LIMIT = 56 * 1024 * 1024

FFT_N = 2 * SEQ
FFT_N1 = 256
FFT_N2 = 64
FFT_K1 = FFT_N1 // 2
SLAB = 2 * FFT_K1 + SUBLANES
CONV_DT = 128
DFT_UNROLL = 16


def _cparams(sem):
    return pltpu.CompilerParams(dimension_semantics=sem, vmem_limit_bytes=VMEM_LIMIT)


def _rms(x, g):
    return x * lax.rsqrt(jnp.mean(x * x, axis=-1, keepdims=True) + EPS) * g


FFN_CHUNK = 256


def _ffn_kernel(x_ref, g_ref, wg_ref, wu_ref, wd_ref, o_ref, h_ref):
    x = x_ref[...]
    xn = _rms(x, g_ref[...]).astype(BF16)
    for c in range(D_FF // FFN_CHUNK):
        cols = slice(c * FFN_CHUNK, (c + 1) * FFN_CHUNK)
        gate = jnp.dot(xn, wg_ref[:, cols], preferred_element_type=F32)
        up = jnp.dot(xn, wu_ref[:, cols], preferred_element_type=F32)
        h_ref[:, cols] = (gate * jax.nn.sigmoid(gate) * up).astype(BF16)
    o_ref[...] = x + jnp.dot(h_ref[...], wd_ref[...], preferred_element_type=F32)


def _ffn(x2, g, wg, wu, wd, tm=512):
    m, d = x2.shape
    f = wg.shape[1]
    resident = dict(pipeline_mode=pl.Buffered(1))
    return pl.pallas_call(
        _ffn_kernel,
        grid=(m // tm,),
        in_specs=[
            pl.BlockSpec((tm, d), lambda i: (i, 0)),
            pl.BlockSpec((1, d), lambda i: (0, 0)),
            pl.BlockSpec((d, f), lambda i: (0, 0), **resident),
            pl.BlockSpec((d, f), lambda i: (0, 0), **resident),
            pl.BlockSpec((f, d), lambda i: (0, 0), **resident),
        ],
        out_specs=pl.BlockSpec((tm, d), lambda i: (i, 0)),
        out_shape=jax.ShapeDtypeStruct((m, d), F32),
        scratch_shapes=[pltpu.VMEM((tm, f), BF16)],
        compiler_params=_cparams(("parallel",)),
        name="ffn",
    )(x2, g, wg, wu, wd)


INPROJ_N1 = 8


def _slab_perm():
    nb = INPROJ_N1
    r = jnp.arange(FFT_N2 * nb)
    src = FFT_N2 * (r % nb) + r // nb
    return (src[:, None] == r[None, :]).astype(BF16)


def _inproj_kernel(xp_ref, xc_ref, xn_ref, g_ref, pm_ref, w_ref, b_ref, cw_ref, cb_ref, o_ref, xs_ref):
    i = pl.program_id(1)
    nb = INPROJ_N1
    tm = FFT_N2 * nb

    @pl.when(pl.program_id(2) == 0)
    def _():
        g = g_ref[...]
        xn = _rms(xc_ref[...], g).astype(BF16)
        xs_ref[:tm, :] = jnp.dot(pm_ref[...], xn, preferred_element_type=F32).astype(BF16)
        halo = jnp.concatenate([xp_ref[...], xn_ref[...]], axis=0)
        xs_ref[tm:, :] = _rms(halo, g).astype(BF16)

    p = jnp.dot(xs_ref[...], w_ref[...], preferred_element_type=F32) + b_ref[...]
    cur = p[:tm]
    before = jnp.where(i > 0, p[tm + SUBLANES - 1:tm + SUBLANES], 0.0)
    after = jnp.where(i < pl.num_programs(1) - 1, p[tm + SUBLANES:tm + SUBLANES + 1], 0.0)
    row = lax.broadcasted_iota(jnp.int32, (nb, 1), 0)
    first = jnp.where(row == 0, before, pltpu.roll(cur[tm - nb:], 1, 0))
    last = jnp.where(row == nb - 1, after, pltpu.roll(cur[:nb], nb - 1, 0))
    dn = jnp.concatenate([first, cur[:tm - nb]], axis=0)
    up = jnp.concatenate([cur[nb:], last], axis=0)
    cw = cw_ref[...]
    out = cb_ref[...] + cw[0:1] * dn + cw[1:2] * cur + cw[2:3] * up
    o_ref[...] = out.reshape(o_ref.shape)


def _inproj_conv(x, g, w, b, cw, cb, tn=1024):
    bsz, seq, d = x.shape
    n = w.shape[1]
    nb = INPROJ_N1
    tm = FFT_N2 * nb
    hb = tm // SUBLANES
    last = seq // SUBLANES - 1
    return pl.pallas_call(
        _inproj_kernel,
        grid=(bsz, seq // tm, n // tn),
        in_specs=[
            pl.BlockSpec((None, SUBLANES, d), lambda bi, i, j: (bi, jnp.maximum(i * hb - 1, 0), 0)),
            pl.BlockSpec((None, tm, d), lambda bi, i, j: (bi, i, 0)),
            pl.BlockSpec((None, SUBLANES, d), lambda bi, i, j: (bi, jnp.minimum((i + 1) * hb, last), 0)),
            pl.BlockSpec((1, d), lambda bi, i, j: (0, 0)),
            pl.BlockSpec((tm, tm), lambda bi, i, j: (0, 0)),
            pl.BlockSpec((d, tn), lambda bi, i, j: (0, j)),
            pl.BlockSpec((1, tn), lambda bi, i, j: (0, j)),
            pl.BlockSpec((HY_SHORT, tn), lambda bi, i, j: (0, j)),
            pl.BlockSpec((1, tn), lambda bi, i, j: (0, j)),
        ],
        out_specs=pl.BlockSpec((None, FFT_N2, nb, tn), lambda bi, i, j: (bi, 0, i, j)),
        out_shape=jax.ShapeDtypeStruct((bsz, FFT_N2, seq // FFT_N2, n), F32),
        scratch_shapes=[pltpu.VMEM((tm + 2 * SUBLANES, d), BF16)],
        compiler_params=_cparams(("parallel", "parallel", "arbitrary")),
        name="hyena_inproj",
    )(x, x, x, g, _slab_perm(), w, b, cw, cb)


def _dft_mats():
    i32 = jnp.int32
    k1 = jnp.arange(FFT_K1, dtype=i32)[:, None]
    n1 = jnp.arange(FFT_N1, dtype=i32)[None, :]
    m = ((2 * k1 + 1) * n1) % (2 * FFT_N1)
    ang = m.astype(F32) * (math.pi / FFT_N1)
    f1 = jnp.concatenate([jnp.cos(ang), -jnp.sin(ang)], axis=0)
    k1 = jnp.arange(FFT_K1, dtype=i32)[:, None, None]
    k2 = jnp.arange(FFT_N2, dtype=i32)[None, :, None]
    n2 = jnp.arange(FFT_N2, dtype=i32)[None, None, :]
    m = ((2 * (k1 + FFT_N1 * k2) + 1) * n2) % (2 * FFT_N)
    ang = m.astype(F32) * (math.pi / FFT_N)
    c, s = jnp.cos(ang), jnp.sin(ang)
    gm = jnp.concatenate(
        [jnp.concatenate([c, s], axis=2), jnp.concatenate([-s, c], axis=2)], axis=1)
    return f1, gm


def _hdot(a, b):
    return jnp.dot(a, b, preferred_element_type=F32, precision=lax.Precision.HIGHEST)


def _tap_lag(n):
    return jnp.where(n >= SEQ, FFT_N - n, n).astype(F32)


def _split_bf16(a):
    hi = a.astype(BF16)
    return hi, (a - hi.astype(F32)).astype(BF16)


def _filter_mlp_kernel(fq_ref, w1t_ref, w1c_ref, w1s_ref, b1_ref, wi_ref, bi_ref, fr_ref,
                       h3_ref, *, slabs):
    cols = slabs * FFT_N1
    c = lax.broadcasted_iota(jnp.int32, (1, cols), 1)
    n2 = pl.program_id(0) * slabs + c // FFT_N1
    pos = _tap_lag(FFT_N2 * (c % FFT_N1) + n2)
    t = pos * (1.0 / (SEQ - 1))
    ang = (2.0 * math.pi / SEQ) * pos * fq_ref[...]
    fr = fr_ref[...]
    pre = (w1t_ref[...] * t + _hdot(w1c_ref[...], jnp.cos(ang))
           + _hdot(w1s_ref[...], -jnp.sin(ang)))
    h = jnp.sin(fr * (pre + b1_ref[...]))
    for l in range(HY_INNER):
        h = jnp.sin(fr * (_hdot(wi_ref[l], h) + bi_ref[l]))
    hi, lo = _split_bf16(h)
    h3 = jnp.concatenate([hi, lo, hi], axis=0)
    for k in range(slabs):
        h3_ref[k] = h3[:, k * FFT_N1:(k + 1) * FFT_N1]


def _filter_mlp(fq, w1, b1, wi, bi, fr, slabs=8):
    w1t, w1c, w1s = w1[0:1].T, w1[1:1 + HY_BANDS].T, w1[1 + HY_BANDS:].T
    col = lambda a: a.reshape(-1, 1)
    const2 = lambda i: (0, 0)
    const3 = lambda i: (0, 0, 0)
    return pl.pallas_call(
        functools.partial(_filter_mlp_kernel, slabs=slabs),
        grid=(FFT_N2 // slabs,),
        in_specs=[
            pl.BlockSpec((HY_BANDS, 1), const2),
            pl.BlockSpec((HY_FW, 1), const2),
            pl.BlockSpec((HY_FW, HY_BANDS), const2),
            pl.BlockSpec((HY_FW, HY_BANDS), const2),
            pl.BlockSpec((HY_FW, 1), const2),
            pl.BlockSpec((HY_INNER, HY_FW, HY_FW), const3),
            pl.BlockSpec((HY_INNER, HY_FW, 1), const3),
            pl.BlockSpec((HY_FW, 1), const2),
        ],
        out_specs=pl.BlockSpec((slabs, 3 * HY_FW, FFT_N1), lambda i: (i, 0, 0)),
        out_shape=jax.ShapeDtypeStruct((FFT_N2, 3 * HY_FW, FFT_N1), BF16),
        compiler_params=_cparams(("parallel",)),
        name="hyena_filter_mlp",
    )(col(fq), w1t, w1c, w1s, col(b1), jnp.swapaxes(wi, 1, 2), bi[:, :, None], col(fr))


def _stage1(src_ref, f1_ref, s_ref):
    def body(n2, c):
        a = jnp.dot(f1_ref[...], src_ref[n2].astype(BF16), preferred_element_type=F32)
        s_ref[pl.ds(pl.multiple_of(n2 * SLAB, SUBLANES), 2 * FFT_K1), :] = a
        return c
    lax.fori_loop(0, FFT_N2, body, 0, unroll=DFT_UNROLL)


def _stage2_rhs(s_ref, k1):
    re = s_ref[pl.ds(k1, FFT_N2, stride=SLAB), :]
    im = s_ref[pl.ds(FFT_K1 + k1, FFT_N2, stride=SLAB), :]
    return jnp.concatenate([re, im], axis=0).astype(BF16)


def _filt_dft_kernel(h3_ref, w3_ref, dl_ref, f1_ref, gm_ref, kf_ref, s_ref):
    half = FFT_N1 // 2
    dt = dl_ref.shape[1]
    n1 = lax.broadcasted_iota(jnp.int32, (FFT_N1, 1), 0)
    dl = dl_ref[...]

    def body(n2, l1):
        both = lax.dot_general(h3_ref[n2], w3_ref[...], (((0,), (0,)), ((), ())),
                               preferred_element_type=F32)
        fwd = both[:half, :dt]
        bwd = both[half:, dt:]
        n = FFT_N2 * n1 + n2
        t = _tap_lag(n) * (1.0 / (SEQ - 1))
        taps = jnp.concatenate([fwd, -bwd], axis=0) * jnp.exp(-t * dl)
        taps = jnp.where(n == SEQ, 0.0, taps)
        a = jnp.dot(f1_ref[...], taps.astype(BF16), preferred_element_type=F32)
        s_ref[pl.ds(pl.multiple_of(n2 * SLAB, SUBLANES), 2 * FFT_K1), :] = a
        return l1 + jnp.sum(jnp.abs(taps), axis=0, keepdims=True)
    l1 = lax.fori_loop(0, FFT_N2, body, jnp.zeros(dl.shape, F32), unroll=DFT_UNROLL)
    inv = 1.0 / l1

    def stage2(k1, c):
        x = jnp.dot(gm_ref[k1], _stage2_rhs(s_ref, k1), preferred_element_type=F32)
        kf_ref[pl.ds(pl.multiple_of(k1 * 2 * FFT_N2, 2 * FFT_N2), 2 * FFT_N2), :] = x * inv
        return c
    lax.fori_loop(0, FFT_K1, stage2, 0, unroll=DFT_UNROLL)


def _filter_spectrum(h3, w3, dl2, f1_full, gm):
    ncol = 2 * D_MODEL
    dt = CONV_DT
    nct = ncol // dt
    tiles = lambda a: a.reshape(HY_FW, 2, nct, dt).transpose(2, 0, 1, 3).reshape(nct, HY_FW, 2 * dt)
    w3h, w3l = _split_bf16(w3)
    w3cat = jnp.concatenate([tiles(w3h), tiles(w3h), tiles(w3l)], axis=1)
    return pl.pallas_call(
        _filt_dft_kernel,
        grid=(nct,),
        in_specs=[
            pl.BlockSpec((FFT_N2, 3 * HY_FW, FFT_N1), lambda j: (0, 0, 0)),
            pl.BlockSpec((None, 3 * HY_FW, 2 * dt), lambda j: (j, 0, 0)),
            pl.BlockSpec((1, dt), lambda j: (0, j)),
            pl.BlockSpec((2 * FFT_K1, FFT_N1), lambda j: (0, 0)),
            pl.BlockSpec((FFT_K1, 2 * FFT_N2, 2 * FFT_N2), lambda j: (0, 0, 0),
                         pipeline_mode=pl.Buffered(1)),
        ],
        out_specs=pl.BlockSpec((FFT_K1 * 2 * FFT_N2, dt), lambda j: (0, j)),
        out_shape=jax.ShapeDtypeStruct((FFT_K1 * 2 * FFT_N2, ncol), F32),
        scratch_shapes=[pltpu.VMEM((FFT_N2 * SLAB, dt), F32)],
        compiler_params=_cparams(("parallel",)),
        name="hyena_filter_spectrum",
    )(h3, w3cat, dl2, f1_full, gm)


def _longconv_kernel(z_ref, g_ref, fb_ref, kf_ref, f1_ref, gm_ref, gh_ref, fi_ref, o_ref, s_ref):
    _stage1(z_ref, f1_ref, s_ref)

    def mid(k1, c):
        x = jnp.dot(gm_ref[k1], _stage2_rhs(s_ref, k1), preferred_element_type=F32)
        kf = kf_ref[pl.ds(pl.multiple_of(k1 * 2 * FFT_N2, 2 * FFT_N2), 2 * FFT_N2), :]
        xr, xi = x[:FFT_N2], x[FFT_N2:]
        kr, ki = kf[:FFT_N2], kf[FFT_N2:]
        y = jnp.concatenate([xr * kr - xi * ki, xr * ki + xi * kr], axis=0).astype(BF16)
        b = jnp.dot(gh_ref[k1], y, preferred_element_type=F32)
        s_ref[pl.ds(k1, FFT_N2, stride=SLAB), :] = b[:FFT_N2]
        s_ref[pl.ds(FFT_K1 + k1, FFT_N2, stride=SLAB), :] = b[FFT_N2:]
        return c
    lax.fori_loop(0, FFT_K1, mid, 0, unroll=2 * DFT_UNROLL)

    fb = fb_ref[...]

    def last(n2, c):
        rhs = s_ref[pl.ds(pl.multiple_of(n2 * SLAB, SUBLANES), 2 * FFT_K1), :].astype(BF16)
        y = jnp.dot(fi_ref[...], rhs, preferred_element_type=F32)
        o_ref[n2] = g_ref[n2] * (y + fb * z_ref[n2])
        return c
    lax.fori_loop(0, FFT_N2, last, 0, unroll=DFT_UNROLL)


def _long_conv_gate(z, zcol, g, gcol, fb, kf, kcol, f1_half, gm, gh, fi):
    bsz = z.shape[0]
    dt = CONV_DT
    nt = D_MODEL // dt
    n1 = FFT_N1 // 2
    const3 = lambda j, bi: (0, 0, 0)
    return pl.pallas_call(
        _longconv_kernel,
        grid=(nt, bsz),
        in_specs=[
            pl.BlockSpec((None, FFT_N2, n1, dt), lambda j, bi: (bi, 0, 0, zcol + j)),
            pl.BlockSpec((None, FFT_N2, n1, dt), lambda j, bi: (bi, 0, 0, gcol + j)),
            pl.BlockSpec((1, dt), lambda j, bi: (0, j)),
            pl.BlockSpec((FFT_K1 * 2 * FFT_N2, dt), lambda j, bi: (0, kcol + j),
                         pipeline_mode=pl.Buffered(1)),
            pl.BlockSpec((2 * FFT_K1, FFT_N1 // 2), lambda j, bi: (0, 0)),
            pl.BlockSpec((FFT_K1, 2 * FFT_N2, 2 * FFT_N2), const3, pipeline_mode=pl.Buffered(1)),
            pl.BlockSpec((FFT_K1, 2 * FFT_N2, 2 * FFT_N2), const3, pipeline_mode=pl.Buffered(1)),
            pl.BlockSpec((FFT_N1 // 2, 2 * FFT_K1), lambda j, bi: (0, 0)),
        ],
        out_specs=pl.BlockSpec((None, FFT_N2, n1, dt), lambda j, bi: (bi, 0, 0, j)),
        out_shape=jax.ShapeDtypeStruct((bsz, FFT_N2, n1, D_MODEL), F32),
        scratch_shapes=[pltpu.VMEM((FFT_N2 * SLAB, dt), F32)],
        compiler_params=_cparams(("parallel", "arbitrary")),
        name="hyena_long_conv",
    )(z, g, fb, kf, f1_half, gm, gh, fi)


def _hyena_out_kernel(z_ref, pm_ref, w_ref, b_ref, x_ref, o_ref):
    z = z_ref[...].reshape(x_ref.shape).astype(BF16)
    a = jnp.dot(pm_ref[...], z, preferred_element_type=F32).astype(BF16)
    acc = jnp.dot(a, w_ref[...], preferred_element_type=F32)
    o_ref[...] = x_ref[...] + acc + b_ref[...]


def _hyena_out_proj(z, w, b, x):
    bsz, seq, d = x.shape
    nb = INPROJ_N1
    tm = FFT_N2 * nb
    return pl.pallas_call(
        _hyena_out_kernel,
        grid=(bsz, seq // tm),
        in_specs=[
            pl.BlockSpec((None, FFT_N2, nb, d), lambda bi, i: (bi, 0, i, 0)),
            pl.BlockSpec((tm, tm), lambda bi, i: (0, 0)),
            pl.BlockSpec((d, d), lambda bi, i: (0, 0)),
            pl.BlockSpec((1, d), lambda bi, i: (0, 0)),
            pl.BlockSpec((None, tm, d), lambda bi, i: (bi, i, 0)),
        ],
        out_specs=pl.BlockSpec((None, tm, d), lambda bi, i: (bi, i, 0)),
        out_shape=jax.ShapeDtypeStruct((bsz, seq, d), F32),
        compiler_params=_cparams(("parallel", "parallel")),
        name="hyena_out_proj",
    )(z, _slab_perm().T, w, b, x)


def _hyena_layer(x, g_mix, w_in, b_in, cw, cb, fw1, fb1, fwi, fbi, fw3, freq, fbias, w_out, b_out, consts):
    f1_full, f1_half, gm, gh, fi, fq, dl2 = consts
    bsz, seq, d = x.shape
    pc = _inproj_conv(x, g_mix[None], w_in.astype(BF16), b_in[None], cw, cb[None])
    h3 = _filter_mlp(fq, fw1, fb1, fwi, fbi, freq)
    kf = _filter_spectrum(h3, fw3, dl2, f1_full, gm)
    nt = D_MODEL // CONV_DT
    z1 = _long_conv_gate(pc, 2 * nt, pc, 0, fbias[0:1], kf, 0, f1_half, gm, gh, fi)
    z2 = _long_conv_gate(z1, 0, pc, nt, fbias[1:2], kf, nt, f1_half, gm, gh, fi)
    return _hyena_out_proj(z2, w_out.astype(BF16), b_out[None], x)


def _relbias_kernel(tab_ref, bucket_ref, band_ref, o_ref):
    bucket = bucket_ref[...]
    band = band_ref[...]
    for h in range(N_HEADS):
        acc = jnp.zeros(bucket.shape, F32)
        for k in range(REL_BUCKETS):
            acc = jnp.where(bucket == k, tab_ref[k, h], acc)
        o_ref[h] = jnp.where(band > 0, acc, -1e30)


def _rel_bias(rel_table):
    a = jnp.arange(BLOCK)[:, None]
    j = jnp.arange(3 * BLOCK)[None, :]
    rel = j - BLOCK - a
    half = REL_BUCKETS // 2
    exact = half // 2
    n = jnp.abs(rel)
    nf = jnp.maximum(n, 1).astype(F32)
    large = exact + (jnp.log(nf / exact) / math.log(REL_MAX_DIST / exact) * (half - exact)).astype(jnp.int32)
    large = jnp.minimum(large, half - 1)
    bucket = (jnp.where(rel > 0, half, 0) + jnp.where(n < exact, n, large)).astype(jnp.int32)
    band = (n <= WINDOW).astype(jnp.int32)
    return pl.pallas_call(
        _relbias_kernel,
        in_specs=[
            pl.BlockSpec(memory_space=pltpu.SMEM),
            pl.BlockSpec((BLOCK, 3 * BLOCK), lambda: (0, 0)),
            pl.BlockSpec((BLOCK, 3 * BLOCK), lambda: (0, 0)),
        ],
        out_specs=pl.BlockSpec((N_HEADS, BLOCK, 3 * BLOCK), lambda: (0, 0, 0)),
        out_shape=jax.ShapeDtypeStruct((N_HEADS, BLOCK, 3 * BLOCK), F32),
        name="rel_bias",
    )(rel_table, bucket, band)


def _qkv_kernel(x_ref, g_ref, w_ref, gain_ref, flag_ref, bd_ref, o_ref, xn_ref):
    @pl.when(pl.program_id(1) == 0)
    def _():
        xn_ref[...] = _rms(x_ref[...], g_ref[...]).astype(BF16)

    acc = jnp.dot(xn_ref[...], w_ref[...], preferred_element_type=F32)
    hi, lo = _split_bf16(acc * acc)
    ssq = (jnp.dot(hi, bd_ref[...], preferred_element_type=F32)
           + jnp.dot(lo, bd_ref[...], preferred_element_type=F32))
    normed = acc * lax.rsqrt(ssq * (1.0 / HEAD_DIM) + EPS) * gain_ref[...]
    o_ref[...] = jnp.where(flag_ref[...] > 0, normed, acc).astype(o_ref.dtype)


def _qkv_proj(x2, g, w, gain, flag, tm=1024, tn=512):
    m, k = x2.shape
    n = w.shape[1]
    eye = jnp.arange(tn)[:, None] // HEAD_DIM == jnp.arange(tn)[None, :] // HEAD_DIM
    return pl.pallas_call(
        _qkv_kernel,
        grid=(m // tm, n // tn),
        in_specs=[
            pl.BlockSpec((tm, k), lambda i, j: (i, 0)),
            pl.BlockSpec((1, k), lambda i, j: (0, 0)),
            pl.BlockSpec((k, tn), lambda i, j: (0, j)),
            pl.BlockSpec((1, tn), lambda i, j: (0, j)),
            pl.BlockSpec((1, tn), lambda i, j: (0, j)),
            pl.BlockSpec((tn, tn), lambda i, j: (0, 0)),
        ],
        out_specs=pl.BlockSpec((tm, tn), lambda i, j: (i, j)),
        out_shape=jax.ShapeDtypeStruct((m, n), BF16),
        scratch_shapes=[pltpu.VMEM((tm, k), BF16)],
        compiler_params=_cparams(("parallel", "arbitrary")),
        name="qkv_proj",
    )(x2, g, w, gain, flag, eye.astype(BF16))


def _attn_kernel(q_ref, kp_ref, kc_ref, kn_ref, bias_ref, sink_ref, o_ref):
    nb = pl.num_programs(1)
    n = pl.program_id(1)
    kv = jnp.concatenate([kp_ref[...], kc_ref[...], kn_ref[...]], axis=0)
    key = lax.broadcasted_iota(jnp.int32, (3 * BLOCK, GROUP * BLOCK), 0)
    valid = ((key >= BLOCK) | (n > 0)) & ((key < 2 * BLOCK) | (n < nb - 1))
    lane = lax.broadcasted_iota(jnp.int32, (1, LANES), 1)
    keep = [(lane < HEAD_DIM).astype(F32).astype(BF16), (lane >= HEAD_DIM).astype(F32).astype(BF16)]
    ntile = N_KV // 2
    for j in range(ntile):
        q4 = jnp.concatenate(
            [q_ref[:, (GROUP * j + g) * LANES:(GROUP * j + g + 1) * LANES] for g in range(GROUP)], axis=0)
        kt = kv[:, j * LANES:(j + 1) * LANES]
        vt = kv[:, (ntile + j) * LANES:(ntile + j + 1) * LANES]
        acc = None
        for half in range(2):
            kh = 2 * j + half
            logits = lax.dot_general(kt * keep[half], q4, (((1,), (1,)), ((), ())),
                                     preferred_element_type=F32)
            logits = jnp.where(valid, logits + bias_ref[kh], -1e30)
            sink = sink_ref[kh]
            m = jnp.maximum(jnp.max(logits, axis=0, keepdims=True), sink)
            p = jnp.exp(logits - m)
            den = jnp.sum(p, axis=0, keepdims=True) + jnp.exp(sink - m)
            o = lax.dot_general(vt * keep[half], p.astype(BF16), (((0,), (0,)), ((), ())),
                                preferred_element_type=F32) * (1.0 / den)
            acc = o if acc is None else acc + o
        for g in range(GROUP):
            o_ref[(GROUP * j + g) * LANES:(GROUP * j + g + 1) * LANES, :] = (
                acc[:, g * BLOCK:(g + 1) * BLOCK].astype(o_ref.dtype))


def _attention(qkv, bias_t, sink_row):
    bsz, seq, _ = qkv.shape
    nb = seq // BLOCK
    dq = N_HEADS * HEAD_DIM
    dkv = 2 * N_KV * HEAD_DIM
    kvb = dq // dkv
    return pl.pallas_call(
        _attn_kernel,
        grid=(bsz, nb),
        in_specs=[
            pl.BlockSpec((None, BLOCK, dq), lambda b, n: (b, n, 0)),
            pl.BlockSpec((None, BLOCK, dkv), lambda b, n: (b, jnp.maximum(n - 1, 0), kvb)),
            pl.BlockSpec((None, BLOCK, dkv), lambda b, n: (b, n, kvb)),
            pl.BlockSpec((None, BLOCK, dkv), lambda b, n: (b, jnp.minimum(n + 1, nb - 1), kvb)),
            pl.BlockSpec((N_KV, 3 * BLOCK, GROUP * BLOCK), lambda b, n: (0, 0, 0)),
            pl.BlockSpec((N_KV, 1, GROUP * BLOCK), lambda b, n: (0, 0, 0)),
        ],
        out_specs=pl.BlockSpec((None, None, dq, BLOCK), lambda b, n: (b, n, 0, 0)),
        out_shape=jax.ShapeDtypeStruct((bsz, nb, dq, BLOCK), BF16),
        compiler_params=_cparams(("parallel", "parallel")),
        name="window_attention",
    )(qkv, qkv, qkv, qkv, bias_t, sink_row)


ATTN_OUT_BLOCKS = 8


def _attn_out_kernel(ot_ref, w_ref, x_ref, o_ref):
    for i in range(ATTN_OUT_BLOCKS):
        rows = slice(i * BLOCK, (i + 1) * BLOCK)
        y = lax.dot_general(ot_ref[i], w_ref[...], (((0,), (0,)), ((), ())),
                            preferred_element_type=F32)
        o_ref[rows, :] = x_ref[rows, :] + y


def _attn_out_proj(o_t, w, x2):
    nblk, dq, _ = o_t.shape
    m, d = x2.shape
    tm = ATTN_OUT_BLOCKS * BLOCK
    return pl.pallas_call(
        _attn_out_kernel,
        grid=(nblk // ATTN_OUT_BLOCKS,),
        in_specs=[
            pl.BlockSpec((ATTN_OUT_BLOCKS, dq, BLOCK), lambda i: (i, 0, 0)),
            pl.BlockSpec((dq, d), lambda i: (0, 0)),
            pl.BlockSpec((tm, d), lambda i: (i, 0)),
        ],
        out_specs=pl.BlockSpec((tm, d), lambda i: (i, 0)),
        out_shape=jax.ShapeDtypeStruct((m, d), F32),
        compiler_params=_cparams(("parallel",)),
        name="attn_out_proj",
    )(o_t, w, x2)


def _head_tile_perm():
    cols = []
    for tile in range(N_HEADS // 2):
        j, g = tile // GROUP, tile % GROUP
        for half in range(2):
            h = (2 * j + half) * GROUP + g
            cols.extend(range(h * HEAD_DIM, (h + 1) * HEAD_DIM))
    return jnp.asarray(cols, jnp.int32)


def _attn_layer(x, g_mix, w_qkv, q_gain, k_gain, sink, w_o, rel_table):
    bsz, seq, d = x.shape
    x2 = x.reshape(bsz * seq, d)
    dq = N_HEADS * HEAD_DIM
    dk = N_KV * HEAD_DIM
    perm = _head_tile_perm()
    w = jnp.concatenate([w_qkv[:, :dq][:, perm], w_qkv[:, dq:]], axis=1).astype(BF16)
    gain = jnp.concatenate([jnp.tile(q_gain * (HEAD_DIM ** -0.5), N_HEADS), jnp.tile(k_gain, N_KV),
                            jnp.ones((dk,), F32)])[None]
    flag = jnp.concatenate([jnp.ones((dq + dk,), F32), jnp.zeros((dk,), F32)])[None]
    qkv = _qkv_proj(x2, g_mix[None], w, gain, flag)
    bias_t = _rel_bias(rel_table).reshape(N_KV, GROUP * BLOCK, 3 * BLOCK).transpose(0, 2, 1)
    sink_row = jnp.repeat(sink, BLOCK).reshape(N_KV, 1, GROUP * BLOCK)
    o_t = _attention(qkv.reshape(bsz, seq, dq + 2 * dk), bias_t, sink_row)
    y = _attn_out_proj(o_t.reshape(bsz * (seq // BLOCK), dq, BLOCK), w_o[perm, :].astype(BF16), x2)
    return y.reshape(bsz, seq, d)


def _pool_kernel(xp_ref, xc_ref, xn_ref, g_ref, w_ref, b_ref, sc_ref, o_ref, *, tm):
    i = pl.program_id(1)
    g = g_ref[...]
    rows = tm + 2 * SUBLANES
    t = i * tm - SUBLANES + lax.broadcasted_iota(jnp.int32, (rows, 1), 0)
    inside = (t >= 0) & (t < SEQ)
    xa = jnp.concatenate([_rms(xp_ref[...], g), _rms(xc_ref[...], g), _rms(xn_ref[...], g)], axis=0)
    xa = jnp.where(inside, xa, 0.0)
    tc = t[SUBLANES:SUBLANES + tm]
    outs = []
    for gi, wdt in enumerate(POOL_WINDOWS):
        r = wdt // 2
        xg = xa[:, gi * POOL_GROUP:(gi + 1) * POOL_GROUP]
        run = xg
        span = 1
        while span < 2 * r:
            run = run + pltpu.roll(run, rows - span, 0)
            span *= 2
        win = (pltpu.roll(run, r, 0) + pltpu.roll(xg, rows - r, 0))[SUBLANES:SUBLANES + tm]
        cnt = (jnp.minimum(tc + r + 1, SEQ) - jnp.maximum(tc - r, 0)).astype(F32)
        dlt = win / cnt - xg[SUBLANES:SUBLANES + tm]
        outs.append(jnp.dot(dlt.astype(BF16), w_ref[gi], preferred_element_type=F32))
    y = (jnp.concatenate(outs, axis=-1) + b_ref[...]) * sc_ref[...]
    o_ref[...] = xc_ref[...] + y


def _pool_layer(x, g_mix, w_grp, b, scale, tm=512):
    bsz, seq, d = x.shape
    hb = tm // SUBLANES
    last = seq // SUBLANES - 1
    ng = len(POOL_WINDOWS)
    return pl.pallas_call(
        functools.partial(_pool_kernel, tm=tm),
        grid=(bsz, seq // tm),
        in_specs=[
            pl.BlockSpec((None, SUBLANES, d), lambda bi, i: (bi, jnp.maximum(i * hb - 1, 0), 0)),
            pl.BlockSpec((None, tm, d), lambda bi, i: (bi, i, 0)),
            pl.BlockSpec((None, SUBLANES, d), lambda bi, i: (bi, jnp.minimum((i + 1) * hb, last), 0)),
            pl.BlockSpec((1, d), lambda bi, i: (0, 0)),
            pl.BlockSpec((ng, POOL_GROUP, POOL_GROUP), lambda bi, i: (0, 0, 0)),
            pl.BlockSpec((1, d), lambda bi, i: (0, 0)),
            pl.BlockSpec((1, d), lambda bi, i: (0, 0)),
        ],
        out_specs=pl.BlockSpec((None, tm, d), lambda bi, i: (bi, i, 0)),
        out_shape=jax.ShapeDtypeStruct((bsz, seq, d), F32),
        compiler_params=_cparams(("parallel", "parallel")),
        name="pool_mixer",
    )(x, x, x, g_mix[None], w_grp.astype(BF16), b[None], scale[None])


def kernel(x, norm_mix, norm_ffn, hy_w_in, hy_b_in, hy_conv_w, hy_conv_b, hy_f_w1, hy_f_b1, hy_f_wi, hy_f_bi, hy_f_w3, hy_f_freq, hy_f_bias, hy_w_out, hy_b_out, at_w_qkv, at_q_gain, at_k_gain, at_sink, at_w_o, rel_table, pl_w, pl_b, pl_scale, ff_w_gate, ff_w_up, ff_w_down):
    bsz, seq, d = x.shape
    f1, gm = _dft_mats()
    f1_full = f1.astype(BF16)
    f1_half = f1[:, :FFT_N1 // 2].astype(BF16)
    fi = (f1[:, :FFT_N1 // 2].T * (2.0 / FFT_N)).astype(BF16)
    gh = jnp.swapaxes(gm, 1, 2).astype(BF16)
    gm = gm.astype(BF16)
    fq = jnp.linspace(1e-4, HY_BANDS - 1, HY_BANDS, dtype=F32)[None, :]
    deltas = jnp.abs(jnp.linspace(HY_MIN_DECAY, HY_MAX_DECAY, D_MODEL, dtype=F32))
    dl2 = jnp.concatenate([deltas, deltas])[None, :]
    consts = (f1_full, f1_half, gm, gh, fi, fq, dl2)

    for i in range(DEPTH):
        kind, s = i % 3, i // 3
        if kind == 0:
            x = _hyena_layer(x, norm_mix[i], hy_w_in[s], hy_b_in[s], hy_conv_w[s], hy_conv_b[s],
                             hy_f_w1[s], hy_f_b1[s], hy_f_wi[s], hy_f_bi[s], hy_f_w3[s],
                             hy_f_freq[s], hy_f_bias[s], hy_w_out[s], hy_b_out[s], consts)
        elif kind == 1:
            x = _attn_layer(x, norm_mix[i], at_w_qkv[s], at_q_gain[s], at_k_gain[s], at_sink[s],
                            at_w_o[s], rel_table)
        else:
            x = _pool_layer(x, norm_mix[i], pl_w[s], pl_b[s], pl_scale[s])
        x2 = _ffn(x.reshape(bsz * seq, d), norm_ffn[i][None], ff_w_gate[i].astype(BF16),
                  ff_w_up[i].astype(BF16), ff_w_down[i].astype(BF16))
        x = x2.reshape(bsz, seq, d)
    return x
```

```python
import functools
import math

import jax
import jax.numpy as jnp
from jax import lax
from jax.experimental import pallas as pl
from jax.experimental.pallas import tpu as pltpu

F32 = jnp.float32
BF16 = jnp.bfloat16

D_MODEL = 1024
BATCH = 4
SEQ = 8192
DEPTH = 4
EPS = 1e-6

HY_SHORT = 3
HY_EMB = 33
HY_BANDS = 16
HY_FW = 64
HY_INNER = 2
HY_MIN_DECAY = math.log(1e-2) / 1.5
HY_MAX_DECAY = math.log(1e-2) / 0.3

HEAD_DIM = 64
N_HEADS = 16
N_KV = 4
GROUP = 4
WINDOW = 128
BLOCK = 128
REL_BUCKETS = 32
REL_MAX_DIST = 128

POOL_WINDOWS = (2, 4, 8, 16)
POOL_GROUP = 256
D_FF = 2816

SUBLANES = 8
LANES = 128
VMEM_LIMIT = 56 * 1024 * 1024

FFT_N = 2 * SEQ
FFT_N1 = 256
FFT_N2 = 64
FFT_K1 = FFT_N1 // 2
SLAB = 2 * FFT_K1 + SUBLANES
CONV_DT = 128
DFT_UNROLL = 16


def _cparams(sem):
    return pltpu.CompilerParams(dimension_semantics=sem, vmem_limit_bytes=VMEM_LIMIT)


def _rms(x, g):
    return x * lax.rsqrt(jnp.mean(x * x, axis=-1, keepdims=True) + EPS) * g


FFN_CHUNK = 256


def _ffn_kernel(x_ref, g_ref, wg_ref, wu_ref, wd_ref, o_ref, h_ref):
    x = x_ref[...]
    xn = _rms(x, g_ref[...]).astype(BF16)
    for c in range(D_FF // FFN_CHUNK):
        cols = slice(c * FFN_CHUNK, (c + 1) * FFN_CHUNK)
        gate = jnp.dot(xn, wg_ref[:, cols], preferred_element_type=F32)
        up = jnp.dot(xn, wu_ref[:, cols], preferred_element_type=F32)
        h_ref[:, cols] = (gate * jax.nn.sigmoid(gate) * up).astype(BF16)
    o_ref[...] = x + jnp.dot(h_ref[...], wd_ref[...], preferred_element_type=F32)


def _ffn(x2, g, wg, wu, wd, tm=512):
    m, d = x2.shape
    f = wg.shape[1]
    resident = dict(pipeline_mode=pl.Buffered(1))
    return pl.pallas_call(
        _ffn_kernel,
        grid=(m // tm,),
        in_specs=[
            pl.BlockSpec((tm, d), lambda i: (i, 0)),
            pl.BlockSpec((1, d), lambda i: (0, 0)),
            pl.BlockSpec((d, f), lambda i: (0, 0), **resident),
            pl.BlockSpec((d, f), lambda i: (0, 0), **resident),
            pl.BlockSpec((f, d), lambda i: (0, 0), **resident),
        ],
        out_specs=pl.BlockSpec((tm, d), lambda i: (i, 0)),
        out_shape=jax.ShapeDtypeStruct((m, d), F32),
        scratch_shapes=[pltpu.VMEM((tm, f), BF16)],
        compiler_params=_cparams(("parallel",)),
        name="ffn",
    )(x2, g, wg, wu, wd)


INPROJ_N1 = 16
INPROJ_CHUNK = 512


def _slab_perm():
    nb = INPROJ_N1
    r = jnp.arange(FFT_N2 * nb)
    src = FFT_N2 * (r % nb) + r // nb
    return (src[:, None] == r[None, :]).astype(BF16)


def _inproj_kernel(xp_ref, xc_ref, xn_ref, g_ref, pm_ref, w_ref, b_ref, cw_ref, cb_ref, o_ref, xs_ref):
    i = pl.program_id(1)
    nb = INPROJ_N1
    tm = FFT_N2 * nb
    g = g_ref[...]
    xn = _rms(xc_ref[...], g).astype(BF16)
    xs_ref[:tm, :] = jnp.dot(pm_ref[...], xn, preferred_element_type=F32).astype(BF16)
    halo = jnp.concatenate([xp_ref[...], xn_ref[...]], axis=0)
    xs_ref[tm:, :] = _rms(halo, g).astype(BF16)

    xs = xs_ref[...]
    row = lax.broadcasted_iota(jnp.int32, (nb, 1), 0)
    inside_before = i > 0
    inside_after = i < pl.num_programs(1) - 1
    for c in range(w_ref.shape[1] // INPROJ_CHUNK):
        cols = slice(c * INPROJ_CHUNK, (c + 1) * INPROJ_CHUNK)
        p = jnp.dot(xs, w_ref[:, cols], preferred_element_type=F32) + b_ref[:, cols]
        cur = p[:tm]
        before = jnp.where(inside_before, p[tm + SUBLANES - 1:tm + SUBLANES], 0.0)
        after = jnp.where(inside_after, p[tm + SUBLANES:tm + SUBLANES + 1], 0.0)
        first = jnp.where(row == 0, before, pltpu.roll(cur[tm - nb:], 1, 0))
        last = jnp.where(row == nb - 1, after, pltpu.roll(cur[:nb], nb - 1, 0))
        dn = jnp.concatenate([first, cur[:tm - nb]], axis=0)
        up = jnp.concatenate([cur[nb:], last], axis=0)
        cw = cw_ref[:, cols]
        out = cb_ref[:, cols] + cw[0:1] * dn + cw[1:2] * cur + cw[2:3] * up
        o_ref[:, :, cols] = out.reshape(FFT_N2, nb, INPROJ_CHUNK).astype(o_ref.dtype)


def _inproj_conv(x, g, w, b, cw, cb):
    bsz, seq, d = x.shape
    n = w.shape[1]
    nb = INPROJ_N1
    tm = FFT_N2 * nb
    hb = tm // SUBLANES
    last = seq // SUBLANES - 1
    const = lambda bi, i: (0, 0)
    return pl.pallas_call(
        _inproj_kernel,
        grid=(bsz, seq // tm),
        in_specs=[
            pl.BlockSpec((None, SUBLANES, d), lambda bi, i: (bi, jnp.maximum(i * hb - 1, 0), 0)),
            pl.BlockSpec((None, tm, d), lambda bi, i: (bi, i, 0)),
            pl.BlockSpec((None, SUBLANES, d), lambda bi, i: (bi, jnp.minimum((i + 1) * hb, last), 0)),
            pl.BlockSpec((1, d), const),
            pl.BlockSpec((tm, tm), const, pipeline_mode=pl.Buffered(1)),
            pl.BlockSpec((d, n), const, pipeline_mode=pl.Buffered(1)),
            pl.BlockSpec((1, n), const),
            pl.BlockSpec((HY_SHORT, n), const),
            pl.BlockSpec((1, n), const),
        ],
        out_specs=pl.BlockSpec((None, FFT_N2, nb, n), lambda bi, i: (bi, 0, i, 0)),
        out_shape=jax.ShapeDtypeStruct((bsz, FFT_N2, seq // FFT_N2, n), BF16),
        scratch_shapes=[pltpu.VMEM((tm + 2 * SUBLANES, d), BF16)],
        compiler_params=_cparams(("parallel", "parallel")),
        name="hyena_inproj",
    )(x, x, x, g, _slab_perm(), w, b, cw, cb)


def _dft_mats():
    i32 = jnp.int32
    k1 = jnp.arange(FFT_K1, dtype=i32)[:, None]
    n1 = jnp.arange(FFT_N1, dtype=i32)[None, :]
    m = ((2 * k1 + 1) * n1) % (2 * FFT_N1)
    ang = m.astype(F32) * (math.pi / FFT_N1)
    f1 = jnp.concatenate([jnp.cos(ang), -jnp.sin(ang)], axis=0)
    k1 = jnp.arange(FFT_K1, dtype=i32)[:, None, None]
    k2 = jnp.arange(FFT_N2, dtype=i32)[None, :, None]
    n2 = jnp.arange(FFT_N2, dtype=i32)[None, None, :]
    m = ((2 * (k1 + FFT_N1 * k2) + 1) * n2) % (2 * FFT_N)
    ang = m.astype(F32) * (math.pi / FFT_N)
    c, s = jnp.cos(ang), jnp.sin(ang)
    gm = jnp.concatenate(
        [jnp.concatenate([c, s], axis=2), jnp.concatenate([-s, c], axis=2)], axis=1)
    return f1, gm


def _hdot(a, b):
    return jnp.dot(a, b, preferred_element_type=F32, precision=lax.Precision.HIGHEST)


def _tap_lag(n):
    return jnp.where(n >= SEQ, FFT_N - n, n).astype(F32)


def _split_bf16(a):
    hi = a.astype(BF16)
    return hi, (a - hi.astype(F32)).astype(BF16)


def _filter_mlp_kernel(fq_ref, w1t_ref, w1c_ref, w1s_ref, b1_ref, wi_ref, bi_ref, fr_ref,
                       h3_ref, *, slabs):
    cols = slabs * FFT_N1
    c = lax.broadcasted_iota(jnp.int32, (1, cols), 1)
    n2 = pl.program_id(0) * slabs + c // FFT_N1
    pos = _tap_lag(FFT_N2 * (c % FFT_N1) + n2)
    t = pos * (1.0 / (SEQ - 1))
    ang = (2.0 * math.pi / SEQ) * pos * fq_ref[...]
    fr = fr_ref[...]
    pre = (w1t_ref[...] * t + _hdot(w1c_ref[...], jnp.cos(ang))
           + _hdot(w1s_ref[...], -jnp.sin(ang)))
    h = jnp.sin(fr * (pre + b1_ref[...]))
    for l in range(HY_INNER):
        h = jnp.sin(fr * (_hdot(wi_ref[l], h) + bi_ref[l]))
    hi, lo = _split_bf16(h)
    h3 = jnp.concatenate([hi, lo, hi], axis=0)
    for k in range(slabs):
        h3_ref[k] = h3[:, k * FFT_N1:(k + 1) * FFT_N1]


def _filter_mlp(fq, w1, b1, wi, bi, fr, slabs=8):
    w1t, w1c, w1s = w1[0:1].T, w1[1:1 + HY_BANDS].T, w1[1 + HY_BANDS:].T
    col = lambda a: a.reshape(-1, 1)
    const2 = lambda i: (0, 0)
    const3 = lambda i: (0, 0, 0)
    return pl.pallas_call(
        functools.partial(_filter_mlp_kernel, slabs=slabs),
        grid=(FFT_N2 // slabs,),
        in_specs=[
            pl.BlockSpec((HY_BANDS, 1), const2),
            pl.BlockSpec((HY_FW, 1), const2),
            pl.BlockSpec((HY_FW, HY_BANDS), const2),
            pl.BlockSpec((HY_FW, HY_BANDS), const2),
            pl.BlockSpec((HY_FW, 1), const2),
            pl.BlockSpec((HY_INNER, HY_FW, HY_FW), const3),
            pl.BlockSpec((HY_INNER, HY_FW, 1), const3),
            pl.BlockSpec((HY_FW, 1), const2),
        ],
        out_specs=pl.BlockSpec((slabs, 3 * HY_FW, FFT_N1), lambda i: (i, 0, 0)),
        out_shape=jax.ShapeDtypeStruct((FFT_N2, 3 * HY_FW, FFT_N1), BF16),
        compiler_params=_cparams(("parallel",)),
        name="hyena_filter_mlp",
    )(col(fq), w1t, w1c, w1s, col(b1), jnp.swapaxes(wi, 1, 2), bi[:, :, None], col(fr))


def _stage1(src_ref, f1_ref, s_ref):
    def body(n2, c):
        a = jnp.dot(f1_ref[...], src_ref[n2].astype(BF16), preferred_element_type=F32)
        s_ref[pl.ds(pl.multiple_of(n2 * SLAB, SUBLANES), 2 * FFT_K1), :] = a
        return c
    lax.fori_loop(0, FFT_N2, body, 0, unroll=DFT_UNROLL)


def _stage2_rhs(s_ref, k1):
    re = s_ref[pl.ds(k1, FFT_N2, stride=SLAB), :]
    im = s_ref[pl.ds(FFT_K1 + k1, FFT_N2, stride=SLAB), :]
    return jnp.concatenate([re, im], axis=0).astype(BF16)


def _filt_dft_kernel(h3_ref, w3_ref, dl_ref, f1_ref, gm_ref, kf_ref, s_ref):
    half = FFT_N1 // 2
    dt = dl_ref.shape[1]
    n1 = lax.broadcasted_iota(jnp.int32, (FFT_N1, 1), 0)
    dl = dl_ref[...]

    def body(n2, l1):
        both = lax.dot_general(h3_ref[n2], w3_ref[...], (((0,), (0,)), ((), ())),
                               preferred_element_type=F32)
        fwd = both[:half, :dt]
        bwd = both[half:, dt:]
        n = FFT_N2 * n1 + n2
        t = _tap_lag(n) * (1.0 / (SEQ - 1))
        taps = jnp.concatenate([fwd, -bwd], axis=0) * jnp.exp(-t * dl)
        taps = jnp.where(n == SEQ, 0.0, taps)
        a = jnp.dot(f1_ref[...], taps.astype(BF16), preferred_element_type=F32)
        s_ref[pl.ds(pl.multiple_of(n2 * SLAB, SUBLANES), 2 * FFT_K1), :] = a
        return l1 + jnp.sum(jnp.abs(taps), axis=0, keepdims=True)
    l1 = lax.fori_loop(0, FFT_N2, body, jnp.zeros(dl.shape, F32), unroll=DFT_UNROLL)
    inv = 1.0 / l1

    def stage2(k1, c):
        x = jnp.dot(gm_ref[k1], _stage2_rhs(s_ref, k1), preferred_element_type=F32)
        kf_ref[pl.ds(pl.multiple_of(k1 * 2 * FFT_N2, 2 * FFT_N2), 2 * FFT_N2), :] = x * inv
        return c
    lax.fori_loop(0, FFT_K1, stage2, 0, unroll=DFT_UNROLL)


def _filter_spectrum(h3, w3, dl2, f1_full, gm):
    ncol = 2 * D_MODEL
    dt = CONV_DT
    nct = ncol // dt
    tiles = lambda a: a.reshape(HY_FW, 2, nct, dt).transpose(2, 0, 1, 3).reshape(nct, HY_FW, 2 * dt)
    w3h, w3l = _split_bf16(w3)
    w3cat = jnp.concatenate([tiles(w3h), tiles(w3h), tiles(w3l)], axis=1)
    return pl.pallas_call(
        _filt_dft_kernel,
        grid=(nct,),
        in_specs=[
            pl.BlockSpec((FFT_N2, 3 * HY_FW, FFT_N1), lambda j: (0, 0, 0)),
            pl.BlockSpec((None, 3 * HY_FW, 2 * dt), lambda j: (j, 0, 0)),
            pl.BlockSpec((1, dt), lambda j: (0, j)),
            pl.BlockSpec((2 * FFT_K1, FFT_N1), lambda j: (0, 0)),
            pl.BlockSpec((FFT_K1, 2 * FFT_N2, 2 * FFT_N2), lambda j: (0, 0, 0),
                         pipeline_mode=pl.Buffered(1)),
        ],
        out_specs=pl.BlockSpec((FFT_K1 * 2 * FFT_N2, dt), lambda j: (0, j)),
        out_shape=jax.ShapeDtypeStruct((FFT_K1 * 2 * FFT_N2, ncol), F32),
        scratch_shapes=[pltpu.VMEM((FFT_N2 * SLAB, dt), F32)],
        compiler_params=_cparams(("parallel",)),
        name="hyena_filter_spectrum",
    )(h3, w3cat, dl2, f1_full, gm)


def _longconv_kernel(z_ref, g_ref, fb_ref, kf_ref, f1_ref, gm_ref, gh_ref, fi_ref, o_ref, s_ref):
    _stage1(z_ref, f1_ref, s_ref)

    def mid(k1, c):
        x = jnp.dot(gm_ref[k1], _stage2_rhs(s_ref, k1), preferred_element_type=F32)
        kf = kf_ref[pl.ds(pl.multiple_of(k1 * 2 * FFT_N2, 2 * FFT_N2), 2 * FFT_N2), :]
        xr, xi = x[:FFT_N2], x[FFT_N2:]
        kr, ki = kf[:FFT_N2], kf[FFT_N2:]
        y = jnp.concatenate([xr * kr - xi * ki, xr * ki + xi * kr], axis=0).astype(BF16)
        b = jnp.dot(gh_ref[k1], y, preferred_element_type=F32)
        s_ref[pl.ds(k1, FFT_N2, stride=SLAB), :] = b[:FFT_N2]
        s_ref[pl.ds(FFT_K1 + k1, FFT_N2, stride=SLAB), :] = b[FFT_N2:]
        return c
    lax.fori_loop(0, FFT_K1, mid, 0, unroll=2 * DFT_UNROLL)

    fb = fb_ref[...]

    def last(n2, c):
        rhs = s_ref[pl.ds(pl.multiple_of(n2 * SLAB, SUBLANES), 2 * FFT_K1), :].astype(BF16)
        y = jnp.dot(fi_ref[...], rhs, preferred_element_type=F32)
        o_ref[n2] = (g_ref[n2] * (y + fb * z_ref[n2])).astype(o_ref.dtype)
        return c
    lax.fori_loop(0, FFT_N2, last, 0, unroll=DFT_UNROLL)


def _long_conv_gate(z, zcol, g, gcol, fb, kf, kcol, f1_half, gm, gh, fi):
    bsz = z.shape[0]
    dt = CONV_DT
    nt = D_MODEL // dt
    n1 = FFT_N1 // 2
    const3 = lambda j, bi: (0, 0, 0)
    return pl.pallas_call(
        _longconv_kernel,
        grid=(nt, bsz),
        in_specs=[
            pl.BlockSpec((None, FFT_N2, n1, dt), lambda j, bi: (bi, 0, 0, zcol + j)),
            pl.BlockSpec((None, FFT_N2, n1, dt), lambda j, bi: (bi, 0, 0, gcol + j)),
            pl.BlockSpec((1, dt), lambda j, bi: (0, j)),
            pl.BlockSpec((FFT_K1 * 2 * FFT_N2, dt), lambda j, bi: (0, kcol + j),
                         pipeline_mode=pl.Buffered(1)),
            pl.BlockSpec((2 * FFT_K1, FFT_N1 // 2), lambda j, bi: (0, 0)),
            pl.BlockSpec((FFT_K1, 2 * FFT_N2, 2 * FFT_N2), const3, pipeline_mode=pl.Buffered(1)),
            pl.BlockSpec((FFT_K1, 2 * FFT_N2, 2 * FFT_N2), const3, pipeline_mode=pl.Buffered(1)),
            pl.BlockSpec((FFT_N1 // 2, 2 * FFT_K1), lambda j, bi: (0, 0)),
        ],
        out_specs=pl.BlockSpec((None, FFT_N2, n1, dt), lambda j, bi: (bi, 0, 0, j)),
        out_shape=jax.ShapeDtypeStruct((bsz, FFT_N2, n1, D_MODEL), BF16),
        scratch_shapes=[pltpu.VMEM((FFT_N2 * SLAB, dt), F32)],
        compiler_params=_cparams(("parallel", "arbitrary")),
        name="hyena_long_conv",
    )(z, g, fb, kf, f1_half, gm, gh, fi)


def _hyena_out_kernel(z_ref, pm_ref, w_ref, b_ref, x_ref, o_ref):
    z = z_ref[...].reshape(x_ref.shape)
    a = jnp.dot(pm_ref[...], z, preferred_element_type=F32).astype(BF16)
    acc = jnp.dot(a, w_ref[...], preferred_element_type=F32)
    o_ref[...] = x_ref[...] + acc + b_ref[...]


def _hyena_out_proj(z, w, b, x):
    bsz, seq, d = x.shape
    nb = INPROJ_N1
    tm = FFT_N2 * nb
    return pl.pallas_call(
        _hyena_out_kernel,
        grid=(bsz, seq // tm),
        in_specs=[
            pl.BlockSpec((None, FFT_N2, nb, d), lambda bi, i: (bi, 0, i, 0)),
            pl.BlockSpec((tm, tm), lambda bi, i: (0, 0)),
            pl.BlockSpec((d, d), lambda bi, i: (0, 0)),
            pl.BlockSpec((1, d), lambda bi, i: (0, 0)),
            pl.BlockSpec((None, tm, d), lambda bi, i: (bi, i, 0)),
        ],
        out_specs=pl.BlockSpec((None, tm, d), lambda bi, i: (bi, i, 0)),
        out_shape=jax.ShapeDtypeStruct((bsz, seq, d), F32),
        compiler_params=_cparams(("parallel", "parallel")),
        name="hyena_out_proj",
    )(z, _slab_perm().T, w, b, x)


def _hyena_layer(x, g_mix, w_in, b_in, cw, cb, fw1, fb1, fwi, fbi, fw3, freq, fbias, w_out, b_out, consts):
    f1_full, f1_half, gm, gh, fi, fq, dl2 = consts
    bsz, seq, d = x.shape
    pc = _inproj_conv(x, g_mix[None], w_in.astype(BF16), b_in[None], cw, cb[None])
    h3 = _filter_mlp(fq, fw1, fb1, fwi, fbi, freq)
    kf = _filter_spectrum(h3, fw3, dl2, f1_full, gm)
    nt = D_MODEL // CONV_DT
    z1 = _long_conv_gate(pc, 2 * nt, pc, 0, fbias[0:1], kf, 0, f1_half, gm, gh, fi)
    z2 = _long_conv_gate(z1, 0, pc, nt, fbias[1:2], kf, nt, f1_half, gm, gh, fi)
    return _hyena_out_proj(z2, w_out.astype(BF16), b_out[None], x)


def _relbias_kernel(tab_ref, bucket_ref, band_ref, o_ref):
    bucket = bucket_ref[...]
    band = band_ref[...]
    for h in range(N_HEADS):
        acc = jnp.zeros(bucket.shape, F32)
        for k in range(REL_BUCKETS):
            acc = jnp.where(bucket == k, tab_ref[k, h], acc)
        o_ref[h] = jnp.where(band > 0, acc, -1e30)


def _rel_bias(rel_table):
    a = jnp.arange(BLOCK)[:, None]
    j = jnp.arange(3 * BLOCK)[None, :]
    rel = j - BLOCK - a
    half = REL_BUCKETS // 2
    exact = half // 2
    n = jnp.abs(rel)
    nf = jnp.maximum(n, 1).astype(F32)
    large = exact + (jnp.log(nf / exact) / math.log(REL_MAX_DIST / exact) * (half - exact)).astype(jnp.int32)
    large = jnp.minimum(large, half - 1)
    bucket = (jnp.where(rel > 0, half, 0) + jnp.where(n < exact, n, large)).astype(jnp.int32)
    band = (n <= WINDOW).astype(jnp.int32)
    return pl.pallas_call(
        _relbias_kernel,
        in_specs=[
            pl.BlockSpec(memory_space=pltpu.SMEM),
            pl.BlockSpec((BLOCK, 3 * BLOCK), lambda: (0, 0)),
            pl.BlockSpec((BLOCK, 3 * BLOCK), lambda: (0, 0)),
        ],
        out_specs=pl.BlockSpec((N_HEADS, BLOCK, 3 * BLOCK), lambda: (0, 0, 0)),
        out_shape=jax.ShapeDtypeStruct((N_HEADS, BLOCK, 3 * BLOCK), F32),
        name="rel_bias",
    )(rel_table, bucket, band)


def _qkv_kernel(x_ref, g_ref, w_ref, gain_ref, flag_ref, bd_ref, o_ref, xn_ref):
    @pl.when(pl.program_id(1) == 0)
    def _():
        xn_ref[...] = _rms(x_ref[...], g_ref[...]).astype(BF16)

    acc = jnp.dot(xn_ref[...], w_ref[...], preferred_element_type=F32)
    hi, lo = _split_bf16(acc * acc)
    ssq = (jnp.dot(hi, bd_ref[...], preferred_element_type=F32)
           + jnp.dot(lo, bd_ref[...], preferred_element_type=F32))
    normed = acc * lax.rsqrt(ssq * (1.0 / HEAD_DIM) + EPS) * gain_ref[...]
    o_ref[...] = jnp.where(flag_ref[...] > 0, normed, acc).astype(o_ref.dtype)


def _qkv_proj(x2, g, w, gain, flag, tm=1024, tn=512):
    m, k = x2.shape
    n = w.shape[1]
    eye = jnp.arange(tn)[:, None] // HEAD_DIM == jnp.arange(tn)[None, :] // HEAD_DIM
    return pl.pallas_call(
        _qkv_kernel,
        grid=(m // tm, n // tn),
        in_specs=[
            pl.BlockSpec((tm, k), lambda i, j: (i, 0)),
            pl.BlockSpec((1, k), lambda i, j: (0, 0)),
            pl.BlockSpec((k, tn), lambda i, j: (0, j)),
            pl.BlockSpec((1, tn), lambda i, j: (0, j)),
            pl.BlockSpec((1, tn), lambda i, j: (0, j)),
            pl.BlockSpec((tn, tn), lambda i, j: (0, 0)),
        ],
        out_specs=pl.BlockSpec((tm, tn), lambda i, j: (i, j)),
        out_shape=jax.ShapeDtypeStruct((m, n), BF16),
        scratch_shapes=[pltpu.VMEM((tm, k), BF16)],
        compiler_params=_cparams(("parallel", "arbitrary")),
        name="qkv_proj",
    )(x2, g, w, gain, flag, eye.astype(BF16))


def _attn_kernel(q_ref, kp_ref, kc_ref, kn_ref, bias_ref, sink_ref, o_ref):
    kv = jnp.concatenate([kp_ref[...], kc_ref[...], kn_ref[...]], axis=0)
    lane = lax.broadcasted_iota(jnp.int32, (1, LANES), 1)
    keep = [(lane < HEAD_DIM).astype(F32).astype(BF16), (lane >= HEAD_DIM).astype(F32).astype(BF16)]
    ntile = N_KV // 2
    logits = []
    for kh in range(N_KV):
        j, half = kh // 2, kh % 2
        q4 = jnp.concatenate(
            [q_ref[:, (GROUP * j + g) * LANES:(GROUP * j + g + 1) * LANES] for g in range(GROUP)], axis=0)
        kt = kv[:, j * LANES:(j + 1) * LANES]
        logits.append(lax.dot_general(kt * keep[half], q4, (((1,), (1,)), ((), ())),
                                      preferred_element_type=F32))
    outs = []
    for kh in range(N_KV):
        j, half = kh // 2, kh % 2
        vt = kv[:, (ntile + j) * LANES:(ntile + j + 1) * LANES]
        lg = logits[kh] + bias_ref[kh]
        sink = sink_ref[kh]
        m = jnp.maximum(jnp.max(lg, axis=0, keepdims=True), sink)
        p = jnp.exp(lg - m)
        den = jnp.sum(p, axis=0, keepdims=True) + jnp.exp(sink - m)
        outs.append(lax.dot_general(vt * keep[half], p.astype(BF16), (((0,), (0,)), ((), ())),
                                    preferred_element_type=F32) * (1.0 / den))
    for j in range(ntile):
        acc = outs[2 * j] + outs[2 * j + 1]
        for g in range(GROUP):
            o_ref[(GROUP * j + g) * LANES:(GROUP * j + g + 1) * LANES, :] = (
                acc[:, g * BLOCK:(g + 1) * BLOCK].astype(o_ref.dtype))


def _attention(qkv, bias_t, sink_row):
    bsz, seq, _ = qkv.shape
    nb = seq // BLOCK
    dq = N_HEADS * HEAD_DIM
    dkv = 2 * N_KV * HEAD_DIM
    kvb = dq // dkv
    return pl.pallas_call(
        _attn_kernel,
        grid=(bsz, nb),
        in_specs=[
            pl.BlockSpec((None, BLOCK, dq), lambda b, n: (b, n, 0)),
            pl.BlockSpec((None, BLOCK, dkv), lambda b, n: (b, jnp.maximum(n - 1, 0), kvb)),
            pl.BlockSpec((None, BLOCK, dkv), lambda b, n: (b, n, kvb)),
            pl.BlockSpec((None, BLOCK, dkv), lambda b, n: (b, jnp.minimum(n + 1, nb - 1), kvb)),
            pl.BlockSpec((None, N_KV, 3 * BLOCK, GROUP * BLOCK),
                         lambda b, n: (jnp.where(n == 0, 0, jnp.where(n == nb - 1, 2, 1)), 0, 0, 0)),
            pl.BlockSpec((N_KV, 1, GROUP * BLOCK), lambda b, n: (0, 0, 0)),
        ],
        out_specs=pl.BlockSpec((None, None, dq, BLOCK), lambda b, n: (b, n, 0, 0)),
        out_shape=jax.ShapeDtypeStruct((bsz, nb, dq, BLOCK), BF16),
        compiler_params=_cparams(("parallel", "parallel")),
        name="window_attention",
    )(qkv, qkv, qkv, qkv, bias_t, sink_row)


ATTN_OUT_BLOCKS = 8


def _attn_out_kernel(ot_ref, w_ref, x_ref, o_ref):
    for i in range(ATTN_OUT_BLOCKS):
        rows = slice(i * BLOCK, (i + 1) * BLOCK)
        y = lax.dot_general(ot_ref[i], w_ref[...], (((0,), (0,)), ((), ())),
                            preferred_element_type=F32)
        o_ref[rows, :] = x_ref[rows, :] + y


def _attn_out_proj(o_t, w, x2):
    nblk, dq, _ = o_t.shape
    m, d = x2.shape
    tm = ATTN_OUT_BLOCKS * BLOCK
    return pl.pallas_call(
        _attn_out_kernel,
        grid=(nblk // ATTN_OUT_BLOCKS,),
        in_specs=[
            pl.BlockSpec((ATTN_OUT_BLOCKS, dq, BLOCK), lambda i: (i, 0, 0)),
            pl.BlockSpec((dq, d), lambda i: (0, 0)),
            pl.BlockSpec((tm, d), lambda i: (i, 0)),
        ],
        out_specs=pl.BlockSpec((tm, d), lambda i: (i, 0)),
        out_shape=jax.ShapeDtypeStruct((m, d), F32),
        compiler_params=_cparams(("parallel",)),
        name="attn_out_proj",
    )(o_t, w, x2)


def _head_tile_perm():
    cols = []
    for tile in range(N_HEADS // 2):
        j, g = tile // GROUP, tile % GROUP
        for half in range(2):
            h = (2 * j + half) * GROUP + g
            cols.extend(range(h * HEAD_DIM, (h + 1) * HEAD_DIM))
    return jnp.asarray(cols, jnp.int32)


def _attn_layer(x, g_mix, w_qkv, q_gain, k_gain, sink, w_o, rel_table):
    bsz, seq, d = x.shape
    x2 = x.reshape(bsz * seq, d)
    dq = N_HEADS * HEAD_DIM
    dk = N_KV * HEAD_DIM
    perm = _head_tile_perm()
    w = jnp.concatenate([w_qkv[:, :dq][:, perm], w_qkv[:, dq:]], axis=1).astype(BF16)
    gain = jnp.concatenate([jnp.tile(q_gain * (HEAD_DIM ** -0.5), N_HEADS), jnp.tile(k_gain, N_KV),
                            jnp.ones((dk,), F32)])[None]
    flag = jnp.concatenate([jnp.ones((dq + dk,), F32), jnp.zeros((dk,), F32)])[None]
    qkv = _qkv_proj(x2, g_mix[None], w, gain, flag)
    bias_t = _rel_bias(rel_table).reshape(N_KV, GROUP * BLOCK, 3 * BLOCK).transpose(0, 2, 1)
    key = jnp.arange(3 * BLOCK)[None, :, None]
    bias_t = jnp.stack([jnp.where(key < BLOCK, -1e30, bias_t), bias_t,
                        jnp.where(key >= 2 * BLOCK, -1e30, bias_t)])
    sink_row = jnp.repeat(sink, BLOCK).reshape(N_KV, 1, GROUP * BLOCK)
    o_t = _attention(qkv.reshape(bsz, seq, dq + 2 * dk), bias_t, sink_row)
    y = _attn_out_proj(o_t.reshape(bsz * (seq // BLOCK), dq, BLOCK), w_o[perm, :].astype(BF16), x2)
    return y.reshape(bsz, seq, d)


def _pool_kernel(xp_ref, xc_ref, xn_ref, g_ref, w_ref, b_ref, sc_ref, o_ref, *, tm):
    i = pl.program_id(1)
    g = g_ref[...]
    rows = tm + 2 * SUBLANES
    t = i * tm - SUBLANES + lax.broadcasted_iota(jnp.int32, (rows, 1), 0)
    inside = (t >= 0) & (t < SEQ)
    xa = jnp.concatenate([_rms(xp_ref[...], g), _rms(xc_ref[...], g), _rms(xn_ref[...], g)], axis=0)
    xa = jnp.where(inside, xa, 0.0)
    tc = t[SUBLANES:SUBLANES + tm]
    outs = []
    for gi, wdt in enumerate(POOL_WINDOWS):
        r = wdt // 2
        xg = xa[:, gi * POOL_GROUP:(gi + 1) * POOL_GROUP]
        run = xg
        span = 1
        while span < 2 * r:
            run = run + pltpu.roll(run, rows - span, 0)
            span *= 2
        win = (pltpu.roll(run, r, 0) + pltpu.roll(xg, rows - r, 0))[SUBLANES:SUBLANES + tm]
        cnt = (jnp.minimum(tc + r + 1, SEQ) - jnp.maximum(tc - r, 0)).astype(F32)
        dlt = win / cnt - xg[SUBLANES:SUBLANES + tm]
        outs.append(jnp.dot(dlt.astype(BF16), w_ref[gi], preferred_element_type=F32))
    y = (jnp.concatenate(outs, axis=-1) + b_ref[...]) * sc_ref[...]
    o_ref[...] = xc_ref[...] + y


def _pool_layer(x, g_mix, w_grp, b, scale, tm=512):
    bsz, seq, d = x.shape
    hb = tm // SUBLANES
    last = seq // SUBLANES - 1
    ng = len(POOL_WINDOWS)
    return pl.pallas_call(
        functools.partial(_pool_kernel, tm=tm),
        grid=(bsz, seq // tm),
        in_specs=[
            pl.BlockSpec((None, SUBLANES, d), lambda bi, i: (bi, jnp.maximum(i * hb - 1, 0), 0)),
            pl.BlockSpec((None, tm, d), lambda bi, i: (bi, i, 0)),
            pl.BlockSpec((None, SUBLANES, d), lambda bi, i: (bi, jnp.minimum((i + 1) * hb, last), 0)),
            pl.BlockSpec((1, d), lambda bi, i: (0, 0)),
            pl.BlockSpec((ng, POOL_GROUP, POOL_GROUP), lambda bi, i: (0, 0, 0)),
            pl.BlockSpec((1, d), lambda bi, i: (0, 0)),
            pl.BlockSpec((1, d), lambda bi, i: (0, 0)),
        ],
        out_specs=pl.BlockSpec((None, tm, d), lambda bi, i: (bi, i, 0)),
        out_shape=jax.ShapeDtypeStruct((bsz, seq, d), F32),
        compiler_params=_cparams(("parallel", "parallel")),
        name="pool_mixer",
    )(x, x, x, g_mix[None], w_grp.astype(BF16), b[None], scale[None])


def kernel(x, norm_mix, norm_ffn, hy_w_in, hy_b_in, hy_conv_w, hy_conv_b, hy_f_w1, hy_f_b1, hy_f_wi, hy_f_bi, hy_f_w3, hy_f_freq, hy_f_bias, hy_w_out, hy_b_out, at_w_qkv, at_q_gain, at_k_gain, at_sink, at_w_o, rel_table, pl_w, pl_b, pl_scale, ff_w_gate, ff_w_up, ff_w_down):
    bsz, seq, d = x.shape
    f1, gm = _dft_mats()
    f1_full = f1.astype(BF16)
    f1_half = f1[:, :FFT_N1 // 2].astype(BF16)
    fi = (f1[:, :FFT_N1 // 2].T * (2.0 / FFT_N)).astype(BF16)
    gh = jnp.swapaxes(gm, 1, 2).astype(BF16)
    gm = gm.astype(BF16)
    fq = jnp.linspace(1e-4, HY_BANDS - 1, HY_BANDS, dtype=F32)[None, :]
    deltas = jnp.abs(jnp.linspace(HY_MIN_DECAY, HY_MAX_DECAY, D_MODEL, dtype=F32))
    dl2 = jnp.concatenate([deltas, deltas])[None, :]
    consts = (f1_full, f1_half, gm, gh, fi, fq, dl2)

    for i in range(DEPTH):
        kind, s = i % 3, i // 3
        if kind == 0:
            x = _hyena_layer(x, norm_mix[i], hy_w_in[s], hy_b_in[s], hy_conv_w[s], hy_conv_b[s],
                             hy_f_w1[s], hy_f_b1[s], hy_f_wi[s], hy_f_bi[s], hy_f_w3[s],
                             hy_f_freq[s], hy_f_bias[s], hy_w_out[s], hy_b_out[s], consts)
        elif kind == 1:
            x = _attn_layer(x, norm_mix[i], at_w_qkv[s], at_q_gain[s], at_k_gain[s], at_sink[s],
                            at_w_o[s], rel_table)
        else:
            x = _pool_layer(x, norm_mix[i], pl_w[s], pl_b[s], pl_scale[s])
        x2 = _ffn(x.reshape(bsz * seq, d), norm_ffn[i][None], ff_w_gate[i].astype(BF16),
                  ff_w_up[i].astype(BF16), ff_w_down[i].astype(BF16))
        x = x2.reshape(bsz, seq, d)
    return x
```

```python
import functools
import math

import jax
import jax.numpy as jnp
from jax import lax
from jax.experimental import pallas as pl
from jax.experimental.pallas import tpu as pltpu

F32 = jnp.float32
BF16 = jnp.bfloat16

D_MODEL = 1024
BATCH = 4
SEQ = 8192
DEPTH = 4
EPS = 1e-6

HY_SHORT = 3
HY_EMB = 33
HY_BANDS = 16
HY_FW = 64
HY_INNER = 2
HY_MIN_DECAY = math.log(1e-2) / 1.5
HY_MAX_DECAY = math.log(1e-2) / 0.3

HEAD_DIM = 64
N_HEADS = 16
N_KV = 4
GROUP = 4
WINDOW = 128
BLOCK = 128
REL_BUCKETS = 32
REL_MAX_DIST = 128

POOL_WINDOWS = (2, 4, 8, 16)
POOL_GROUP = 256
D_FF = 2816

SUBLANES = 8
LANES = 128
VMEM_LIMIT = 56 * 1024 * 1024

FFT_N = 2 * SEQ
FFT_N1 = 256
FFT_N2 = 64
FFT_K1 = FFT_N1 // 2
SLAB = 2 * FFT_K1 + SUBLANES
CONV_DT = 256
DFT_UNROLL = 16


def _cparams(sem):
    return pltpu.CompilerParams(dimension_semantics=sem, vmem_limit_bytes=VMEM_LIMIT)


def _rms(x, g):
    return x * lax.rsqrt(jnp.mean(x * x, axis=-1, keepdims=True) + EPS) * g


FFN_CHUNK = 256


FFN_CHUNKS = D_FF // FFN_CHUNK


def _ffn_kernel(x_ref, g_ref, wg_ref, wu_ref, wd_ref, o_ref, wgb_ref, wub_ref, wdb_ref, h_ref):
    s = pl.program_id(0)

    @pl.when(s < FFN_CHUNKS)
    def _():
        wgb_ref[s] = wg_ref[...].astype(BF16)
        wub_ref[s] = wu_ref[...].astype(BF16)
        wdb_ref[pl.ds(pl.multiple_of(s * FFN_CHUNK, FFN_CHUNK), FFN_CHUNK), :] = wd_ref[...].astype(BF16)

    @pl.when(s >= FFN_CHUNKS)
    def _():
        x = x_ref[...]
        xn = _rms(x, g_ref[...]).astype(BF16)
        for c in range(FFN_CHUNKS):
            gate = jnp.dot(xn, wgb_ref[c], preferred_element_type=F32)
            up = jnp.dot(xn, wub_ref[c], preferred_element_type=F32)
            h_ref[:, c * FFN_CHUNK:(c + 1) * FFN_CHUNK] = (gate * jax.nn.sigmoid(gate) * up).astype(BF16)
        o_ref[...] = x + jnp.dot(h_ref[...], wdb_ref[...], preferred_element_type=F32)


def _ffn(x2, g, wg, wu, wd, tm=512):
    m, d = x2.shape
    f = wg.shape[1]
    tile = lambda s: jnp.maximum(s - FFN_CHUNKS, 0)
    chunk = lambda s: jnp.minimum(s, FFN_CHUNKS - 1)
    return pl.pallas_call(
        _ffn_kernel,
        grid=(FFN_CHUNKS + m // tm,),
        in_specs=[
            pl.BlockSpec((tm, d), lambda s: (tile(s), 0)),
            pl.BlockSpec((1, d), lambda s: (0, 0)),
            pl.BlockSpec((d, FFN_CHUNK), lambda s: (0, chunk(s))),
            pl.BlockSpec((d, FFN_CHUNK), lambda s: (0, chunk(s))),
            pl.BlockSpec((FFN_CHUNK, d), lambda s: (chunk(s), 0)),
        ],
        out_specs=pl.BlockSpec((tm, d), lambda s: (tile(s), 0)),
        out_shape=jax.ShapeDtypeStruct((m, d), F32),
        scratch_shapes=[pltpu.VMEM((FFN_CHUNKS, d, FFN_CHUNK), BF16),
                        pltpu.VMEM((FFN_CHUNKS, d, FFN_CHUNK), BF16),
                        pltpu.VMEM((f, d), BF16),
                        pltpu.VMEM((tm, f), BF16)],
        compiler_params=_cparams(("arbitrary",)),
        name="ffn",
    )(x2, g, wg, wu, wd)


INPROJ_N1 = 16
INPROJ_CHUNK = 512


def _slab_perm():
    nb = INPROJ_N1
    r = jnp.arange(FFT_N2 * nb)
    src = FFT_N2 * (r % nb) + r // nb
    return (src[:, None] == r[None, :]).astype(BF16)


def _inproj_kernel(xp_ref, xc_ref, xn_ref, g_ref, pm_ref, w_ref, b_ref, cw_ref, cb_ref, o_ref, xs_ref):
    i = pl.program_id(1)
    nb = INPROJ_N1
    tm = FFT_N2 * nb
    g = g_ref[...]
    xn = _rms(xc_ref[...], g).astype(BF16)
    xs_ref[:tm, :] = jnp.dot(pm_ref[...], xn, preferred_element_type=F32).astype(BF16)
    halo = jnp.concatenate([xp_ref[...], xn_ref[...]], axis=0)
    xs_ref[tm:, :] = _rms(halo, g).astype(BF16)

    xs = xs_ref[...]
    row = lax.broadcasted_iota(jnp.int32, (nb, 1), 0)
    inside_before = i > 0
    inside_after = i < pl.num_programs(1) - 1
    for c in range(w_ref.shape[1] // INPROJ_CHUNK):
        cols = slice(c * INPROJ_CHUNK, (c + 1) * INPROJ_CHUNK)
        p = jnp.dot(xs, w_ref[:, cols], preferred_element_type=F32) + b_ref[:, cols]
        cur = p[:tm]
        before = jnp.where(inside_before, p[tm + SUBLANES - 1:tm + SUBLANES], 0.0)
        after = jnp.where(inside_after, p[tm + SUBLANES:tm + SUBLANES + 1], 0.0)
        first = jnp.where(row == 0, before, pltpu.roll(cur[tm - nb:], 1, 0))
        last = jnp.where(row == nb - 1, after, pltpu.roll(cur[:nb], nb - 1, 0))
        dn = jnp.concatenate([first, cur[:tm - nb]], axis=0)
        up = jnp.concatenate([cur[nb:], last], axis=0)
        cw = cw_ref[:, cols]
        out = cb_ref[:, cols] + cw[0:1] * dn + cw[1:2] * cur + cw[2:3] * up
        o_ref[:, :, cols] = out.reshape(FFT_N2, nb, INPROJ_CHUNK).astype(o_ref.dtype)


def _inproj_conv(x, g, w, b, cw, cb):
    bsz, seq, d = x.shape
    n = w.shape[1]
    nb = INPROJ_N1
    tm = FFT_N2 * nb
    hb = tm // SUBLANES
    last = seq // SUBLANES - 1
    const = lambda bi, i: (0, 0)
    return pl.pallas_call(
        _inproj_kernel,
        grid=(bsz, seq // tm),
        in_specs=[
            pl.BlockSpec((None, SUBLANES, d), lambda bi, i: (bi, jnp.maximum(i * hb - 1, 0), 0)),
            pl.BlockSpec((None, tm, d), lambda bi, i: (bi, i, 0)),
            pl.BlockSpec((None, SUBLANES, d), lambda bi, i: (bi, jnp.minimum((i + 1) * hb, last), 0)),
            pl.BlockSpec((1, d), const),
            pl.BlockSpec((tm, tm), const, pipeline_mode=pl.Buffered(1)),
            pl.BlockSpec((d, n), const, pipeline_mode=pl.Buffered(1)),
            pl.BlockSpec((1, n), const),
            pl.BlockSpec((HY_SHORT, n), const),
            pl.BlockSpec((1, n), const),
        ],
        out_specs=pl.BlockSpec((None, FFT_N2, nb, n), lambda bi, i: (bi, 0, i, 0)),
        out_shape=jax.ShapeDtypeStruct((bsz, FFT_N2, seq // FFT_N2, n), BF16),
        scratch_shapes=[pltpu.VMEM((tm + 2 * SUBLANES, d), BF16)],
        compiler_params=_cparams(("parallel", "parallel")),
        name="hyena_inproj",
    )(x, x, x, g, _slab_perm(), w, b, cw, cb)


def _dft_mats():
    i32 = jnp.int32
    k1 = jnp.arange(FFT_K1, dtype=i32)[:, None]
    n1 = jnp.arange(FFT_N1, dtype=i32)[None, :]
    m = ((2 * k1 + 1) * n1) % (2 * FFT_N1)
    ang = m.astype(F32) * (math.pi / FFT_N1)
    f1 = jnp.concatenate([jnp.cos(ang), -jnp.sin(ang)], axis=0)
    k1 = jnp.arange(FFT_K1, dtype=i32)[:, None, None]
    k2 = jnp.arange(FFT_N2, dtype=i32)[None, :, None]
    n2 = jnp.arange(FFT_N2, dtype=i32)[None, None, :]
    m = ((2 * (k1 + FFT_N1 * k2) + 1) * n2) % (2 * FFT_N)
    ang = m.astype(F32) * (math.pi / FFT_N)
    c, s = jnp.cos(ang), jnp.sin(ang)
    gm = jnp.concatenate(
        [jnp.concatenate([c, s], axis=2), jnp.concatenate([-s, c], axis=2)], axis=1)
    return f1, gm


def _hdot(a, b):
    return jnp.dot(a, b, preferred_element_type=F32, precision=lax.Precision.HIGHEST)


def _tap_lag(n):
    return jnp.where(n >= SEQ, FFT_N - n, n).astype(F32)


def _split_bf16(a):
    hi = a.astype(BF16)
    return hi, (a - hi.astype(F32)).astype(BF16)


def _filter_mlp_kernel(fq_ref, w1t_ref, w1c_ref, w1s_ref, b1_ref, wi_ref, bi_ref, fr_ref,
                       h3_ref, *, slabs):
    cols = slabs * FFT_N1
    c = lax.broadcasted_iota(jnp.int32, (1, cols), 1)
    n2 = pl.program_id(0) * slabs + c // FFT_N1
    pos = _tap_lag(FFT_N2 * (c % FFT_N1) + n2)
    t = pos * (1.0 / (SEQ - 1))
    ang = (2.0 * math.pi / SEQ) * pos * fq_ref[...]
    fr = fr_ref[...]
    pre = (w1t_ref[...] * t + _hdot(w1c_ref[...], jnp.cos(ang))
           + _hdot(w1s_ref[...], -jnp.sin(ang)))
    h = jnp.sin(fr * (pre + b1_ref[...]))
    for l in range(HY_INNER):
        h = jnp.sin(fr * (_hdot(wi_ref[l], h) + bi_ref[l]))
    hi, lo = _split_bf16(h)
    h3 = jnp.concatenate([hi, lo, hi], axis=0)
    for k in range(slabs):
        h3_ref[k] = h3[:, k * FFT_N1:(k + 1) * FFT_N1]


def _filter_mlp(fq, w1, b1, wi, bi, fr, slabs=8):
    w1t, w1c, w1s = w1[0:1].T, w1[1:1 + HY_BANDS].T, w1[1 + HY_BANDS:].T
    col = lambda a: a.reshape(-1, 1)
    const2 = lambda i: (0, 0)
    const3 = lambda i: (0, 0, 0)
    return pl.pallas_call(
        functools.partial(_filter_mlp_kernel, slabs=slabs),
        grid=(FFT_N2 // slabs,),
        in_specs=[
            pl.BlockSpec((HY_BANDS, 1), const2),
            pl.BlockSpec((HY_FW, 1), const2),
            pl.BlockSpec((HY_FW, HY_BANDS), const2),
            pl.BlockSpec((HY_FW, HY_BANDS), const2),
            pl.BlockSpec((HY_FW, 1), const2),
            pl.BlockSpec((HY_INNER, HY_FW, HY_FW), const3),
            pl.BlockSpec((HY_INNER, HY_FW, 1), const3),
            pl.BlockSpec((HY_FW, 1), const2),
        ],
        out_specs=pl.BlockSpec((slabs, 3 * HY_FW, FFT_N1), lambda i: (i, 0, 0)),
        out_shape=jax.ShapeDtypeStruct((FFT_N2, 3 * HY_FW, FFT_N1), BF16),
        compiler_params=_cparams(("parallel",)),
        name="hyena_filter_mlp",
    )(col(fq), w1t, w1c, w1s, col(b1), jnp.swapaxes(wi, 1, 2), bi[:, :, None], col(fr))


def _slab_store(s_ref, n2, a):
    rows = pl.ds(pl.multiple_of(n2 * SLAB, SUBLANES), 2 * FFT_K1)
    for l in range(s_ref.shape[0]):
        s_ref[l, rows, :] = a[:, l * LANES:(l + 1) * LANES]


def _slab_load(s_ref, n2):
    rows = pl.ds(pl.multiple_of(n2 * SLAB, SUBLANES), 2 * FFT_K1)
    return jnp.concatenate([s_ref[l, rows, :] for l in range(s_ref.shape[0])], axis=1).astype(BF16)


def _k1_load(s_ref, k1):
    re = pl.ds(k1, FFT_N2, stride=SLAB)
    im = pl.ds(FFT_K1 + k1, FFT_N2, stride=SLAB)
    tiles = [jnp.concatenate([s_ref[l, re, :], s_ref[l, im, :]], axis=0) for l in range(s_ref.shape[0])]
    return jnp.concatenate(tiles, axis=1).astype(BF16)


def _k1_store(s_ref, k1, b):
    for l in range(s_ref.shape[0]):
        cols = slice(l * LANES, (l + 1) * LANES)
        s_ref[l, pl.ds(k1, FFT_N2, stride=SLAB), :] = b[:FFT_N2, cols]
        s_ref[l, pl.ds(FFT_K1 + k1, FFT_N2, stride=SLAB), :] = b[FFT_N2:, cols]


def _stage_scratch(dt):
    return pltpu.VMEM((dt // LANES, FFT_N2 * SLAB, LANES), F32)


def _stage1(src_ref, f1_ref, s_ref):
    def body(n2, c):
        _slab_store(s_ref, n2, jnp.dot(f1_ref[...], src_ref[n2].astype(BF16), preferred_element_type=F32))
        return c
    lax.fori_loop(0, FFT_N2, body, 0, unroll=DFT_UNROLL)


def _filt_dft_kernel(h3_ref, w3_ref, dl_ref, f1_ref, gm_ref, kf_ref, s_ref):
    half = FFT_N1 // 2
    dt = dl_ref.shape[1]
    n1 = lax.broadcasted_iota(jnp.int32, (FFT_N1, 1), 0)
    dl = dl_ref[...]

    def body(n2, l1):
        both = lax.dot_general(h3_ref[n2], w3_ref[...], (((0,), (0,)), ((), ())),
                               preferred_element_type=F32)
        fwd = both[:half, :dt]
        bwd = both[half:, dt:]
        n = FFT_N2 * n1 + n2
        t = _tap_lag(n) * (1.0 / (SEQ - 1))
        taps = jnp.concatenate([fwd, -bwd], axis=0) * jnp.exp(-t * dl)
        taps = jnp.where(n == SEQ, 0.0, taps)
        _slab_store(s_ref, n2, jnp.dot(f1_ref[...], taps.astype(BF16), preferred_element_type=F32))
        return l1 + jnp.sum(jnp.abs(taps), axis=0, keepdims=True)
    l1 = lax.fori_loop(0, FFT_N2, body, jnp.zeros(dl.shape, F32), unroll=DFT_UNROLL)
    inv = 1.0 / l1

    def stage2(k1, c):
        x = jnp.dot(gm_ref[k1], _k1_load(s_ref, k1), preferred_element_type=F32)
        rows = pl.ds(pl.multiple_of(k1 * 2 * FFT_N2, 2 * FFT_N2), 2 * FFT_N2)
        kf_ref[rows, :] = (x * inv).astype(kf_ref.dtype)
        return c
    lax.fori_loop(0, FFT_K1, stage2, 0, unroll=DFT_UNROLL)


def _filter_spectrum(h3, w3, dl2, f1_full, gm):
    ncol = 2 * D_MODEL
    dt = CONV_DT
    nct = ncol // dt
    tiles = lambda a: a.reshape(HY_FW, 2, nct, dt).transpose(2, 0, 1, 3).reshape(nct, HY_FW, 2 * dt)
    w3h, w3l = _split_bf16(w3)
    w3cat = jnp.concatenate([tiles(w3h), tiles(w3h), tiles(w3l)], axis=1)
    return pl.pallas_call(
        _filt_dft_kernel,
        grid=(nct,),
        in_specs=[
            pl.BlockSpec((FFT_N2, 3 * HY_FW, FFT_N1), lambda j: (0, 0, 0)),
            pl.BlockSpec((None, 3 * HY_FW, 2 * dt), lambda j: (j, 0, 0)),
            pl.BlockSpec((1, dt), lambda j: (0, j)),
            pl.BlockSpec((2 * FFT_K1, FFT_N1), lambda j: (0, 0)),
            pl.BlockSpec((FFT_K1, 2 * FFT_N2, 2 * FFT_N2), lambda j: (0, 0, 0),
                         pipeline_mode=pl.Buffered(1)),
        ],
        out_specs=pl.BlockSpec((FFT_K1 * 2 * FFT_N2, dt), lambda j: (0, j)),
        out_shape=jax.ShapeDtypeStruct((FFT_K1 * 2 * FFT_N2, ncol), BF16),
        scratch_shapes=[_stage_scratch(dt)],
        compiler_params=_cparams(("parallel",)),
        name="hyena_filter_spectrum",
    )(h3, w3cat, dl2, f1_full, gm)


def _longconv_kernel(z_ref, g_ref, fb_ref, kf_ref, f1_ref, gm_ref, gh_ref, fi_ref, o_ref, s_ref):
    _stage1(z_ref, f1_ref, s_ref)

    def mid(k1, c):
        x = jnp.dot(gm_ref[k1], _k1_load(s_ref, k1), preferred_element_type=F32)
        kf = kf_ref[pl.ds(pl.multiple_of(k1 * 2 * FFT_N2, 2 * FFT_N2), 2 * FFT_N2), :].astype(F32)
        xr, xi = x[:FFT_N2], x[FFT_N2:]
        kr, ki = kf[:FFT_N2], kf[FFT_N2:]
        y = jnp.concatenate([xr * kr - xi * ki, xr * ki + xi * kr], axis=0).astype(BF16)
        b = jnp.dot(gh_ref[k1], y, preferred_element_type=F32)
        _k1_store(s_ref, k1, b)
        return c
    lax.fori_loop(0, FFT_K1, mid, 0, unroll=2 * DFT_UNROLL)

    fb = fb_ref[...]

    def last(n2, c):
        y = jnp.dot(fi_ref[...], _slab_load(s_ref, n2), preferred_element_type=F32)
        o_ref[n2] = (g_ref[n2] * (y + fb * z_ref[n2])).astype(o_ref.dtype)
        return c
    lax.fori_loop(0, FFT_N2, last, 0, unroll=DFT_UNROLL)


def _long_conv_gate(z, zcol, g, gcol, fb, kf, kcol, f1_half, gm, gh, fi):
    bsz = z.shape[0]
    dt = CONV_DT
    nt = D_MODEL // dt
    n1 = FFT_N1 // 2
    const3 = lambda j, bi: (0, 0, 0)
    return pl.pallas_call(
        _longconv_kernel,
        grid=(nt, bsz),
        in_specs=[
            pl.BlockSpec((None, FFT_N2, n1, dt), lambda j, bi: (bi, 0, 0, zcol + j)),
            pl.BlockSpec((None, FFT_N2, n1, dt), lambda j, bi: (bi, 0, 0, gcol + j),
                         pipeline_mode=pl.Buffered(1)),
            pl.BlockSpec((1, dt), lambda j, bi: (0, j)),
            pl.BlockSpec((FFT_K1 * 2 * FFT_N2, dt), lambda j, bi: (0, kcol + j),
                         pipeline_mode=pl.Buffered(1)),
            pl.BlockSpec((2 * FFT_K1, FFT_N1 // 2), lambda j, bi: (0, 0)),
            pl.BlockSpec((FFT_K1, 2 * FFT_N2, 2 * FFT_N2), const3, pipeline_mode=pl.Buffered(1)),
            pl.BlockSpec((FFT_K1, 2 * FFT_N2, 2 * FFT_N2), const3, pipeline_mode=pl.Buffered(1)),
            pl.BlockSpec((FFT_N1 // 2, 2 * FFT_K1), lambda j, bi: (0, 0)),
        ],
        out_specs=pl.BlockSpec((None, FFT_N2, n1, dt), lambda j, bi: (bi, 0, 0, j)),
        out_shape=jax.ShapeDtypeStruct((bsz, FFT_N2, n1, D_MODEL), BF16),
        scratch_shapes=[_stage_scratch(dt)],
        compiler_params=_cparams(("parallel", "arbitrary")),
        name="hyena_long_conv",
    )(z, g, fb, kf, f1_half, gm, gh, fi)


def _hyena_out_kernel(z_ref, pm_ref, w_ref, b_ref, x_ref, o_ref):
    z = z_ref[...].reshape(x_ref.shape)
    a = jnp.dot(pm_ref[...], z, preferred_element_type=F32).astype(BF16)
    acc = jnp.dot(a, w_ref[...], preferred_element_type=F32)
    o_ref[...] = x_ref[...] + acc + b_ref[...]


def _hyena_out_proj(z, w, b, x):
    bsz, seq, d = x.shape
    nb = INPROJ_N1
    tm = FFT_N2 * nb
    return pl.pallas_call(
        _hyena_out_kernel,
        grid=(bsz, seq // tm),
        in_specs=[
            pl.BlockSpec((None, FFT_N2, nb, d), lambda bi, i: (bi, 0, i, 0)),
            pl.BlockSpec((tm, tm), lambda bi, i: (0, 0)),
            pl.BlockSpec((d, d), lambda bi, i: (0, 0)),
            pl.BlockSpec((1, d), lambda bi, i: (0, 0)),
            pl.BlockSpec((None, tm, d), lambda bi, i: (bi, i, 0)),
        ],
        out_specs=pl.BlockSpec((None, tm, d), lambda bi, i: (bi, i, 0)),
        out_shape=jax.ShapeDtypeStruct((bsz, seq, d), F32),
        compiler_params=_cparams(("parallel", "parallel")),
        name="hyena_out_proj",
    )(z, _slab_perm().T, w, b, x)


def _hyena_layer(x, g_mix, w_in, b_in, cw, cb, fw1, fb1, fwi, fbi, fw3, freq, fbias, w_out, b_out, consts):
    f1_full, f1_half, gm, gh, fi, fq, dl2 = consts
    bsz, seq, d = x.shape
    pc = _inproj_conv(x, g_mix[None], w_in.astype(BF16), b_in[None], cw, cb[None])
    h3 = _filter_mlp(fq, fw1, fb1, fwi, fbi, freq)
    kf = _filter_spectrum(h3, fw3, dl2, f1_full, gm)
    nt = D_MODEL // CONV_DT
    z1 = _long_conv_gate(pc, 2 * nt, pc, 0, fbias[0:1], kf, 0, f1_half, gm, gh, fi)
    z2 = _long_conv_gate(z1, 0, pc, nt, fbias[1:2], kf, nt, f1_half, gm, gh, fi)
    return _hyena_out_proj(z2, w_out.astype(BF16), b_out[None], x)


def _relbias_kernel(tab_ref, bucket_ref, band_ref, o_ref):
    bucket = bucket_ref[...]
    band = band_ref[...]
    for h in range(N_HEADS):
        acc = jnp.zeros(bucket.shape, F32)
        for k in range(REL_BUCKETS):
            acc = jnp.where(bucket == k, tab_ref[k, h], acc)
        o_ref[h] = jnp.where(band > 0, acc, -1e30)


def _rel_bias(rel_table):
    a = jnp.arange(BLOCK)[:, None]
    j = jnp.arange(3 * BLOCK)[None, :]
    rel = j - BLOCK - a
    half = REL_BUCKETS // 2
    exact = half // 2
    n = jnp.abs(rel)
    nf = jnp.maximum(n, 1).astype(F32)
    large = exact + (jnp.log(nf / exact) / math.log(REL_MAX_DIST / exact) * (half - exact)).astype(jnp.int32)
    large = jnp.minimum(large, half - 1)
    bucket = (jnp.where(rel > 0, half, 0) + jnp.where(n < exact, n, large)).astype(jnp.int32)
    band = (n <= WINDOW).astype(jnp.int32)
    return pl.pallas_call(
        _relbias_kernel,
        in_specs=[
            pl.BlockSpec(memory_space=pltpu.SMEM),
            pl.BlockSpec((BLOCK, 3 * BLOCK), lambda: (0, 0)),
            pl.BlockSpec((BLOCK, 3 * BLOCK), lambda: (0, 0)),
        ],
        out_specs=pl.BlockSpec((N_HEADS, BLOCK, 3 * BLOCK), lambda: (0, 0, 0)),
        out_shape=jax.ShapeDtypeStruct((N_HEADS, BLOCK, 3 * BLOCK), F32),
        name="rel_bias",
    )(rel_table, bucket, band)


def _qkv_kernel(x_ref, g_ref, w_ref, gain_ref, flag_ref, bd_ref, o_ref, xn_ref):
    @pl.when(pl.program_id(1) == 0)
    def _():
        xn_ref[...] = _rms(x_ref[...], g_ref[...]).astype(BF16)

    acc = jnp.dot(xn_ref[...], w_ref[...], preferred_element_type=F32)
    hi, lo = _split_bf16(acc * acc)
    ssq = (jnp.dot(hi, bd_ref[...], preferred_element_type=F32)
           + jnp.dot(lo, bd_ref[...], preferred_element_type=F32))
    normed = acc * lax.rsqrt(ssq * (1.0 / HEAD_DIM) + EPS) * gain_ref[...]
    o_ref[...] = jnp.where(flag_ref[...] > 0, normed, acc).astype(o_ref.dtype)


def _qkv_proj(x2, g, w, gain, flag, tm=1024, tn=512):
    m, k = x2.shape
    n = w.shape[1]
    eye = jnp.arange(tn)[:, None] // HEAD_DIM == jnp.arange(tn)[None, :] // HEAD_DIM
    return pl.pallas_call(
        _qkv_kernel,
        grid=(m // tm, n // tn),
        in_specs=[
            pl.BlockSpec((tm, k), lambda i, j: (i, 0)),
            pl.BlockSpec((1, k), lambda i, j: (0, 0)),
            pl.BlockSpec((k, tn), lambda i, j: (0, j)),
            pl.BlockSpec((1, tn), lambda i, j: (0, j)),
            pl.BlockSpec((1, tn), lambda i, j: (0, j)),
            pl.BlockSpec((tn, tn), lambda i, j: (0, 0)),
        ],
        out_specs=pl.BlockSpec((tm, tn), lambda i, j: (i, j)),
        out_shape=jax.ShapeDtypeStruct((m, n), BF16),
        scratch_shapes=[pltpu.VMEM((tm, k), BF16)],
        compiler_params=_cparams(("parallel", "arbitrary")),
        name="qkv_proj",
    )(x2, g, w, gain, flag, eye.astype(BF16))


def _attn_kernel(q_ref, kp_ref, kc_ref, kn_ref, bias_ref, sink_ref, o_ref):
    kv = jnp.concatenate([kp_ref[...], kc_ref[...], kn_ref[...]], axis=0)
    lane = lax.broadcasted_iota(jnp.int32, (1, LANES), 1)
    keep = [(lane < HEAD_DIM).astype(F32).astype(BF16), (lane >= HEAD_DIM).astype(F32).astype(BF16)]
    ntile = N_KV // 2
    logits = []
    for kh in range(N_KV):
        j, half = kh // 2, kh % 2
        q4 = jnp.concatenate(
            [q_ref[:, (GROUP * j + g) * LANES:(GROUP * j + g + 1) * LANES] for g in range(GROUP)], axis=0)
        kt = kv[:, j * LANES:(j + 1) * LANES]
        logits.append(lax.dot_general(kt * keep[half], q4, (((1,), (1,)), ((), ())),
                                      preferred_element_type=F32))
    outs = []
    for kh in range(N_KV):
        j, half = kh // 2, kh % 2
        vt = kv[:, (ntile + j) * LANES:(ntile + j + 1) * LANES]
        lg = logits[kh] + bias_ref[kh]
        sink = sink_ref[kh]
        m = jnp.maximum(jnp.max(lg, axis=0, keepdims=True), sink)
        p = jnp.exp(lg - m)
        den = jnp.sum(p, axis=0, keepdims=True) + jnp.exp(sink - m)
        outs.append(lax.dot_general(vt * keep[half], p.astype(BF16), (((0,), (0,)), ((), ())),
                                    preferred_element_type=F32) * (1.0 / den))
    for j in range(ntile):
        acc = outs[2 * j] + outs[2 * j + 1]
        for g in range(GROUP):
            o_ref[(GROUP * j + g) * LANES:(GROUP * j + g + 1) * LANES, :] = (
                acc[:, g * BLOCK:(g + 1) * BLOCK].astype(o_ref.dtype))


def _attention(qkv, bias_t, sink_row):
    bsz, seq, _ = qkv.shape
    nb = seq // BLOCK
    dq = N_HEADS * HEAD_DIM
    dkv = 2 * N_KV * HEAD_DIM
    kvb = dq // dkv
    return pl.pallas_call(
        _attn_kernel,
        grid=(bsz, nb),
        in_specs=[
            pl.BlockSpec((None, BLOCK, dq), lambda b, n: (b, n, 0)),
            pl.BlockSpec((None, BLOCK, dkv), lambda b, n: (b, jnp.maximum(n - 1, 0), kvb)),
            pl.BlockSpec((None, BLOCK, dkv), lambda b, n: (b, n, kvb)),
            pl.BlockSpec((None, BLOCK, dkv), lambda b, n: (b, jnp.minimum(n + 1, nb - 1), kvb)),
            pl.BlockSpec((None, N_KV, 3 * BLOCK, GROUP * BLOCK),
                         lambda b, n: (jnp.where(n == 0, 0, jnp.where(n == nb - 1, 2, 1)), 0, 0, 0)),
            pl.BlockSpec((N_KV, 1, GROUP * BLOCK), lambda b, n: (0, 0, 0)),
        ],
        out_specs=pl.BlockSpec((None, None, dq, BLOCK), lambda b, n: (b, n, 0, 0)),
        out_shape=jax.ShapeDtypeStruct((bsz, nb, dq, BLOCK), BF16),
        compiler_params=_cparams(("parallel", "parallel")),
        name="window_attention",
    )(qkv, qkv, qkv, qkv, bias_t, sink_row)


ATTN_OUT_BLOCKS = 8


def _attn_out_kernel(ot_ref, w_ref, x_ref, o_ref):
    for i in range(ATTN_OUT_BLOCKS):
        rows = slice(i * BLOCK, (i + 1) * BLOCK)
        y = lax.dot_general(ot_ref[i], w_ref[...], (((0,), (0,)), ((), ())),
                            preferred_element_type=F32)
        o_ref[rows, :] = x_ref[rows, :] + y


def _attn_out_proj(o_t, w, x2):
    nblk, dq, _ = o_t.shape
    m, d = x2.shape
    tm = ATTN_OUT_BLOCKS * BLOCK
    return pl.pallas_call(
        _attn_out_kernel,
        grid=(nblk // ATTN_OUT_BLOCKS,),
        in_specs=[
            pl.BlockSpec((ATTN_OUT_BLOCKS, dq, BLOCK), lambda i: (i, 0, 0)),
            pl.BlockSpec((dq, d), lambda i: (0, 0)),
            pl.BlockSpec((tm, d), lambda i: (i, 0)),
        ],
        out_specs=pl.BlockSpec((tm, d), lambda i: (i, 0)),
        out_shape=jax.ShapeDtypeStruct((m, d), F32),
        compiler_params=_cparams(("parallel",)),
        name="attn_out_proj",
    )(o_t, w, x2)


def _head_tile_perm():
    cols = []
    for tile in range(N_HEADS // 2):
        j, g = tile // GROUP, tile % GROUP
        for half in range(2):
            h = (2 * j + half) * GROUP + g
            cols.extend(range(h * HEAD_DIM, (h + 1) * HEAD_DIM))
    return jnp.asarray(cols, jnp.int32)


def _attn_layer(x, g_mix, w_qkv, q_gain, k_gain, sink, w_o, rel_table):
    bsz, seq, d = x.shape
    x2 = x.reshape(bsz * seq, d)
    dq = N_HEADS * HEAD_DIM
    dk = N_KV * HEAD_DIM
    perm = _head_tile_perm()
    w = jnp.concatenate([w_qkv[:, :dq][:, perm], w_qkv[:, dq:]], axis=1).astype(BF16)
    gain = jnp.concatenate([jnp.tile(q_gain * (HEAD_DIM ** -0.5), N_HEADS), jnp.tile(k_gain, N_KV),
                            jnp.ones((dk,), F32)])[None]
    flag = jnp.concatenate([jnp.ones((dq + dk,), F32), jnp.zeros((dk,), F32)])[None]
    qkv = _qkv_proj(x2, g_mix[None], w, gain, flag)
    bias_t = _rel_bias(rel_table).reshape(N_KV, GROUP * BLOCK, 3 * BLOCK).transpose(0, 2, 1)
    key = jnp.arange(3 * BLOCK)[None, :, None]
    bias_t = jnp.stack([jnp.where(key < BLOCK, -1e30, bias_t), bias_t,
                        jnp.where(key >= 2 * BLOCK, -1e30, bias_t)])
    sink_row = jnp.repeat(sink, BLOCK).reshape(N_KV, 1, GROUP * BLOCK)
    o_t = _attention(qkv.reshape(bsz, seq, dq + 2 * dk), bias_t, sink_row)
    y = _attn_out_proj(o_t.reshape(bsz * (seq // BLOCK), dq, BLOCK), w_o[perm, :].astype(BF16), x2)
    return y.reshape(bsz, seq, d)


def _pool_kernel(xp_ref, xc_ref, xn_ref, g_ref, w_ref, b_ref, sc_ref, o_ref, *, tm):
    i = pl.program_id(1)
    g = g_ref[...]
    rows = tm + 2 * SUBLANES
    t = i * tm - SUBLANES + lax.broadcasted_iota(jnp.int32, (rows, 1), 0)
    inside = (t >= 0) & (t < SEQ)
    xa = jnp.concatenate([_rms(xp_ref[...], g), _rms(xc_ref[...], g), _rms(xn_ref[...], g)], axis=0)
    xa = jnp.where(inside, xa, 0.0)
    tc = t[SUBLANES:SUBLANES + tm]
    outs = []
    for gi, wdt in enumerate(POOL_WINDOWS):
        r = wdt // 2
        xg = xa[:, gi * POOL_GROUP:(gi + 1) * POOL_GROUP]
        run = xg
        span = 1
        while span < 2 * r:
            run = run + pltpu.roll(run, rows - span, 0)
            span *= 2
        win = (pltpu.roll(run, r, 0) + pltpu.roll(xg, rows - r, 0))[SUBLANES:SUBLANES + tm]
        cnt = (jnp.minimum(tc + r + 1, SEQ) - jnp.maximum(tc - r, 0)).astype(F32)
        dlt = win / cnt - xg[SUBLANES:SUBLANES + tm]
        outs.append(jnp.dot(dlt.astype(BF16), w_ref[gi], preferred_element_type=F32))
    y = (jnp.concatenate(outs, axis=-1) + b_ref[...]) * sc_ref[...]
    o_ref[...] = xc_ref[...] + y


def _pool_layer(x, g_mix, w_grp, b, scale, tm=512):
    bsz, seq, d = x.shape
    hb = tm // SUBLANES
    last = seq // SUBLANES - 1
    ng = len(POOL_WINDOWS)
    return pl.pallas_call(
        functools.partial(_pool_kernel, tm=tm),
        grid=(bsz, seq // tm),
        in_specs=[
            pl.BlockSpec((None, SUBLANES, d), lambda bi, i: (bi, jnp.maximum(i * hb - 1, 0), 0)),
            pl.BlockSpec((None, tm, d), lambda bi, i: (bi, i, 0)),
            pl.BlockSpec((None, SUBLANES, d), lambda bi, i: (bi, jnp.minimum((i + 1) * hb, last), 0)),
            pl.BlockSpec((1, d), lambda bi, i: (0, 0)),
            pl.BlockSpec((ng, POOL_GROUP, POOL_GROUP), lambda bi, i: (0, 0, 0)),
            pl.BlockSpec((1, d), lambda bi, i: (0, 0)),
            pl.BlockSpec((1, d), lambda bi, i: (0, 0)),
        ],
        out_specs=pl.BlockSpec((None, tm, d), lambda bi, i: (bi, i, 0)),
        out_shape=jax.ShapeDtypeStruct((bsz, seq, d), F32),
        compiler_params=_cparams(("parallel", "parallel")),
        name="pool_mixer",
    )(x, x, x, g_mix[None], w_grp.astype(BF16), b[None], scale[None])


def kernel(x, norm_mix, norm_ffn, hy_w_in, hy_b_in, hy_conv_w, hy_conv_b, hy_f_w1, hy_f_b1, hy_f_wi, hy_f_bi, hy_f_w3, hy_f_freq, hy_f_bias, hy_w_out, hy_b_out, at_w_qkv, at_q_gain, at_k_gain, at_sink, at_w_o, rel_table, pl_w, pl_b, pl_scale, ff_w_gate, ff_w_up, ff_w_down):
    bsz, seq, d = x.shape
    f1, gm = _dft_mats()
    f1_full = f1.astype(BF16)
    f1_half = f1[:, :FFT_N1 // 2].astype(BF16)
    fi = (f1[:, :FFT_N1 // 2].T * (2.0 / FFT_N)).astype(BF16)
    gh = jnp.swapaxes(gm, 1, 2).astype(BF16)
    gm = gm.astype(BF16)
    fq = jnp.linspace(1e-4, HY_BANDS - 1, HY_BANDS, dtype=F32)[None, :]
    deltas = jnp.abs(jnp.linspace(HY_MIN_DECAY, HY_MAX_DECAY, D_MODEL, dtype=F32))
    dl2 = jnp.concatenate([deltas, deltas])[None, :]
    consts = (f1_full, f1_half, gm, gh, fi, fq, dl2)

    for i in range(DEPTH):
        kind, s = i % 3, i // 3
        if kind == 0:
            x = _hyena_layer(x, norm_mix[i], hy_w_in[s], hy_b_in[s], hy_conv_w[s], hy_conv_b[s],
                             hy_f_w1[s], hy_f_b1[s], hy_f_wi[s], hy_f_bi[s], hy_f_w3[s],
                             hy_f_freq[s], hy_f_bias[s], hy_w_out[s], hy_b_out[s], consts)
        elif kind == 1:
            x = _attn_layer(x, norm_mix[i], at_w_qkv[s], at_q_gain[s], at_k_gain[s], at_sink[s],
                            at_w_o[s], rel_table)
        else:
            x = _pool_layer(x, norm_mix[i], pl_w[s], pl_b[s], pl_scale[s])
        x2 = _ffn(x.reshape(bsz * seq, d), norm_ffn[i][None], ff_w_gate[i], ff_w_up[i], ff_w_down[i])
        x = x2.reshape(bsz, seq, d)
    return x
```

```python
import functools
import math

import jax
import jax.numpy as jnp
from jax import lax
from jax.experimental import pallas as pl
from jax.experimental.pallas import tpu as pltpu

F32 = jnp.float32
BF16 = jnp.bfloat16

D_MODEL = 1024
BATCH = 4
SEQ = 8192
DEPTH = 4
EPS = 1e-6

HY_SHORT = 3
HY_EMB = 33
HY_BANDS = 16
HY_FW = 64
HY_INNER = 2
HY_MIN_DECAY = math.log(1e-2) / 1.5
HY_MAX_DECAY = math.log(1e-2) / 0.3

HEAD_DIM = 64
N_HEADS = 16
N_KV = 4
GROUP = 4
WINDOW = 128
BLOCK = 128
REL_BUCKETS = 32
REL_MAX_DIST = 128

POOL_WINDOWS = (2, 4, 8, 16)
POOL_GROUP = 256
D_FF = 2816

SUBLANES = 8
LANES = 128
VMEM_LIMIT = 56 * 1024 * 1024

FFT_N = 2 * SEQ
FFT_N1 = 256
FFT_N2 = 64
FFT_K1 = FFT_N1 // 2
SLAB = 2 * FFT_K1 + SUBLANES
CONV_DT = 128
DFT_UNROLL = 32


def _cparams(sem):
    return pltpu.CompilerParams(dimension_semantics=sem, vmem_limit_bytes=VMEM_LIMIT)


def _rms(x, g):
    return x * lax.rsqrt(jnp.mean(x * x, axis=-1, keepdims=True) + EPS) * g


FFN_CHUNK = 256


FFN_CHUNKS = D_FF // FFN_CHUNK


def _ffn_kernel(x_ref, g_ref, wg_ref, wu_ref, wd_ref, o_ref, wgb_ref, wub_ref, wdb_ref, h_ref):
    s = pl.program_id(0)

    @pl.when(s < FFN_CHUNKS)
    def _():
        wgb_ref[s] = wg_ref[...].astype(BF16)
        wub_ref[s] = wu_ref[...].astype(BF16)
        wdb_ref[pl.ds(pl.multiple_of(s * FFN_CHUNK, FFN_CHUNK), FFN_CHUNK), :] = wd_ref[...].astype(BF16)

    @pl.when(s >= FFN_CHUNKS)
    def _():
        x = x_ref[...]
        xn = _rms(x, g_ref[...]).astype(BF16)
        for c in range(FFN_CHUNKS):
            gate = jnp.dot(xn, wgb_ref[c], preferred_element_type=F32)
            up = jnp.dot(xn, wub_ref[c], preferred_element_type=F32)
            h_ref[:, c * FFN_CHUNK:(c + 1) * FFN_CHUNK] = (gate * jax.nn.sigmoid(gate) * up).astype(BF16)
        o_ref[...] = x + jnp.dot(h_ref[...], wdb_ref[...], preferred_element_type=F32)


def _ffn(x2, g, wg, wu, wd, layer, tm=512):
    m, d = x2.shape
    f = wg.shape[2]
    tile = lambda s: jnp.maximum(s - FFN_CHUNKS, 0)
    chunk = lambda s: jnp.minimum(s, FFN_CHUNKS - 1)
    return pl.pallas_call(
        _ffn_kernel,
        grid=(FFN_CHUNKS + m // tm,),
        in_specs=[
            pl.BlockSpec((tm, d), lambda s: (tile(s), 0)),
            pl.BlockSpec((1, d), lambda s: (0, 0)),
            pl.BlockSpec((None, d, FFN_CHUNK), lambda s: (layer, 0, chunk(s))),
            pl.BlockSpec((None, d, FFN_CHUNK), lambda s: (layer, 0, chunk(s))),
            pl.BlockSpec((None, FFN_CHUNK, d), lambda s: (layer, chunk(s), 0)),
        ],
        out_specs=pl.BlockSpec((tm, d), lambda s: (tile(s), 0)),
        out_shape=jax.ShapeDtypeStruct((m, d), F32),
        scratch_shapes=[pltpu.VMEM((FFN_CHUNKS, d, FFN_CHUNK), BF16),
                        pltpu.VMEM((FFN_CHUNKS, d, FFN_CHUNK), BF16),
                        pltpu.VMEM((f, d), BF16),
                        pltpu.VMEM((tm, f), BF16)],
        compiler_params=_cparams(("arbitrary",)),
        name="ffn",
    )(x2, g, wg, wu, wd)


INPROJ_N1 = 16
INPROJ_CHUNK = 512


def _slab_perm():
    nb = INPROJ_N1
    r = jnp.arange(FFT_N2 * nb)
    src = FFT_N2 * (r % nb) + r // nb
    return (src[:, None] == r[None, :]).astype(BF16)


def _inproj_kernel(xp_ref, xc_ref, xn_ref, g_ref, pm_ref, w_ref, b_ref, cw_ref, cb_ref, o_ref, xs_ref):
    i = pl.program_id(1)
    nb = INPROJ_N1
    tm = FFT_N2 * nb
    g = g_ref[...]
    xn = _rms(xc_ref[...], g).astype(BF16)
    xs_ref[:tm, :] = jnp.dot(pm_ref[...], xn, preferred_element_type=F32).astype(BF16)
    halo = jnp.concatenate([xp_ref[...], xn_ref[...]], axis=0)
    xs_ref[tm:, :] = _rms(halo, g).astype(BF16)

    xs = xs_ref[...]
    row = lax.broadcasted_iota(jnp.int32, (nb, 1), 0)
    inside_before = i > 0
    inside_after = i < pl.num_programs(1) - 1
    for c in range(w_ref.shape[1] // INPROJ_CHUNK):
        cols = slice(c * INPROJ_CHUNK, (c + 1) * INPROJ_CHUNK)
        p = jnp.dot(xs, w_ref[:, cols], preferred_element_type=F32) + b_ref[:, cols]
        cur = p[:tm]
        before = jnp.where(inside_before, p[tm + SUBLANES - 1:tm + SUBLANES], 0.0)
        after = jnp.where(inside_after, p[tm + SUBLANES:tm + SUBLANES + 1], 0.0)
        first = jnp.where(row == 0, before, pltpu.roll(cur[tm - nb:], 1, 0))
        last = jnp.where(row == nb - 1, after, pltpu.roll(cur[:nb], nb - 1, 0))
        dn = jnp.concatenate([first, cur[:tm - nb]], axis=0)
        up = jnp.concatenate([cur[nb:], last], axis=0)
        cw = cw_ref[:, cols]
        out = cb_ref[:, cols] + cw[0:1] * dn + cw[1:2] * cur + cw[2:3] * up
        o_ref[:, :, cols] = out.reshape(FFT_N2, nb, INPROJ_CHUNK).astype(o_ref.dtype)


def _inproj_conv(x, g, w, b, cw, cb):
    bsz, seq, d = x.shape
    n = w.shape[1]
    nb = INPROJ_N1
    tm = FFT_N2 * nb
    hb = tm // SUBLANES
    last = seq // SUBLANES - 1
    const = lambda bi, i: (0, 0)
    return pl.pallas_call(
        _inproj_kernel,
        grid=(bsz, seq // tm),
        in_specs=[
            pl.BlockSpec((None, SUBLANES, d), lambda bi, i: (bi, jnp.maximum(i * hb - 1, 0), 0)),
            pl.BlockSpec((None, tm, d), lambda bi, i: (bi, i, 0)),
            pl.BlockSpec((None, SUBLANES, d), lambda bi, i: (bi, jnp.minimum((i + 1) * hb, last), 0)),
            pl.BlockSpec((1, d), const),
            pl.BlockSpec((tm, tm), const, pipeline_mode=pl.Buffered(1)),
            pl.BlockSpec((d, n), const, pipeline_mode=pl.Buffered(1)),
            pl.BlockSpec((1, n), const),
            pl.BlockSpec((HY_SHORT, n), const),
            pl.BlockSpec((1, n), const),
        ],
        out_specs=pl.BlockSpec((None, FFT_N2, nb, n), lambda bi, i: (bi, 0, i, 0)),
        out_shape=jax.ShapeDtypeStruct((bsz, FFT_N2, seq // FFT_N2, n), BF16),
        scratch_shapes=[pltpu.VMEM((tm + 2 * SUBLANES, d), BF16)],
        compiler_params=_cparams(("parallel", "parallel")),
        name="hyena_inproj",
    )(x, x, x, g, _slab_perm(), w, b, cw, cb)


def _dft_mats():
    i32 = jnp.int32
    k1 = jnp.arange(FFT_K1, dtype=i32)[:, None]
    n1 = jnp.arange(FFT_N1, dtype=i32)[None, :]
    m = ((2 * k1 + 1) * n1) % (2 * FFT_N1)
    ang = m.astype(F32) * (math.pi / FFT_N1)
    f1 = jnp.concatenate([jnp.cos(ang), -jnp.sin(ang)], axis=0)
    k1 = jnp.arange(FFT_K1, dtype=i32)[:, None, None]
    k2 = jnp.arange(FFT_N2, dtype=i32)[None, :, None]
    n2 = jnp.arange(FFT_N2, dtype=i32)[None, None, :]
    m = ((2 * (k1 + FFT_N1 * k2) + 1) * n2) % (2 * FFT_N)
    ang = m.astype(F32) * (math.pi / FFT_N)
    c, s = jnp.cos(ang), jnp.sin(ang)
    gm = jnp.concatenate(
        [jnp.concatenate([c, s], axis=2), jnp.concatenate([-s, c], axis=2)], axis=1)
    return f1, gm


def _hdot(a, b):
    return jnp.dot(a, b, preferred_element_type=F32, precision=lax.Precision.HIGHEST)


def _tap_lag(n):
    return jnp.where(n >= SEQ, FFT_N - n, n).astype(F32)


def _split_bf16(a):
    hi = a.astype(BF16)
    return hi, (a - hi.astype(F32)).astype(BF16)


def _filter_mlp_kernel(fq_ref, w1t_ref, w1c_ref, w1s_ref, b1_ref, wi_ref, bi_ref, fr_ref,
                       h3_ref, *, slabs):
    cols = slabs * FFT_N1
    c = lax.broadcasted_iota(jnp.int32, (1, cols), 1)
    n2 = pl.program_id(0) * slabs + c // FFT_N1
    pos = _tap_lag(FFT_N2 * (c % FFT_N1) + n2)
    t = pos * (1.0 / (SEQ - 1))
    ang = (2.0 * math.pi / SEQ) * pos * fq_ref[...]
    fr = fr_ref[...]
    pre = (w1t_ref[...] * t + _hdot(w1c_ref[...], jnp.cos(ang))
           + _hdot(w1s_ref[...], -jnp.sin(ang)))
    h = jnp.sin(fr * (pre + b1_ref[...]))
    for l in range(HY_INNER):
        h = jnp.sin(fr * (_hdot(wi_ref[l], h) + bi_ref[l]))
    hi, lo = _split_bf16(h)
    h3 = jnp.concatenate([hi, lo, hi], axis=0)
    for k in range(slabs):
        h3_ref[k] = h3[:, k * FFT_N1:(k + 1) * FFT_N1]


def _filter_mlp(fq, w1, b1, wi, bi, fr, slabs=8):
    w1t, w1c, w1s = w1[0:1].T, w1[1:1 + HY_BANDS].T, w1[1 + HY_BANDS:].T
    col = lambda a: a.reshape(-1, 1)
    const2 = lambda i: (0, 0)
    const3 = lambda i: (0, 0, 0)
    return pl.pallas_call(
        functools.partial(_filter_mlp_kernel, slabs=slabs),
        grid=(FFT_N2 // slabs,),
        in_specs=[
            pl.BlockSpec((HY_BANDS, 1), const2),
            pl.BlockSpec((HY_FW, 1), const2),
            pl.BlockSpec((HY_FW, HY_BANDS), const2),
            pl.BlockSpec((HY_FW, HY_BANDS), const2),
            pl.BlockSpec((HY_FW, 1), const2),
            pl.BlockSpec((HY_INNER, HY_FW, HY_FW), const3),
            pl.BlockSpec((HY_INNER, HY_FW, 1), const3),
            pl.BlockSpec((HY_FW, 1), const2),
        ],
        out_specs=pl.BlockSpec((slabs, 3 * HY_FW, FFT_N1), lambda i: (i, 0, 0)),
        out_shape=jax.ShapeDtypeStruct((FFT_N2, 3 * HY_FW, FFT_N1), BF16),
        compiler_params=_cparams(("parallel",)),
        name="hyena_filter_mlp",
    )(col(fq), w1t, w1c, w1s, col(b1), jnp.swapaxes(wi, 1, 2), bi[:, :, None], col(fr))


def _slab_store(s_ref, n2, a):
    rows = pl.ds(pl.multiple_of(n2 * SLAB, SUBLANES), 2 * FFT_K1)
    for l in range(s_ref.shape[0]):
        s_ref[l, rows, :] = a[:, l * LANES:(l + 1) * LANES]


def _slab_load(s_ref, n2):
    rows = pl.ds(pl.multiple_of(n2 * SLAB, SUBLANES), 2 * FFT_K1)
    return jnp.concatenate([s_ref[l, rows, :] for l in range(s_ref.shape[0])], axis=1).astype(BF16)


def _k1_load(s_ref, k1):
    re = pl.ds(k1, FFT_N2, stride=SLAB)
    im = pl.ds(FFT_K1 + k1, FFT_N2, stride=SLAB)
    tiles = [jnp.concatenate([s_ref[l, re, :], s_ref[l, im, :]], axis=0) for l in range(s_ref.shape[0])]
    return jnp.concatenate(tiles, axis=1).astype(BF16)


def _k1_store(s_ref, k1, b):
    for l in range(s_ref.shape[0]):
        cols = slice(l * LANES, (l + 1) * LANES)
        s_ref[l, pl.ds(k1, FFT_N2, stride=SLAB), :] = b[:FFT_N2, cols]
        s_ref[l, pl.ds(FFT_K1 + k1, FFT_N2, stride=SLAB), :] = b[FFT_N2:, cols]


def _stage_scratch(dt):
    return pltpu.VMEM((dt // LANES, FFT_N2 * SLAB, LANES), F32)


def _stage1(src_ref, f1_ref, s_ref):
    def body(n2, c):
        _slab_store(s_ref, n2, jnp.dot(f1_ref[...], src_ref[n2].astype(BF16), preferred_element_type=F32))
        return c
    lax.fori_loop(0, FFT_N2, body, 0, unroll=DFT_UNROLL)


def _filt_dft_kernel(h3_ref, w3_ref, dl_ref, f1_ref, gm_ref, kf_ref, s_ref):
    half = FFT_N1 // 2
    dt = dl_ref.shape[1]
    n1 = lax.broadcasted_iota(jnp.int32, (FFT_N1, 1), 0)
    dl = dl_ref[...]

    def body(n2, l1):
        both = lax.dot_general(h3_ref[n2], w3_ref[...], (((0,), (0,)), ((), ())),
                               preferred_element_type=F32)
        fwd = both[:half, :dt]
        bwd = both[half:, dt:]
        n = FFT_N2 * n1 + n2
        t = _tap_lag(n) * (1.0 / (SEQ - 1))
        taps = jnp.concatenate([fwd, -bwd], axis=0) * jnp.exp(-t * dl)
        taps = jnp.where(n == SEQ, 0.0, taps)
        _slab_store(s_ref, n2, jnp.dot(f1_ref[...], taps.astype(BF16), preferred_element_type=F32))
        return l1 + jnp.sum(jnp.abs(taps), axis=0, keepdims=True)
    l1 = lax.fori_loop(0, FFT_N2, body, jnp.zeros(dl.shape, F32), unroll=DFT_UNROLL)
    inv = 1.0 / l1

    def stage2(k1, c):
        x = jnp.dot(gm_ref[k1], _k1_load(s_ref, k1), preferred_element_type=F32)
        rows = pl.ds(pl.multiple_of(k1 * 2 * FFT_N2, 2 * FFT_N2), 2 * FFT_N2)
        kf_ref[rows, :] = (x * inv).astype(kf_ref.dtype)
        return c
    lax.fori_loop(0, FFT_K1, stage2, 0, unroll=DFT_UNROLL)


def _filter_spectrum(h3, w3, dl2, f1_full, gm):
    ncol = 2 * D_MODEL
    dt = CONV_DT
    nct = ncol // dt
    tiles = lambda a: a.reshape(HY_FW, 2, nct, dt).transpose(2, 0, 1, 3).reshape(nct, HY_FW, 2 * dt)
    w3h, w3l = _split_bf16(w3)
    w3cat = jnp.concatenate([tiles(w3h), tiles(w3h), tiles(w3l)], axis=1)
    return pl.pallas_call(
        _filt_dft_kernel,
        grid=(nct,),
        in_specs=[
            pl.BlockSpec((FFT_N2, 3 * HY_FW, FFT_N1), lambda j: (0, 0, 0)),
            pl.BlockSpec((None, 3 * HY_FW, 2 * dt), lambda j: (j, 0, 0)),
            pl.BlockSpec((1, dt), lambda j: (0, j)),
            pl.BlockSpec((2 * FFT_K1, FFT_N1), lambda j: (0, 0)),
            pl.BlockSpec((FFT_K1, 2 * FFT_N2, 2 * FFT_N2), lambda j: (0, 0, 0),
                         pipeline_mode=pl.Buffered(1)),
        ],
        out_specs=pl.BlockSpec((FFT_K1 * 2 * FFT_N2, dt), lambda j: (0, j)),
        out_shape=jax.ShapeDtypeStruct((FFT_K1 * 2 * FFT_N2, ncol), F32),
        scratch_shapes=[_stage_scratch(dt)],
        compiler_params=_cparams(("parallel",)),
        name="hyena_filter_spectrum",
    )(h3, w3cat, dl2, f1_full, gm)


def _longconv_kernel(z_ref, g_ref, fb_ref, kf_ref, f1_ref, gm_ref, gh_ref, fi_ref, o_ref, s_ref):
    _stage1(z_ref, f1_ref, s_ref)

    def mid(k1, c):
        x = jnp.dot(gm_ref[k1], _k1_load(s_ref, k1), preferred_element_type=F32)
        kf = kf_ref[pl.ds(pl.multiple_of(k1 * 2 * FFT_N2, 2 * FFT_N2), 2 * FFT_N2), :].astype(F32)
        xr, xi = x[:FFT_N2], x[FFT_N2:]
        kr, ki = kf[:FFT_N2], kf[FFT_N2:]
        y = jnp.concatenate([xr * kr - xi * ki, xr * ki + xi * kr], axis=0).astype(BF16)
        b = jnp.dot(gh_ref[k1], y, preferred_element_type=F32)
        _k1_store(s_ref, k1, b)
        return c
    lax.fori_loop(0, FFT_K1, mid, 0, unroll=2 * DFT_UNROLL)

    fb = fb_ref[...]

    def last(n2, c):
        y = jnp.dot(fi_ref[...], _slab_load(s_ref, n2), preferred_element_type=F32)
        o_ref[n2] = (g_ref[n2] * (y + fb * z_ref[n2])).astype(o_ref.dtype)
        return c
    lax.fori_loop(0, FFT_N2, last, 0, unroll=DFT_UNROLL)


def _long_conv_gate(z, zcol, g, gcol, fb, kf, kcol, f1_half, gm, gh, fi):
    bsz = z.shape[0]
    dt = CONV_DT
    nt = D_MODEL // dt
    n1 = FFT_N1 // 2
    const3 = lambda j, bi: (0, 0, 0)
    return pl.pallas_call(
        _longconv_kernel,
        grid=(nt, bsz),
        in_specs=[
            pl.BlockSpec((None, FFT_N2, n1, dt), lambda j, bi: (bi, 0, 0, zcol + j)),
            pl.BlockSpec((None, FFT_N2, n1, dt), lambda j, bi: (bi, 0, 0, gcol + j)),
            pl.BlockSpec((1, dt), lambda j, bi: (0, j)),
            pl.BlockSpec((FFT_K1 * 2 * FFT_N2, dt), lambda j, bi: (0, kcol + j),
                         pipeline_mode=pl.Buffered(1)),
            pl.BlockSpec((2 * FFT_K1, FFT_N1 // 2), lambda j, bi: (0, 0)),
            pl.BlockSpec((FFT_K1, 2 * FFT_N2, 2 * FFT_N2), const3, pipeline_mode=pl.Buffered(1)),
            pl.BlockSpec((FFT_K1, 2 * FFT_N2, 2 * FFT_N2), const3, pipeline_mode=pl.Buffered(1)),
            pl.BlockSpec((FFT_N1 // 2, 2 * FFT_K1), lambda j, bi: (0, 0)),
        ],
        out_specs=pl.BlockSpec((None, FFT_N2, n1, dt), lambda j, bi: (bi, 0, 0, j)),
        out_shape=jax.ShapeDtypeStruct((bsz, FFT_N2, n1, D_MODEL), BF16),
        scratch_shapes=[_stage_scratch(dt)],
        compiler_params=_cparams(("parallel", "arbitrary")),
        name="hyena_long_conv",
    )(z, g, fb, kf, f1_half, gm, gh, fi)


def _hyena_out_kernel(z_ref, pm_ref, w_ref, b_ref, x_ref, o_ref):
    z = z_ref[...].reshape(x_ref.shape)
    a = jnp.dot(pm_ref[...], z, preferred_element_type=F32).astype(BF16)
    acc = jnp.dot(a, w_ref[...], preferred_element_type=F32)
    o_ref[...] = x_ref[...] + acc + b_ref[...]


def _hyena_out_proj(z, w, b, x):
    bsz, seq, d = x.shape
    nb = INPROJ_N1
    tm = FFT_N2 * nb
    return pl.pallas_call(
        _hyena_out_kernel,
        grid=(bsz, seq // tm),
        in_specs=[
            pl.BlockSpec((None, FFT_N2, nb, d), lambda bi, i: (bi, 0, i, 0)),
            pl.BlockSpec((tm, tm), lambda bi, i: (0, 0)),
            pl.BlockSpec((d, d), lambda bi, i: (0, 0)),
            pl.BlockSpec((1, d), lambda bi, i: (0, 0)),
            pl.BlockSpec((None, tm, d), lambda bi, i: (bi, i, 0)),
        ],
        out_specs=pl.BlockSpec((None, tm, d), lambda bi, i: (bi, i, 0)),
        out_shape=jax.ShapeDtypeStruct((bsz, seq, d), F32),
        compiler_params=_cparams(("parallel", "parallel")),
        name="hyena_out_proj",
    )(z, _slab_perm().T, w, b, x)


def _hyena_layer(x, g_mix, w_in, b_in, cw, cb, fw1, fb1, fwi, fbi, fw3, freq, fbias, w_out, b_out, consts):
    f1_full, f1_half, gm, gh, fi, fq, dl2 = consts
    bsz, seq, d = x.shape
    pc = _inproj_conv(x, g_mix[None], w_in.astype(BF16), b_in[None], cw, cb[None])
    h3 = _filter_mlp(fq, fw1, fb1, fwi, fbi, freq)
    kf = _filter_spectrum(h3, fw3, dl2, f1_full, gm)
    nt = D_MODEL // CONV_DT
    z1 = _long_conv_gate(pc, 2 * nt, pc, 0, fbias[0:1], kf, 0, f1_half, gm, gh, fi)
    z2 = _long_conv_gate(z1, 0, pc, nt, fbias[1:2], kf, nt, f1_half, gm, gh, fi)
    return _hyena_out_proj(z2, w_out.astype(BF16), b_out[None], x)


def _relbias_kernel(tab_ref, bucket_ref, band_ref, o_ref):
    bucket = bucket_ref[...]
    band = band_ref[...]
    for h in range(N_HEADS):
        acc = jnp.zeros(bucket.shape, F32)
        for k in range(REL_BUCKETS):
            acc = jnp.where(bucket == k, tab_ref[k, h], acc)
        o_ref[h] = jnp.where(band > 0, acc, -1e30)


def _rel_bias(rel_table):
    a = jnp.arange(BLOCK)[:, None]
    j = jnp.arange(3 * BLOCK)[None, :]
    rel = j - BLOCK - a
    half = REL_BUCKETS // 2
    exact = half // 2
    n = jnp.abs(rel)
    nf = jnp.maximum(n, 1).astype(F32)
    large = exact + (jnp.log(nf / exact) / math.log(REL_MAX_DIST / exact) * (half - exact)).astype(jnp.int32)
    large = jnp.minimum(large, half - 1)
    bucket = (jnp.where(rel > 0, half, 0) + jnp.where(n < exact, n, large)).astype(jnp.int32)
    band = (n <= WINDOW).astype(jnp.int32)
    return pl.pallas_call(
        _relbias_kernel,
        in_specs=[
            pl.BlockSpec(memory_space=pltpu.SMEM),
            pl.BlockSpec((BLOCK, 3 * BLOCK), lambda: (0, 0)),
            pl.BlockSpec((BLOCK, 3 * BLOCK), lambda: (0, 0)),
        ],
        out_specs=pl.BlockSpec((N_HEADS, BLOCK, 3 * BLOCK), lambda: (0, 0, 0)),
        out_shape=jax.ShapeDtypeStruct((N_HEADS, BLOCK, 3 * BLOCK), F32),
        name="rel_bias",
    )(rel_table, bucket, band)


def _qkv_kernel(x_ref, g_ref, w_ref, gain_ref, flag_ref, bd_ref, o_ref, xn_ref):
    @pl.when(pl.program_id(1) == 0)
    def _():
        xn_ref[...] = _rms(x_ref[...], g_ref[...]).astype(BF16)

    acc = jnp.dot(xn_ref[...], w_ref[...], preferred_element_type=F32)
    ssq = jnp.dot((acc * acc).astype(BF16), bd_ref[...], preferred_element_type=F32)
    normed = acc * lax.rsqrt(ssq * (1.0 / HEAD_DIM) + EPS) * gain_ref[...]
    o_ref[...] = jnp.where(flag_ref[...] > 0, normed, acc).astype(o_ref.dtype)


def _qkv_proj(x2, g, w, gain, flag, tm=1024, tn=512):
    m, k = x2.shape
    n = w.shape[1]
    eye = jnp.arange(tn)[:, None] // HEAD_DIM == jnp.arange(tn)[None, :] // HEAD_DIM
    return pl.pallas_call(
        _qkv_kernel,
        grid=(m // tm, n // tn),
        in_specs=[
            pl.BlockSpec((tm, k), lambda i, j: (i, 0)),
            pl.BlockSpec((1, k), lambda i, j: (0, 0)),
            pl.BlockSpec((k, tn), lambda i, j: (0, j)),
            pl.BlockSpec((1, tn), lambda i, j: (0, j)),
            pl.BlockSpec((1, tn), lambda i, j: (0, j)),
            pl.BlockSpec((tn, tn), lambda i, j: (0, 0)),
        ],
        out_specs=pl.BlockSpec((tm, tn), lambda i, j: (i, j)),
        out_shape=jax.ShapeDtypeStruct((m, n), BF16),
        scratch_shapes=[pltpu.VMEM((tm, k), BF16)],
        compiler_params=_cparams(("parallel", "arbitrary")),
        name="qkv_proj",
    )(x2, g, w, gain, flag, eye.astype(BF16))


def _attn_kernel(q_ref, kp_ref, kc_ref, kn_ref, bias_ref, sink_ref, o_ref):
    kv = jnp.concatenate([kp_ref[...], kc_ref[...], kn_ref[...]], axis=0)
    lane = lax.broadcasted_iota(jnp.int32, (1, LANES), 1)
    keep = [(lane < HEAD_DIM).astype(F32).astype(BF16), (lane >= HEAD_DIM).astype(F32).astype(BF16)]
    ntile = N_KV // 2
    logits = []
    for kh in range(N_KV):
        j, half = kh // 2, kh % 2
        q4 = jnp.concatenate(
            [q_ref[:, (GROUP * j + g) * LANES:(GROUP * j + g + 1) * LANES] for g in range(GROUP)], axis=0)
        kt = kv[:, j * LANES:(j + 1) * LANES]
        logits.append(lax.dot_general(kt * keep[half], q4, (((1,), (1,)), ((), ())),
                                      preferred_element_type=F32))
    outs = []
    for kh in range(N_KV):
        j, half = kh // 2, kh % 2
        vt = kv[:, (ntile + j) * LANES:(ntile + j + 1) * LANES]
        lg = logits[kh] + bias_ref[kh]
        sink = sink_ref[kh]
        m = jnp.maximum(jnp.max(lg, axis=0, keepdims=True), sink)
        p = jnp.exp(lg - m)
        den = jnp.sum(p, axis=0, keepdims=True) + jnp.exp(sink - m)
        outs.append(lax.dot_general(vt * keep[half], p.astype(BF16), (((0,), (0,)), ((), ())),
                                    preferred_element_type=F32) * (1.0 / den))
    for j in range(ntile):
        acc = outs[2 * j] + outs[2 * j + 1]
        for g in range(GROUP):
            o_ref[(GROUP * j + g) * LANES:(GROUP * j + g + 1) * LANES, :] = (
                acc[:, g * BLOCK:(g + 1) * BLOCK].astype(o_ref.dtype))


def _attention(qkv, bias_t, sink_row):
    bsz, seq, _ = qkv.shape
    nb = seq // BLOCK
    dq = N_HEADS * HEAD_DIM
    dkv = 2 * N_KV * HEAD_DIM
    kvb = dq // dkv
    return pl.pallas_call(
        _attn_kernel,
        grid=(bsz, nb),
        in_specs=[
            pl.BlockSpec((None, BLOCK, dq), lambda b, n: (b, n, 0)),
            pl.BlockSpec((None, BLOCK, dkv), lambda b, n: (b, jnp.maximum(n - 1, 0), kvb)),
            pl.BlockSpec((None, BLOCK, dkv), lambda b, n: (b, n, kvb)),
            pl.BlockSpec((None, BLOCK, dkv), lambda b, n: (b, jnp.minimum(n + 1, nb - 1), kvb)),
            pl.BlockSpec((None, N_KV, 3 * BLOCK, GROUP * BLOCK),
                         lambda b, n: (jnp.where(n == 0, 0, jnp.where(n == nb - 1, 2, 1)), 0, 0, 0)),
            pl.BlockSpec((N_KV, 1, GROUP * BLOCK), lambda b, n: (0, 0, 0)),
        ],
        out_specs=pl.BlockSpec((None, None, dq, BLOCK), lambda b, n: (b, n, 0, 0)),
        out_shape=jax.ShapeDtypeStruct((bsz, nb, dq, BLOCK), BF16),
        compiler_params=_cparams(("parallel", "parallel")),
        name="window_attention",
    )(qkv, qkv, qkv, qkv, bias_t, sink_row)


ATTN_OUT_BLOCKS = 8


def _attn_out_kernel(ot_ref, w_ref, x_ref, o_ref):
    for i in range(ATTN_OUT_BLOCKS):
        rows = slice(i * BLOCK, (i + 1) * BLOCK)
        y = lax.dot_general(ot_ref[i], w_ref[...], (((0,), (0,)), ((), ())),
                            preferred_element_type=F32)
        o_ref[rows, :] = x_ref[rows, :] + y


def _attn_out_proj(o_t, w, x2):
    nblk, dq, _ = o_t.shape
    m, d = x2.shape
    tm = ATTN_OUT_BLOCKS * BLOCK
    return pl.pallas_call(
        _attn_out_kernel,
        grid=(nblk // ATTN_OUT_BLOCKS,),
        in_specs=[
            pl.BlockSpec((ATTN_OUT_BLOCKS, dq, BLOCK), lambda i: (i, 0, 0)),
            pl.BlockSpec((dq, d), lambda i: (0, 0)),
            pl.BlockSpec((tm, d), lambda i: (i, 0)),
        ],
        out_specs=pl.BlockSpec((tm, d), lambda i: (i, 0)),
        out_shape=jax.ShapeDtypeStruct((m, d), F32),
        compiler_params=_cparams(("parallel",)),
        name="attn_out_proj",
    )(o_t, w, x2)


def _head_tile_perm():
    cols = []
    for tile in range(N_HEADS // 2):
        j, g = tile // GROUP, tile % GROUP
        for half in range(2):
            h = (2 * j + half) * GROUP + g
            cols.extend(range(h * HEAD_DIM, (h + 1) * HEAD_DIM))
    return jnp.asarray(cols, jnp.int32)


def _attn_layer(x, g_mix, w_qkv, q_gain, k_gain, sink, w_o, rel_table):
    bsz, seq, d = x.shape
    x2 = x.reshape(bsz * seq, d)
    dq = N_HEADS * HEAD_DIM
    dk = N_KV * HEAD_DIM
    perm = _head_tile_perm()
    w = jnp.concatenate([w_qkv[:, :dq][:, perm], w_qkv[:, dq:]], axis=1).astype(BF16)
    gain = jnp.concatenate([jnp.tile(q_gain * (HEAD_DIM ** -0.5), N_HEADS), jnp.tile(k_gain, N_KV),
                            jnp.ones((dk,), F32)])[None]
    flag = jnp.concatenate([jnp.ones((dq + dk,), F32), jnp.zeros((dk,), F32)])[None]
    qkv = _qkv_proj(x2, g_mix[None], w, gain, flag)
    bias_t = _rel_bias(rel_table).reshape(N_KV, GROUP * BLOCK, 3 * BLOCK).transpose(0, 2, 1)
    key = jnp.arange(3 * BLOCK)[None, :, None]
    bias_t = jnp.stack([jnp.where(key < BLOCK, -1e30, bias_t), bias_t,
                        jnp.where(key >= 2 * BLOCK, -1e30, bias_t)])
    sink_row = jnp.repeat(sink, BLOCK).reshape(N_KV, 1, GROUP * BLOCK)
    o_t = _attention(qkv.reshape(bsz, seq, dq + 2 * dk), bias_t, sink_row)
    y = _attn_out_proj(o_t.reshape(bsz * (seq // BLOCK), dq, BLOCK), w_o[perm, :].astype(BF16), x2)
    return y.reshape(bsz, seq, d)


def _pool_kernel(xp_ref, xc_ref, xn_ref, g_ref, w_ref, b_ref, sc_ref, o_ref, *, tm):
    i = pl.program_id(1)
    g = g_ref[...]
    rows = tm + 2 * SUBLANES
    t = i * tm - SUBLANES + lax.broadcasted_iota(jnp.int32, (rows, 1), 0)
    inside = (t >= 0) & (t < SEQ)
    xa = jnp.concatenate([_rms(xp_ref[...], g), _rms(xc_ref[...], g), _rms(xn_ref[...], g)], axis=0)
    xa = jnp.where(inside, xa, 0.0)
    tc = t[SUBLANES:SUBLANES + tm]
    outs = []
    for gi, wdt in enumerate(POOL_WINDOWS):
        r = wdt // 2
        xg = xa[:, gi * POOL_GROUP:(gi + 1) * POOL_GROUP]
        run = xg
        span = 1
        while span < 2 * r:
            run = run + pltpu.roll(run, rows - span, 0)
            span *= 2
        win = (pltpu.roll(run, r, 0) + pltpu.roll(xg, rows - r, 0))[SUBLANES:SUBLANES + tm]
        cnt = (jnp.minimum(tc + r + 1, SEQ) - jnp.maximum(tc - r, 0)).astype(F32)
        dlt = win / cnt - xg[SUBLANES:SUBLANES + tm]
        outs.append(jnp.dot(dlt.astype(BF16), w_ref[gi], preferred_element_type=F32))
    y = (jnp.concatenate(outs, axis=-1) + b_ref[...]) * sc_ref[...]
    o_ref[...] = xc_ref[...] + y


def _pool_layer(x, g_mix, w_grp, b, scale, tm=512):
    bsz, seq, d = x.shape
    hb = tm // SUBLANES
    last = seq // SUBLANES - 1
    ng = len(POOL_WINDOWS)
    return pl.pallas_call(
        functools.partial(_pool_kernel, tm=tm),
        grid=(bsz, seq // tm),
        in_specs=[
            pl.BlockSpec((None, SUBLANES, d), lambda bi, i: (bi, jnp.maximum(i * hb - 1, 0), 0)),
            pl.BlockSpec((None, tm, d), lambda bi, i: (bi, i, 0)),
            pl.BlockSpec((None, SUBLANES, d), lambda bi, i: (bi, jnp.minimum((i + 1) * hb, last), 0)),
            pl.BlockSpec((1, d), lambda bi, i: (0, 0)),
            pl.BlockSpec((ng, POOL_GROUP, POOL_GROUP), lambda bi, i: (0, 0, 0)),
            pl.BlockSpec((1, d), lambda bi, i: (0, 0)),
            pl.BlockSpec((1, d), lambda bi, i: (0, 0)),
        ],
        out_specs=pl.BlockSpec((None, tm, d), lambda bi, i: (bi, i, 0)),
        out_shape=jax.ShapeDtypeStruct((bsz, seq, d), F32),
        compiler_params=_cparams(("parallel", "parallel")),
        name="pool_mixer",
    )(x, x, x, g_mix[None], w_grp.astype(BF16), b[None], scale[None])


def kernel(x, norm_mix, norm_ffn, hy_w_in, hy_b_in, hy_conv_w, hy_conv_b, hy_f_w1, hy_f_b1, hy_f_wi, hy_f_bi, hy_f_w3, hy_f_freq, hy_f_bias, hy_w_out, hy_b_out, at_w_qkv, at_q_gain, at_k_gain, at_sink, at_w_o, rel_table, pl_w, pl_b, pl_scale, ff_w_gate, ff_w_up, ff_w_down):
    bsz, seq, d = x.shape
    f1, gm = _dft_mats()
    f1_full = f1.astype(BF16)
    f1_half = f1[:, :FFT_N1 // 2].astype(BF16)
    fi = (f1[:, :FFT_N1 // 2].T * (2.0 / FFT_N)).astype(BF16)
    gh = jnp.swapaxes(gm, 1, 2).astype(BF16)
    gm = gm.astype(BF16)
    fq = jnp.linspace(1e-4, HY_BANDS - 1, HY_BANDS, dtype=F32)[None, :]
    deltas = jnp.abs(jnp.linspace(HY_MIN_DECAY, HY_MAX_DECAY, D_MODEL, dtype=F32))
    dl2 = jnp.concatenate([deltas, deltas])[None, :]
    consts = (f1_full, f1_half, gm, gh, fi, fq, dl2)

    for i in range(DEPTH):
        kind, s = i % 3, i // 3
        if kind == 0:
            x = _hyena_layer(x, norm_mix[i], hy_w_in[s], hy_b_in[s], hy_conv_w[s], hy_conv_b[s],
                             hy_f_w1[s], hy_f_b1[s], hy_f_wi[s], hy_f_bi[s], hy_f_w3[s],
                             hy_f_freq[s], hy_f_bias[s], hy_w_out[s], hy_b_out[s], consts)
        elif kind == 1:
            x = _attn_layer(x, norm_mix[i], at_w_qkv[s], at_q_gain[s], at_k_gain[s], at_sink[s],
                            at_w_o[s], rel_table)
        else:
            x = _pool_layer(x, norm_mix[i], pl_w[s], pl_b[s], pl_scale[s])
        x2 = _ffn(x.reshape(bsz * seq, d), norm_ffn[i][None], ff_w_gate, ff_w_up, ff_w_down, i)
        x = x2.reshape(bsz, seq, d)
    return x
```

```python
import functools
import math

import jax
import jax.numpy as jnp
from jax import lax
from jax.experimental import pallas as pl
from jax.experimental.pallas import tpu as pltpu

F32 = jnp.float32
BF16 = jnp.bfloat16

D_MODEL = 1024
BATCH = 4
SEQ = 8192
DEPTH = 4
EPS = 1e-6

HY_SHORT = 3
HY_EMB = 33
HY_BANDS = 16
HY_FW = 64
HY_INNER = 2
HY_MIN_DECAY = math.log(1e-2) / 1.5
HY_MAX_DECAY = math.log(1e-2) / 0.3

HEAD_DIM = 64
N_HEADS = 16
N_KV = 4
GROUP = 4
WINDOW = 128
BLOCK = 128
REL_BUCKETS = 32
REL_MAX_DIST = 128

POOL_WINDOWS = (2, 4, 8, 16)
POOL_GROUP = 256
D_FF = 2816

SUBLANES = 8
LANES = 128
VMEM_LIMIT = 56 * 1024 * 1024

FFT_N = 2 * SEQ
FFT_N1 = 256
FFT_N2 = 64
FFT_K1 = FFT_N1 // 2
SLAB = 2 * FFT_K1 + SUBLANES
CONV_DT = 128
DFT_UNROLL = 32


def _cparams(sem):
    return pltpu.CompilerParams(dimension_semantics=sem, vmem_limit_bytes=VMEM_LIMIT)


def _rms(x, g):
    return x * lax.rsqrt(jnp.mean(x * x, axis=-1, keepdims=True) + EPS) * g


FFN_CHUNK = 256


FFN_CHUNKS = D_FF // FFN_CHUNK


def _ffn_kernel(x_ref, g_ref, wg_ref, wu_ref, wd_ref, o_ref, wgb_ref, wub_ref, wdb_ref, h_ref):
    s = pl.program_id(0)

    @pl.when(s < FFN_CHUNKS)
    def _():
        wgb_ref[s] = wg_ref[...].astype(BF16)
        wub_ref[s] = wu_ref[...].astype(BF16)
        wdb_ref[pl.ds(pl.multiple_of(s * FFN_CHUNK, FFN_CHUNK), FFN_CHUNK), :] = wd_ref[...].astype(BF16)

    @pl.when(s >= FFN_CHUNKS)
    def _():
        x = x_ref[...]
        xn = _rms(x, g_ref[...]).astype(BF16)
        for c in range(FFN_CHUNKS):
            gate = jnp.dot(xn, wgb_ref[c], preferred_element_type=F32)
            up = jnp.dot(xn, wub_ref[c], preferred_element_type=F32)
            h_ref[:, c * FFN_CHUNK:(c + 1) * FFN_CHUNK] = (gate * jax.nn.sigmoid(gate) * up).astype(BF16)
        o_ref[...] = x + jnp.dot(h_ref[...], wdb_ref[...], preferred_element_type=F32)


def _ffn(x2, g, wg, wu, wd, layer, tm=512):
    m, d = x2.shape
    f = wg.shape[2]
    tile = lambda s: jnp.maximum(s - FFN_CHUNKS, 0)
    chunk = lambda s: jnp.minimum(s, FFN_CHUNKS - 1)
    return pl.pallas_call(
        _ffn_kernel,
        grid=(FFN_CHUNKS + m // tm,),
        in_specs=[
            pl.BlockSpec((tm, d), lambda s: (tile(s), 0)),
            pl.BlockSpec((1, d), lambda s: (0, 0)),
            pl.BlockSpec((None, d, FFN_CHUNK), lambda s: (layer, 0, chunk(s))),
            pl.BlockSpec((None, d, FFN_CHUNK), lambda s: (layer, 0, chunk(s))),
            pl.BlockSpec((None, FFN_CHUNK, d), lambda s: (layer, chunk(s), 0)),
        ],
        out_specs=pl.BlockSpec((tm, d), lambda s: (tile(s), 0)),
        out_shape=jax.ShapeDtypeStruct((m, d), F32),
        scratch_shapes=[pltpu.VMEM((FFN_CHUNKS, d, FFN_CHUNK), BF16),
                        pltpu.VMEM((FFN_CHUNKS, d, FFN_CHUNK), BF16),
                        pltpu.VMEM((f, d), BF16),
                        pltpu.VMEM((tm, f), BF16)],
        compiler_params=_cparams(("arbitrary",)),
        name="ffn",
    )(x2, g, wg, wu, wd)


INPROJ_N1 = 16
OUTPROJ_N1 = 8
INPROJ_CHUNK = 512


def _slab_perm(nb):
    r = jnp.arange(FFT_N2 * nb)
    src = FFT_N2 * (r % nb) + r // nb
    return (src[:, None] == r[None, :]).astype(BF16)


def _inproj_kernel(xp_ref, xc_ref, xn_ref, g_ref, pm_ref, w_ref, b_ref, cw_ref, cb_ref, o_ref, xs_ref):
    i = pl.program_id(1)
    nb = INPROJ_N1
    tm = FFT_N2 * nb
    g = g_ref[...]
    xn = _rms(xc_ref[...], g).astype(BF16)
    xs_ref[:tm, :] = jnp.dot(pm_ref[...], xn, preferred_element_type=F32).astype(BF16)
    halo = jnp.concatenate([xp_ref[...], xn_ref[...]], axis=0)
    xs_ref[tm:, :] = _rms(halo, g).astype(BF16)

    xs = xs_ref[...]
    row = lax.broadcasted_iota(jnp.int32, (nb, 1), 0)
    inside_before = i > 0
    inside_after = i < pl.num_programs(1) - 1
    for c in range(w_ref.shape[1] // INPROJ_CHUNK):
        cols = slice(c * INPROJ_CHUNK, (c + 1) * INPROJ_CHUNK)
        p = jnp.dot(xs, w_ref[:, cols], preferred_element_type=F32) + b_ref[:, cols]
        cur = p[:tm]
        before = jnp.where(inside_before, p[tm + SUBLANES - 1:tm + SUBLANES], 0.0)
        after = jnp.where(inside_after, p[tm + SUBLANES:tm + SUBLANES + 1], 0.0)
        first = jnp.where(row == 0, before, pltpu.roll(cur[tm - nb:], 1, 0))
        last = jnp.where(row == nb - 1, after, pltpu.roll(cur[:nb], nb - 1, 0))
        dn = jnp.concatenate([first, cur[:tm - nb]], axis=0)
        up = jnp.concatenate([cur[nb:], last], axis=0)
        cw = cw_ref[:, cols]
        out = cb_ref[:, cols] + cw[0:1] * dn + cw[1:2] * cur + cw[2:3] * up
        o_ref[:, :, cols] = out.reshape(FFT_N2, nb, INPROJ_CHUNK).astype(o_ref.dtype)


def _inproj_conv(x, g, w, b, cw, cb):
    bsz, seq, d = x.shape
    n = w.shape[1]
    nb = INPROJ_N1
    tm = FFT_N2 * nb
    hb = tm // SUBLANES
    last = seq // SUBLANES - 1
    const = lambda bi, i: (0, 0)
    return pl.pallas_call(
        _inproj_kernel,
        grid=(bsz, seq // tm),
        in_specs=[
            pl.BlockSpec((None, SUBLANES, d), lambda bi, i: (bi, jnp.maximum(i * hb - 1, 0), 0)),
            pl.BlockSpec((None, tm, d), lambda bi, i: (bi, i, 0)),
            pl.BlockSpec((None, SUBLANES, d), lambda bi, i: (bi, jnp.minimum((i + 1) * hb, last), 0)),
            pl.BlockSpec((1, d), const),
            pl.BlockSpec((tm, tm), const, pipeline_mode=pl.Buffered(1)),
            pl.BlockSpec((d, n), const, pipeline_mode=pl.Buffered(1)),
            pl.BlockSpec((1, n), const),
            pl.BlockSpec((HY_SHORT, n), const),
            pl.BlockSpec((1, n), const),
        ],
        out_specs=pl.BlockSpec((None, FFT_N2, nb, n), lambda bi, i: (bi, 0, i, 0)),
        out_shape=jax.ShapeDtypeStruct((bsz, FFT_N2, seq // FFT_N2, n), BF16),
        scratch_shapes=[pltpu.VMEM((tm + 2 * SUBLANES, d), BF16)],
        compiler_params=_cparams(("parallel", "parallel")),
        name="hyena_inproj",
    )(x, x, x, g, _slab_perm(nb), w, b, cw, cb)


def _dft_mats():
    i32 = jnp.int32
    k1 = jnp.arange(FFT_K1, dtype=i32)[:, None]
    n1 = jnp.arange(FFT_N1, dtype=i32)[None, :]
    m = ((2 * k1 + 1) * n1) % (2 * FFT_N1)
    ang = m.astype(F32) * (math.pi / FFT_N1)
    f1 = jnp.concatenate([jnp.cos(ang), -jnp.sin(ang)], axis=0)
    k1 = jnp.arange(FFT_K1, dtype=i32)[:, None, None]
    k2 = jnp.arange(FFT_N2, dtype=i32)[None, :, None]
    n2 = jnp.arange(FFT_N2, dtype=i32)[None, None, :]
    m = ((2 * (k1 + FFT_N1 * k2) + 1) * n2) % (2 * FFT_N)
    ang = m.astype(F32) * (math.pi / FFT_N)
    c, s = jnp.cos(ang), jnp.sin(ang)
    gm = jnp.concatenate(
        [jnp.concatenate([c, s], axis=2), jnp.concatenate([-s, c], axis=2)], axis=1)
    return f1, gm


def _dot3(a3, b):
    hi, lo = _split_bf16(b)
    return jnp.dot(a3, jnp.concatenate([hi, hi, lo], axis=0), preferred_element_type=F32)


def _stack3_cols(a):
    hi, lo = _split_bf16(a)
    return jnp.concatenate([hi, lo, hi], axis=-1)


def _tap_lag(n):
    return jnp.where(n >= SEQ, FFT_N - n, n).astype(F32)


def _split_bf16(a):
    hi = a.astype(BF16)
    return hi, (a - hi.astype(F32)).astype(BF16)


def _filter_mlp_kernel(fq_ref, w1t_ref, w1b_ref, b1_ref, wi_ref, bi_ref, fr_ref, h3_ref, *, slabs):
    cols = slabs * FFT_N1
    c = lax.broadcasted_iota(jnp.int32, (1, cols), 1)
    n2 = pl.program_id(0) * slabs + c // FFT_N1
    pos = _tap_lag(FFT_N2 * (c % FFT_N1) + n2)
    t = pos * (1.0 / (SEQ - 1))
    ang = (2.0 * math.pi / SEQ) * pos * fq_ref[...]
    fr = fr_ref[...]
    bands = jnp.concatenate([jnp.cos(ang), -jnp.sin(ang)], axis=0)
    pre = w1t_ref[...] * t + _dot3(w1b_ref[...], bands)
    h = jnp.sin(fr * (pre + b1_ref[...]))
    for l in range(HY_INNER):
        h = jnp.sin(fr * (_dot3(wi_ref[l], h) + bi_ref[l]))
    hi, lo = _split_bf16(h)
    h3 = jnp.concatenate([hi, lo, hi], axis=0)
    for k in range(slabs):
        h3_ref[k] = h3[:, k * FFT_N1:(k + 1) * FFT_N1]


def _filter_mlp(fq, w1, b1, wi, bi, fr, slabs=8):
    w1t, w1b = w1[0:1].T, _stack3_cols(w1[1:].T)
    wi3 = _stack3_cols(jnp.swapaxes(wi, 1, 2))
    col = lambda a: a.reshape(-1, 1)
    const2 = lambda i: (0, 0)
    const3 = lambda i: (0, 0, 0)
    return pl.pallas_call(
        functools.partial(_filter_mlp_kernel, slabs=slabs),
        grid=(FFT_N2 // slabs,),
        in_specs=[
            pl.BlockSpec((HY_BANDS, 1), const2),
            pl.BlockSpec((HY_FW, 1), const2),
            pl.BlockSpec((HY_FW, 3 * 2 * HY_BANDS), const2),
            pl.BlockSpec((HY_FW, 1), const2),
            pl.BlockSpec((HY_INNER, HY_FW, 3 * HY_FW), const3),
            pl.BlockSpec((HY_INNER, HY_FW, 1), const3),
            pl.BlockSpec((HY_FW, 1), const2),
        ],
        out_specs=pl.BlockSpec((slabs, 3 * HY_FW, FFT_N1), lambda i: (i, 0, 0)),
        out_shape=jax.ShapeDtypeStruct((FFT_N2, 3 * HY_FW, FFT_N1), BF16),
        compiler_params=_cparams(("parallel",)),
        name="hyena_filter_mlp",
    )(col(fq), w1t, w1b, col(b1), wi3, bi[:, :, None], col(fr))


def _slab_store(s_ref, n2, a):
    rows = pl.ds(pl.multiple_of(n2 * SLAB, SUBLANES), 2 * FFT_K1)
    for l in range(s_ref.shape[0]):
        s_ref[l, rows, :] = a[:, l * LANES:(l + 1) * LANES]


def _slab_load(s_ref, n2):
    rows = pl.ds(pl.multiple_of(n2 * SLAB, SUBLANES), 2 * FFT_K1)
    return jnp.concatenate([s_ref[l, rows, :] for l in range(s_ref.shape[0])], axis=1).astype(BF16)


def _k1_load(s_ref, k1):
    re = pl.ds(k1, FFT_N2, stride=SLAB)
    im = pl.ds(FFT_K1 + k1, FFT_N2, stride=SLAB)
    tiles = [jnp.concatenate([s_ref[l, re, :], s_ref[l, im, :]], axis=0) for l in range(s_ref.shape[0])]
    return jnp.concatenate(tiles, axis=1).astype(BF16)


def _k1_store(s_ref, k1, b):
    for l in range(s_ref.shape[0]):
        cols = slice(l * LANES, (l + 1) * LANES)
        s_ref[l, pl.ds(k1, FFT_N2, stride=SLAB), :] = b[:FFT_N2, cols]
        s_ref[l, pl.ds(FFT_K1 + k1, FFT_N2, stride=SLAB), :] = b[FFT_N2:, cols]


def _stage_scratch(dt):
    return pltpu.VMEM((dt // LANES, FFT_N2 * SLAB, LANES), F32)


def _stage1(src_ref, f1_ref, s_ref):
    def body(n2, c):
        _slab_store(s_ref, n2, jnp.dot(f1_ref[...], src_ref[n2].astype(BF16), preferred_element_type=F32))
        return c
    lax.fori_loop(0, FFT_N2, body, 0, unroll=DFT_UNROLL)


def _filt_dft_kernel(h3_ref, w3_ref, dl_ref, f1_ref, gm_ref, kf_ref, s_ref):
    half = FFT_N1 // 2
    dt = dl_ref.shape[1]
    n1 = lax.broadcasted_iota(jnp.int32, (FFT_N1, 1), 0)
    dl = dl_ref[...]

    def body(n2, l1):
        both = lax.dot_general(h3_ref[n2], w3_ref[...], (((0,), (0,)), ((), ())),
                               preferred_element_type=F32)
        fwd = both[:half, :dt]
        bwd = both[half:, dt:]
        n = FFT_N2 * n1 + n2
        t = _tap_lag(n) * (1.0 / (SEQ - 1))
        taps = jnp.concatenate([fwd, -bwd], axis=0) * jnp.exp(-t * dl)
        taps = jnp.where(n == SEQ, 0.0, taps)
        _slab_store(s_ref, n2, jnp.dot(f1_ref[...], taps.astype(BF16), preferred_element_type=F32))
        return l1 + jnp.sum(jnp.abs(taps), axis=0, keepdims=True)
    l1 = lax.fori_loop(0, FFT_N2, body, jnp.zeros(dl.shape, F32), unroll=DFT_UNROLL)
    inv = 1.0 / l1

    def stage2(k1, c):
        x = jnp.dot(gm_ref[k1], _k1_load(s_ref, k1), preferred_element_type=F32)
        rows = pl.ds(pl.multiple_of(k1 * 2 * FFT_N2, 2 * FFT_N2), 2 * FFT_N2)
        kf_ref[rows, :] = (x * inv).astype(kf_ref.dtype)
        return c
    lax.fori_loop(0, FFT_K1, stage2, 0, unroll=DFT_UNROLL)


def _filter_spectrum(h3, w3, dl2, f1_full, gm):
    ncol = 2 * D_MODEL
    dt = CONV_DT
    nct = ncol // dt
    tiles = lambda a: a.reshape(HY_FW, 2, nct, dt).transpose(2, 0, 1, 3).reshape(nct, HY_FW, 2 * dt)
    w3h, w3l = _split_bf16(w3)
    w3cat = jnp.concatenate([tiles(w3h), tiles(w3h), tiles(w3l)], axis=1)
    return pl.pallas_call(
        _filt_dft_kernel,
        grid=(nct,),
        in_specs=[
            pl.BlockSpec((FFT_N2, 3 * HY_FW, FFT_N1), lambda j: (0, 0, 0)),
            pl.BlockSpec((None, 3 * HY_FW, 2 * dt), lambda j: (j, 0, 0)),
            pl.BlockSpec((1, dt), lambda j: (0, j)),
            pl.BlockSpec((2 * FFT_K1, FFT_N1), lambda j: (0, 0)),
            pl.BlockSpec((FFT_K1, 2 * FFT_N2, 2 * FFT_N2), lambda j: (0, 0, 0),
                         pipeline_mode=pl.Buffered(1)),
        ],
        out_specs=pl.BlockSpec((FFT_K1 * 2 * FFT_N2, dt), lambda j: (0, j)),
        out_shape=jax.ShapeDtypeStruct((FFT_K1 * 2 * FFT_N2, ncol), F32),
        scratch_shapes=[_stage_scratch(dt)],
        compiler_params=_cparams(("parallel",)),
        name="hyena_filter_spectrum",
    )(h3, w3cat, dl2, f1_full, gm)


def _longconv_kernel(z_ref, g_ref, fb_ref, kf_ref, f1_ref, gm_ref, gh_ref, fi_ref, o_ref, s_ref):
    _stage1(z_ref, f1_ref, s_ref)

    def mid(k1, c):
        x = jnp.dot(gm_ref[k1], _k1_load(s_ref, k1), preferred_element_type=F32)
        kf = kf_ref[pl.ds(pl.multiple_of(k1 * 2 * FFT_N2, 2 * FFT_N2), 2 * FFT_N2), :].astype(F32)
        xr, xi = x[:FFT_N2], x[FFT_N2:]
        kr, ki = kf[:FFT_N2], kf[FFT_N2:]
        y = jnp.concatenate([xr * kr - xi * ki, xr * ki + xi * kr], axis=0).astype(BF16)
        b = jnp.dot(gh_ref[k1], y, preferred_element_type=F32)
        _k1_store(s_ref, k1, b)
        return c
    lax.fori_loop(0, FFT_K1, mid, 0, unroll=2 * DFT_UNROLL)

    fb = fb_ref[...]

    def last(n2, c):
        y = jnp.dot(fi_ref[...], _slab_load(s_ref, n2), preferred_element_type=F32)
        o_ref[n2] = (g_ref[n2] * (y + fb * z_ref[n2])).astype(o_ref.dtype)
        return c
    lax.fori_loop(0, FFT_N2, last, 0, unroll=DFT_UNROLL)


def _long_conv_gate(z, zcol, g, gcol, fb, kf, kcol, f1_half, gm, gh, fi):
    bsz = z.shape[0]
    dt = CONV_DT
    nt = D_MODEL // dt
    n1 = FFT_N1 // 2
    const3 = lambda j, bi: (0, 0, 0)
    return pl.pallas_call(
        _longconv_kernel,
        grid=(nt, bsz),
        in_specs=[
            pl.BlockSpec((None, FFT_N2, n1, dt), lambda j, bi: (bi, 0, 0, zcol + j)),
            pl.BlockSpec((None, FFT_N2, n1, dt), lambda j, bi: (bi, 0, 0, gcol + j)),
            pl.BlockSpec((1, dt), lambda j, bi: (0, j)),
            pl.BlockSpec((FFT_K1 * 2 * FFT_N2, dt), lambda j, bi: (0, kcol + j),
                         pipeline_mode=pl.Buffered(1)),
            pl.BlockSpec((2 * FFT_K1, FFT_N1 // 2), lambda j, bi: (0, 0)),
            pl.BlockSpec((FFT_K1, 2 * FFT_N2, 2 * FFT_N2), const3, pipeline_mode=pl.Buffered(1)),
            pl.BlockSpec((FFT_K1, 2 * FFT_N2, 2 * FFT_N2), const3, pipeline_mode=pl.Buffered(1)),
            pl.BlockSpec((FFT_N1 // 2, 2 * FFT_K1), lambda j, bi: (0, 0)),
        ],
        out_specs=pl.BlockSpec((None, FFT_N2, n1, dt), lambda j, bi: (bi, 0, 0, j)),
        out_shape=jax.ShapeDtypeStruct((bsz, FFT_N2, n1, D_MODEL), BF16),
        scratch_shapes=[_stage_scratch(dt)],
        compiler_params=_cparams(("parallel", "arbitrary")),
        name="hyena_long_conv",
    )(z, g, fb, kf, f1_half, gm, gh, fi)


def _hyena_out_kernel(z_ref, pm_ref, w_ref, b_ref, x_ref, o_ref):
    z = z_ref[...].reshape(x_ref.shape)
    a = jnp.dot(pm_ref[...], z, preferred_element_type=F32).astype(BF16)
    acc = jnp.dot(a, w_ref[...], preferred_element_type=F32)
    o_ref[...] = x_ref[...] + acc + b_ref[...]


def _hyena_out_proj(z, w, b, x):
    bsz, seq, d = x.shape
    nb = OUTPROJ_N1
    tm = FFT_N2 * nb
    return pl.pallas_call(
        _hyena_out_kernel,
        grid=(bsz, seq // tm),
        in_specs=[
            pl.BlockSpec((None, FFT_N2, nb, d), lambda bi, i: (bi, 0, i, 0)),
            pl.BlockSpec((tm, tm), lambda bi, i: (0, 0)),
            pl.BlockSpec((d, d), lambda bi, i: (0, 0)),
            pl.BlockSpec((1, d), lambda bi, i: (0, 0)),
            pl.BlockSpec((None, tm, d), lambda bi, i: (bi, i, 0)),
        ],
        out_specs=pl.BlockSpec((None, tm, d), lambda bi, i: (bi, i, 0)),
        out_shape=jax.ShapeDtypeStruct((bsz, seq, d), F32),
        compiler_params=_cparams(("parallel", "parallel")),
        name="hyena_out_proj",
    )(z, _slab_perm(nb).T, w, b, x)


def _hyena_layer(x, g_mix, w_in, b_in, cw, cb, fw1, fb1, fwi, fbi, fw3, freq, fbias, w_out, b_out, consts):
    f1_full, f1_half, gm, gh, fi, fq, dl2 = consts
    bsz, seq, d = x.shape
    pc = _inproj_conv(x, g_mix[None], w_in.astype(BF16), b_in[None], cw, cb[None])
    h3 = _filter_mlp(fq, fw1, fb1, fwi, fbi, freq)
    kf = _filter_spectrum(h3, fw3, dl2, f1_full, gm)
    nt = D_MODEL // CONV_DT
    z1 = _long_conv_gate(pc, 2 * nt, pc, 0, fbias[0:1], kf, 0, f1_half, gm, gh, fi)
    z2 = _long_conv_gate(z1, 0, pc, nt, fbias[1:2], kf, nt, f1_half, gm, gh, fi)
    return _hyena_out_proj(z2, w_out.astype(BF16), b_out[None], x)


def _relbias_kernel(tab_ref, bucket_ref, band_ref, o_ref):
    bucket = bucket_ref[...]
    band = band_ref[...]
    for h in range(N_HEADS):
        acc = jnp.zeros(bucket.shape, F32)
        for k in range(REL_BUCKETS):
            acc = jnp.where(bucket == k, tab_ref[k, h], acc)
        o_ref[h] = jnp.where(band > 0, acc, -1e30)


def _rel_bias(rel_table):
    a = jnp.arange(BLOCK)[:, None]
    j = jnp.arange(3 * BLOCK)[None, :]
    rel = j - BLOCK - a
    half = REL_BUCKETS // 2
    exact = half // 2
    n = jnp.abs(rel)
    nf = jnp.maximum(n, 1).astype(F32)
    large = exact + (jnp.log(nf / exact) / math.log(REL_MAX_DIST / exact) * (half - exact)).astype(jnp.int32)
    large = jnp.minimum(large, half - 1)
    bucket = (jnp.where(rel > 0, half, 0) + jnp.where(n < exact, n, large)).astype(jnp.int32)
    band = (n <= WINDOW).astype(jnp.int32)
    return pl.pallas_call(
        _relbias_kernel,
        in_specs=[
            pl.BlockSpec(memory_space=pltpu.SMEM),
            pl.BlockSpec((BLOCK, 3 * BLOCK), lambda: (0, 0)),
            pl.BlockSpec((BLOCK, 3 * BLOCK), lambda: (0, 0)),
        ],
        out_specs=pl.BlockSpec((N_HEADS, BLOCK, 3 * BLOCK), lambda: (0, 0, 0)),
        out_shape=jax.ShapeDtypeStruct((N_HEADS, BLOCK, 3 * BLOCK), F32),
        name="rel_bias",
    )(rel_table, bucket, band)


QKV_CHUNK = 512


def _qkv_kernel(x_ref, g_ref, w_ref, gain_ref, flag_ref, bd_ref, o_ref):
    xn = _rms(x_ref[...], g_ref[...]).astype(BF16)
    for c in range(w_ref.shape[1] // QKV_CHUNK):
        cols = slice(c * QKV_CHUNK, (c + 1) * QKV_CHUNK)
        acc = jnp.dot(xn, w_ref[:, cols], preferred_element_type=F32)
        ssq = jnp.dot((acc * acc).astype(BF16), bd_ref[...], preferred_element_type=F32)
        normed = acc * lax.rsqrt(ssq * (1.0 / HEAD_DIM) + EPS) * gain_ref[:, cols]
        o_ref[:, cols] = jnp.where(flag_ref[:, cols] > 0, normed, acc).astype(o_ref.dtype)


def _qkv_proj(x2, g, w, gain, flag, tm=1024):
    m, k = x2.shape
    n = w.shape[1]
    tn = QKV_CHUNK
    eye = jnp.arange(tn)[:, None] // HEAD_DIM == jnp.arange(tn)[None, :] // HEAD_DIM
    const = lambda i: (0, 0)
    return pl.pallas_call(
        _qkv_kernel,
        grid=(m // tm,),
        in_specs=[
            pl.BlockSpec((tm, k), lambda i: (i, 0)),
            pl.BlockSpec((1, k), const),
            pl.BlockSpec((k, n), const),
            pl.BlockSpec((1, n), const),
            pl.BlockSpec((1, n), const),
            pl.BlockSpec((tn, tn), const),
        ],
        out_specs=pl.BlockSpec((tm, n), lambda i: (i, 0)),
        out_shape=jax.ShapeDtypeStruct((m, n), BF16),
        compiler_params=_cparams(("parallel",)),
        name="qkv_proj",
    )(x2, g, w, gain, flag, eye.astype(BF16))


ATTN_QSPLIT = 1


def _attn_kernel(q_ref, kp_ref, kc_ref, kn_ref, bias_ref, sink_ref, o_ref):
    kv = jnp.concatenate([kp_ref[...], kc_ref[...], kn_ref[...]], axis=0)
    lane = lax.broadcasted_iota(jnp.int32, (1, LANES), 1)
    keep = [(lane < HEAD_DIM).astype(F32).astype(BF16), (lane >= HEAD_DIM).astype(F32).astype(BF16)]
    ntile = N_KV // 2
    gsub = GROUP // ATTN_QSPLIT
    for j in range(ntile):
        kt = kv[:, j * LANES:(j + 1) * LANES]
        vt = kv[:, (ntile + j) * LANES:(ntile + j + 1) * LANES]
        for c in range(ATTN_QSPLIT):
            tiles = [GROUP * j + gsub * c + g for g in range(gsub)]
            qs = jnp.concatenate([q_ref[:, t * LANES:(t + 1) * LANES] for t in tiles], axis=0)
            cols = slice(c * gsub * BLOCK, (c + 1) * gsub * BLOCK)
            acc = None
            for half in range(2):
                kh = 2 * j + half
                lg = lax.dot_general(kt * keep[half], qs, (((1,), (1,)), ((), ())),
                                     preferred_element_type=F32)
                lg = lg + bias_ref[kh, :, cols]
                sink = sink_ref[kh, :, cols]
                m = jnp.maximum(jnp.max(lg, axis=0, keepdims=True), sink)
                p = jnp.exp2(lg - m)
                den = jnp.sum(p, axis=0, keepdims=True) + jnp.exp2(sink - m)
                o = lax.dot_general(vt * keep[half], p.astype(BF16), (((0,), (0,)), ((), ())),
                                    preferred_element_type=F32) * (1.0 / den)
                acc = o if acc is None else acc + o
            for g, t in enumerate(tiles):
                o_ref[t * LANES:(t + 1) * LANES, :] = acc[:, g * BLOCK:(g + 1) * BLOCK].astype(o_ref.dtype)


def _attention(qkv, bias_t, sink_row):
    bsz, seq, _ = qkv.shape
    nb = seq // BLOCK
    dq = N_HEADS * HEAD_DIM
    dkv = 2 * N_KV * HEAD_DIM
    kvb = dq // dkv
    return pl.pallas_call(
        _attn_kernel,
        grid=(bsz, nb),
        in_specs=[
            pl.BlockSpec((None, BLOCK, dq), lambda b, n: (b, n, 0)),
            pl.BlockSpec((None, BLOCK, dkv), lambda b, n: (b, jnp.maximum(n - 1, 0), kvb)),
            pl.BlockSpec((None, BLOCK, dkv), lambda b, n: (b, n, kvb)),
            pl.BlockSpec((None, BLOCK, dkv), lambda b, n: (b, jnp.minimum(n + 1, nb - 1), kvb)),
            pl.BlockSpec((None, N_KV, 3 * BLOCK, GROUP * BLOCK),
                         lambda b, n: (jnp.where(n == 0, 0, jnp.where(n == nb - 1, 2, 1)), 0, 0, 0)),
            pl.BlockSpec((N_KV, 1, GROUP * BLOCK), lambda b, n: (0, 0, 0)),
        ],
        out_specs=pl.BlockSpec((None, None, dq, BLOCK), lambda b, n: (b, n, 0, 0)),
        out_shape=jax.ShapeDtypeStruct((bsz, nb, dq, BLOCK), BF16),
        compiler_params=_cparams(("parallel", "parallel")),
        name="window_attention",
    )(qkv, qkv, qkv, qkv, bias_t, sink_row)


ATTN_OUT_BLOCKS = 8


def _attn_out_kernel(ot_ref, w_ref, x_ref, o_ref):
    for i in range(ATTN_OUT_BLOCKS):
        rows = slice(i * BLOCK, (i + 1) * BLOCK)
        y = lax.dot_general(ot_ref[i], w_ref[...], (((0,), (0,)), ((), ())),
                            preferred_element_type=F32)
        o_ref[rows, :] = x_ref[rows, :] + y


def _attn_out_proj(o_t, w, x2):
    nblk, dq, _ = o_t.shape
    m, d = x2.shape
    tm = ATTN_OUT_BLOCKS * BLOCK
    return pl.pallas_call(
        _attn_out_kernel,
        grid=(nblk // ATTN_OUT_BLOCKS,),
        in_specs=[
            pl.BlockSpec((ATTN_OUT_BLOCKS, dq, BLOCK), lambda i: (i, 0, 0)),
            pl.BlockSpec((dq, d), lambda i: (0, 0)),
            pl.BlockSpec((tm, d), lambda i: (i, 0)),
        ],
        out_specs=pl.BlockSpec((tm, d), lambda i: (i, 0)),
        out_shape=jax.ShapeDtypeStruct((m, d), F32),
        compiler_params=_cparams(("parallel",)),
        name="attn_out_proj",
    )(o_t, w, x2)


def _head_tile_perm():
    cols = []
    for tile in range(N_HEADS // 2):
        j, g = tile // GROUP, tile % GROUP
        for half in range(2):
            h = (2 * j + half) * GROUP + g
            cols.extend(range(h * HEAD_DIM, (h + 1) * HEAD_DIM))
    return jnp.asarray(cols, jnp.int32)


def _attn_layer(x, g_mix, w_qkv, q_gain, k_gain, sink, w_o, rel_table):
    bsz, seq, d = x.shape
    x2 = x.reshape(bsz * seq, d)
    dq = N_HEADS * HEAD_DIM
    dk = N_KV * HEAD_DIM
    perm = _head_tile_perm()
    w = jnp.concatenate([w_qkv[:, :dq][:, perm], w_qkv[:, dq:]], axis=1).astype(BF16)
    log2e = math.log2(math.e)
    gain = jnp.concatenate([jnp.tile(q_gain * (HEAD_DIM ** -0.5 * log2e), N_HEADS), jnp.tile(k_gain, N_KV),
                            jnp.ones((dk,), F32)])[None]
    flag = jnp.concatenate([jnp.ones((dq + dk,), F32), jnp.zeros((dk,), F32)])[None]
    qkv = _qkv_proj(x2, g_mix[None], w, gain, flag)
    bias_t = _rel_bias(rel_table).reshape(N_KV, GROUP * BLOCK, 3 * BLOCK).transpose(0, 2, 1)
    key = jnp.arange(3 * BLOCK)[None, :, None]
    bias_t = jnp.stack([jnp.where(key < BLOCK, -1e30, bias_t), bias_t,
                        jnp.where(key >= 2 * BLOCK, -1e30, bias_t)]) * log2e
    sink_row = jnp.repeat(sink * log2e, BLOCK).reshape(N_KV, 1, GROUP * BLOCK)
    o_t = _attention(qkv.reshape(bsz, seq, dq + 2 * dk), bias_t, sink_row)
    y = _attn_out_proj(o_t.reshape(bsz * (seq // BLOCK), dq, BLOCK), w_o[perm, :].astype(BF16), x2)
    return y.reshape(bsz, seq, d)


def _pool_kernel(xp_ref, xc_ref, xn_ref, g_ref, w_ref, b_ref, sc_ref, o_ref, *, tm):
    i = pl.program_id(1)
    g = g_ref[...]
    rows = tm + 2 * SUBLANES
    t = i * tm - SUBLANES + lax.broadcasted_iota(jnp.int32, (rows, 1), 0)
    inside = (t >= 0) & (t < SEQ)
    xa = jnp.concatenate([_rms(xp_ref[...], g), _rms(xc_ref[...], g), _rms(xn_ref[...], g)], axis=0)
    xa = jnp.where(inside, xa, 0.0)
    tc = t[SUBLANES:SUBLANES + tm]
    outs = []
    for gi, wdt in enumerate(POOL_WINDOWS):
        r = wdt // 2
        xg = xa[:, gi * POOL_GROUP:(gi + 1) * POOL_GROUP]
        run = xg
        span = 1
        while span < 2 * r:
            run = run + pltpu.roll(run, rows - span, 0)
            span *= 2
        win = (pltpu.roll(run, r, 0) + pltpu.roll(xg, rows - r, 0))[SUBLANES:SUBLANES + tm]
        cnt = (jnp.minimum(tc + r + 1, SEQ) - jnp.maximum(tc - r, 0)).astype(F32)
        dlt = win * (1.0 / cnt) - xg[SUBLANES:SUBLANES + tm]
        outs.append(jnp.dot(dlt.astype(BF16), w_ref[gi], preferred_element_type=F32))
    y = (jnp.concatenate(outs, axis=-1) + b_ref[...]) * sc_ref[...]
    o_ref[...] = xc_ref[...] + y


def _pool_layer(x, g_mix, w_grp, b, scale, tm=512):
    bsz, seq, d = x.shape
    hb = tm // SUBLANES
    last = seq // SUBLANES - 1
    ng = len(POOL_WINDOWS)
    return pl.pallas_call(
        functools.partial(_pool_kernel, tm=tm),
        grid=(bsz, seq // tm),
        in_specs=[
            pl.BlockSpec((None, SUBLANES, d), lambda bi, i: (bi, jnp.maximum(i * hb - 1, 0), 0)),
            pl.BlockSpec((None, tm, d), lambda bi, i: (bi, i, 0)),
            pl.BlockSpec((None, SUBLANES, d), lambda bi, i: (bi, jnp.minimum((i + 1) * hb, last), 0)),
            pl.BlockSpec((1, d), lambda bi, i: (0, 0)),
            pl.BlockSpec((ng, POOL_GROUP, POOL_GROUP), lambda bi, i: (0, 0, 0)),
            pl.BlockSpec((1, d), lambda bi, i: (0, 0)),
            pl.BlockSpec((1, d), lambda bi, i: (0, 0)),
        ],
        out_specs=pl.BlockSpec((None, tm, d), lambda bi, i: (bi, i, 0)),
        out_shape=jax.ShapeDtypeStruct((bsz, seq, d), F32),
        compiler_params=_cparams(("parallel", "parallel")),
        name="pool_mixer",
    )(x, x, x, g_mix[None], w_grp.astype(BF16), b[None], scale[None])


def kernel(x, norm_mix, norm_ffn, hy_w_in, hy_b_in, hy_conv_w, hy_conv_b, hy_f_w1, hy_f_b1, hy_f_wi, hy_f_bi, hy_f_w3, hy_f_freq, hy_f_bias, hy_w_out, hy_b_out, at_w_qkv, at_q_gain, at_k_gain, at_sink, at_w_o, rel_table, pl_w, pl_b, pl_scale, ff_w_gate, ff_w_up, ff_w_down):
    bsz, seq, d = x.shape
    f1, gm = _dft_mats()
    f1_full = f1.astype(BF16)
    f1_half = f1[:, :FFT_N1 // 2].astype(BF16)
    fi = (f1[:, :FFT_N1 // 2].T * (2.0 / FFT_N)).astype(BF16)
    gh = jnp.swapaxes(gm, 1, 2).astype(BF16)
    gm = gm.astype(BF16)
    fq = jnp.linspace(1e-4, HY_BANDS - 1, HY_BANDS, dtype=F32)[None, :]
    deltas = jnp.abs(jnp.linspace(HY_MIN_DECAY, HY_MAX_DECAY, D_MODEL, dtype=F32))
    dl2 = jnp.concatenate([deltas, deltas])[None, :]
    consts = (f1_full, f1_half, gm, gh, fi, fq, dl2)

    for i in range(DEPTH):
        kind, s = i % 3, i // 3
        if kind == 0:
            x = _hyena_layer(x, norm_mix[i], hy_w_in[s], hy_b_in[s], hy_conv_w[s], hy_conv_b[s],
                             hy_f_w1[s], hy_f_b1[s], hy_f_wi[s], hy_f_bi[s], hy_f_w3[s],
                             hy_f_freq[s], hy_f_bias[s], hy_w_out[s], hy_b_out[s], consts)
        elif kind == 1:
            x = _attn_layer(x, norm_mix[i], at_w_qkv[s], at_q_gain[s], at_k_gain[s], at_sink[s],
                            at_w_o[s], rel_table)
        else:
            x = _pool_layer(x, norm_mix[i], pl_w[s], pl_b[s], pl_scale[s])
        x2 = _ffn(x.reshape(bsz * seq, d), norm_ffn[i][None], ff_w_gate, ff_w_up, ff_w_down, i)
        x = x2.reshape(bsz, seq, d)
    return x
```

```python
import functools
import math

import jax
import jax.numpy as jnp
from jax import lax
from jax.experimental import pallas as pl
from jax.experimental.pallas import tpu as pltpu

F32 = jnp.float32
BF16 = jnp.bfloat16

D_MODEL = 1024
BATCH = 4
SEQ = 8192
DEPTH = 4
EPS = 1e-6

HY_SHORT = 3
HY_EMB = 33
HY_BANDS = 16
HY_FW = 64
HY_INNER = 2
HY_MIN_DECAY = math.log(1e-2) / 1.5
HY_MAX_DECAY = math.log(1e-2) / 0.3

HEAD_DIM = 64
N_HEADS = 16
N_KV = 4
GROUP = 4
WINDOW = 128
BLOCK = 128
REL_BUCKETS = 32
REL_MAX_DIST = 128

POOL_WINDOWS = (2, 4, 8, 16)
POOL_GROUP = 256
D_FF = 2816

SUBLANES = 8
LANES = 128
VMEM_LIMIT = 56 * 1024 * 1024

FFT_N = 2 * SEQ
FFT_N1 = 256
FFT_N2 = 64
FFT_K1 = FFT_N1 // 2
SLAB = 2 * FFT_K1 + SUBLANES
CONV_DT = 128
DFT_UNROLL = 32


def _cparams(sem):
    return pltpu.CompilerParams(dimension_semantics=sem, vmem_limit_bytes=VMEM_LIMIT)


def _rms(x, g):
    return x * lax.rsqrt(jnp.mean(x * x, axis=-1, keepdims=True) + EPS) * g


FFN_CHUNK = 256


FFN_CHUNKS = D_FF // FFN_CHUNK


FFN_TM = 512
FFN_TILES_PER_SEQ = SEQ // FFN_TM


def _ffn_kernel(*refs, tail, n_tail):
    x_ref, g_ref, wg_ref, wu_ref, wd_ref = refs[:5]
    tail_refs = refs[5:5 + n_tail]
    o_ref, wgb_ref, wub_ref, wdb_ref, h_ref = refs[5 + n_tail:]
    s = pl.program_id(0)

    @pl.when(s < FFN_CHUNKS)
    def _():
        wgb_ref[s] = wg_ref[...].astype(BF16)
        wub_ref[s] = wu_ref[...].astype(BF16)
        wdb_ref[pl.ds(pl.multiple_of(s * FFN_CHUNK, FFN_CHUNK), FFN_CHUNK), :] = wd_ref[...].astype(BF16)

    @pl.when(s >= FFN_CHUNKS)
    def _():
        x = tail(s - FFN_CHUNKS, x_ref, *tail_refs)
        xn = _rms(x, g_ref[...]).astype(BF16)
        for c in range(FFN_CHUNKS):
            gate = jnp.dot(xn, wgb_ref[c], preferred_element_type=F32)
            up = jnp.dot(xn, wub_ref[c], preferred_element_type=F32)
            h_ref[:, c * FFN_CHUNK:(c + 1) * FFN_CHUNK] = (gate * jax.nn.sigmoid(gate) * up).astype(BF16)
        o_ref[...] = x + jnp.dot(h_ref[...], wdb_ref[...], preferred_element_type=F32)


def _ffn(x2, g, wg, wu, wd, layer, tail, tail_args, tail_specs):
    m, d = x2.shape
    f = wg.shape[2]
    tm = FFN_TM
    tile = lambda s: jnp.maximum(s - FFN_CHUNKS, 0)
    chunk = lambda s: jnp.minimum(s, FFN_CHUNKS - 1)
    return pl.pallas_call(
        functools.partial(_ffn_kernel, tail=tail, n_tail=len(tail_args)),
        grid=(FFN_CHUNKS + m // tm,),
        in_specs=[
            pl.BlockSpec((tm, d), lambda s: (tile(s), 0)),
            pl.BlockSpec((1, d), lambda s: (0, 0)),
            pl.BlockSpec((None, d, FFN_CHUNK), lambda s: (layer, 0, chunk(s))),
            pl.BlockSpec((None, d, FFN_CHUNK), lambda s: (layer, 0, chunk(s))),
            pl.BlockSpec((None, FFN_CHUNK, d), lambda s: (layer, chunk(s), 0)),
        ] + tail_specs(tile),
        out_specs=pl.BlockSpec((tm, d), lambda s: (tile(s), 0)),
        out_shape=jax.ShapeDtypeStruct((m, d), F32),
        scratch_shapes=[pltpu.VMEM((FFN_CHUNKS, d, FFN_CHUNK), BF16),
                        pltpu.VMEM((FFN_CHUNKS, d, FFN_CHUNK), BF16),
                        pltpu.VMEM((f, d), BF16),
                        pltpu.VMEM((tm, f), BF16)],
        compiler_params=_cparams(("arbitrary",)),
        name="mixer_tail_ffn",
    )(x2, g, wg, wu, wd, *tail_args)


INPROJ_N1 = 16
INPROJ_CHUNK = 512


def _slab_perm(nb):
    r = jnp.arange(FFT_N2 * nb)
    src = FFT_N2 * (r % nb) + r // nb
    return (src[:, None] == r[None, :]).astype(BF16)


def _inproj_kernel(xp_ref, xc_ref, xn_ref, g_ref, pm_ref, w_ref, b_ref, cw_ref, cb_ref, o_ref, xs_ref):
    i = pl.program_id(1)
    nb = INPROJ_N1
    tm = FFT_N2 * nb
    g = g_ref[...]
    xn = _rms(xc_ref[...], g).astype(BF16)
    xs_ref[:tm, :] = jnp.dot(pm_ref[...], xn, preferred_element_type=F32).astype(BF16)
    halo = jnp.concatenate([xp_ref[...], xn_ref[...]], axis=0)
    xs_ref[tm:, :] = _rms(halo, g).astype(BF16)

    xs = xs_ref[...]
    row = lax.broadcasted_iota(jnp.int32, (nb, 1), 0)
    inside_before = i > 0
    inside_after = i < pl.num_programs(1) - 1
    for c in range(w_ref.shape[1] // INPROJ_CHUNK):
        cols = slice(c * INPROJ_CHUNK, (c + 1) * INPROJ_CHUNK)
        p = jnp.dot(xs, w_ref[:, cols], preferred_element_type=F32) + b_ref[:, cols]
        cur = p[:tm]
        before = jnp.where(inside_before, p[tm + SUBLANES - 1:tm + SUBLANES], 0.0)
        after = jnp.where(inside_after, p[tm + SUBLANES:tm + SUBLANES + 1], 0.0)
        first = jnp.where(row == 0, before, pltpu.roll(cur[tm - nb:], 1, 0))
        last = jnp.where(row == nb - 1, after, pltpu.roll(cur[:nb], nb - 1, 0))
        dn = jnp.concatenate([first, cur[:tm - nb]], axis=0)
        up = jnp.concatenate([cur[nb:], last], axis=0)
        cw = cw_ref[:, cols]
        out = cb_ref[:, cols] + cw[0:1] * dn + cw[1:2] * cur + cw[2:3] * up
        o_ref[:, :, cols] = out.reshape(FFT_N2, nb, INPROJ_CHUNK).astype(o_ref.dtype)


def _inproj_conv(x, g, w, b, cw, cb):
    bsz, seq, d = x.shape
    n = w.shape[1]
    nb = INPROJ_N1
    tm = FFT_N2 * nb
    hb = tm // SUBLANES
    last = seq // SUBLANES - 1
    const = lambda bi, i: (0, 0)
    return pl.pallas_call(
        _inproj_kernel,
        grid=(bsz, seq // tm),
        in_specs=[
            pl.BlockSpec((None, SUBLANES, d), lambda bi, i: (bi, jnp.maximum(i * hb - 1, 0), 0)),
            pl.BlockSpec((None, tm, d), lambda bi, i: (bi, i, 0)),
            pl.BlockSpec((None, SUBLANES, d), lambda bi, i: (bi, jnp.minimum((i + 1) * hb, last), 0)),
            pl.BlockSpec((1, d), const),
            pl.BlockSpec((tm, tm), const, pipeline_mode=pl.Buffered(1)),
            pl.BlockSpec((d, n), const, pipeline_mode=pl.Buffered(1)),
            pl.BlockSpec((1, n), const),
            pl.BlockSpec((HY_SHORT, n), const),
            pl.BlockSpec((1, n), const),
        ],
        out_specs=pl.BlockSpec((None, FFT_N2, nb, n), lambda bi, i: (bi, 0, i, 0)),
        out_shape=jax.ShapeDtypeStruct((bsz, FFT_N2, seq // FFT_N2, n), BF16),
        scratch_shapes=[pltpu.VMEM((tm + 2 * SUBLANES, d), BF16)],
        compiler_params=_cparams(("parallel", "parallel")),
        name="hyena_inproj",
    )(x, x, x, g, _slab_perm(nb), w, b, cw, cb)


def _dft_mats():
    i32 = jnp.int32
    k1 = jnp.arange(FFT_K1, dtype=i32)[:, None]
    n1 = jnp.arange(FFT_N1, dtype=i32)[None, :]
    m = ((2 * k1 + 1) * n1) % (2 * FFT_N1)
    ang = m.astype(F32) * (math.pi / FFT_N1)
    f1 = jnp.concatenate([jnp.cos(ang), -jnp.sin(ang)], axis=0)
    k1 = jnp.arange(FFT_K1, dtype=i32)[:, None, None]
    k2 = jnp.arange(FFT_N2, dtype=i32)[None, :, None]
    n2 = jnp.arange(FFT_N2, dtype=i32)[None, None, :]
    m = ((2 * (k1 + FFT_N1 * k2) + 1) * n2) % (2 * FFT_N)
    ang = m.astype(F32) * (math.pi / FFT_N)
    c, s = jnp.cos(ang), jnp.sin(ang)
    gm = jnp.concatenate(
        [jnp.concatenate([c, s], axis=2), jnp.concatenate([-s, c], axis=2)], axis=1)
    return f1, gm


def _dot3(a3, b):
    hi, lo = _split_bf16(b)
    return jnp.dot(a3, jnp.concatenate([hi, hi, lo], axis=0), preferred_element_type=F32)


def _stack3_cols(a):
    hi, lo = _split_bf16(a)
    return jnp.concatenate([hi, lo, hi], axis=-1)


def _tap_lag(n):
    return jnp.where(n >= SEQ, FFT_N - n, n).astype(F32)


def _split_bf16(a):
    hi = a.astype(BF16)
    return hi, (a - hi.astype(F32)).astype(BF16)


def _filter_mlp_kernel(fq_ref, w1t_ref, w1b_ref, b1_ref, wi_ref, bi_ref, fr_ref, h3_ref, *, slabs):
    cols = slabs * FFT_N1
    c = lax.broadcasted_iota(jnp.int32, (1, cols), 1)
    n2 = pl.program_id(0) * slabs + c // FFT_N1
    pos = _tap_lag(FFT_N2 * (c % FFT_N1) + n2)
    t = pos * (1.0 / (SEQ - 1))
    ang = (2.0 * math.pi / SEQ) * pos * fq_ref[...]
    fr = fr_ref[...]
    bands = jnp.concatenate([jnp.cos(ang), -jnp.sin(ang)], axis=0)
    pre = w1t_ref[...] * t + _dot3(w1b_ref[...], bands)
    h = jnp.sin(fr * (pre + b1_ref[...]))
    for l in range(HY_INNER):
        h = jnp.sin(fr * (_dot3(wi_ref[l], h) + bi_ref[l]))
    hi, lo = _split_bf16(h)
    h3 = jnp.concatenate([hi, lo, hi], axis=0)
    for k in range(slabs):
        h3_ref[k] = h3[:, k * FFT_N1:(k + 1) * FFT_N1]


def _filter_mlp(fq, w1, b1, wi, bi, fr, slabs=8):
    w1t, w1b = w1[0:1].T, _stack3_cols(w1[1:].T)
    wi3 = _stack3_cols(jnp.swapaxes(wi, 1, 2))
    col = lambda a: a.reshape(-1, 1)
    const2 = lambda i: (0, 0)
    const3 = lambda i: (0, 0, 0)
    return pl.pallas_call(
        functools.partial(_filter_mlp_kernel, slabs=slabs),
        grid=(FFT_N2 // slabs,),
        in_specs=[
            pl.BlockSpec((HY_BANDS, 1), const2),
            pl.BlockSpec((HY_FW, 1), const2),
            pl.BlockSpec((HY_FW, 3 * 2 * HY_BANDS), const2),
            pl.BlockSpec((HY_FW, 1), const2),
            pl.BlockSpec((HY_INNER, HY_FW, 3 * HY_FW), const3),
            pl.BlockSpec((HY_INNER, HY_FW, 1), const3),
            pl.BlockSpec((HY_FW, 1), const2),
        ],
        out_specs=pl.BlockSpec((slabs, 3 * HY_FW, FFT_N1), lambda i: (i, 0, 0)),
        out_shape=jax.ShapeDtypeStruct((FFT_N2, 3 * HY_FW, FFT_N1), BF16),
        compiler_params=_cparams(("parallel",)),
        name="hyena_filter_mlp",
    )(col(fq), w1t, w1b, col(b1), wi3, bi[:, :, None], col(fr))


def _slab_store(s_ref, n2, a):
    rows = pl.ds(pl.multiple_of(n2 * SLAB, SUBLANES), 2 * FFT_K1)
    for l in range(s_ref.shape[0]):
        s_ref[l, rows, :] = a[:, l * LANES:(l + 1) * LANES]


def _slab_load(s_ref, n2):
    rows = pl.ds(pl.multiple_of(n2 * SLAB, SUBLANES), 2 * FFT_K1)
    return jnp.concatenate([s_ref[l, rows, :] for l in range(s_ref.shape[0])], axis=1).astype(BF16)


def _k1_load(s_ref, k1):
    re = pl.ds(k1, FFT_N2, stride=SLAB)
    im = pl.ds(FFT_K1 + k1, FFT_N2, stride=SLAB)
    tiles = [jnp.concatenate([s_ref[l, re, :], s_ref[l, im, :]], axis=0) for l in range(s_ref.shape[0])]
    return jnp.concatenate(tiles, axis=1).astype(BF16)


def _k1_store(s_ref, k1, b):
    for l in range(s_ref.shape[0]):
        cols = slice(l * LANES, (l + 1) * LANES)
        s_ref[l, pl.ds(k1, FFT_N2, stride=SLAB), :] = b[:FFT_N2, cols]
        s_ref[l, pl.ds(FFT_K1 + k1, FFT_N2, stride=SLAB), :] = b[FFT_N2:, cols]


def _stage_scratch(dt):
    return pltpu.VMEM((dt // LANES, FFT_N2 * SLAB, LANES), F32)


def _stage1(src_ref, f1_ref, s_ref):
    def body(n2, c):
        _slab_store(s_ref, n2, jnp.dot(f1_ref[...], src_ref[n2].astype(BF16), preferred_element_type=F32))
        return c
    lax.fori_loop(0, FFT_N2, body, 0, unroll=DFT_UNROLL)


def _filt_dft_kernel(h3_ref, w3_ref, dl_ref, f1_ref, gm_ref, kf_ref, s_ref):
    half = FFT_N1 // 2
    dt = dl_ref.shape[1]
    n1 = lax.broadcasted_iota(jnp.int32, (FFT_N1, 1), 0)
    dl = dl_ref[...]

    def body(n2, l1):
        both = lax.dot_general(h3_ref[n2], w3_ref[...], (((0,), (0,)), ((), ())),
                               preferred_element_type=F32)
        fwd = both[:half, :dt]
        bwd = both[half:, dt:]
        n = FFT_N2 * n1 + n2
        t = _tap_lag(n) * (1.0 / (SEQ - 1))
        taps = jnp.concatenate([fwd, -bwd], axis=0) * jnp.exp(-t * dl)
        taps = jnp.where(n == SEQ, 0.0, taps)
        _slab_store(s_ref, n2, jnp.dot(f1_ref[...], taps.astype(BF16), preferred_element_type=F32))
        return l1 + jnp.sum(jnp.abs(taps), axis=0, keepdims=True)
    l1 = lax.fori_loop(0, FFT_N2, body, jnp.zeros(dl.shape, F32), unroll=DFT_UNROLL)
    inv = 1.0 / l1

    def stage2(k1, c):
        x = jnp.dot(gm_ref[k1], _k1_load(s_ref, k1), preferred_element_type=F32)
        rows = pl.ds(pl.multiple_of(k1 * 2 * FFT_N2, 2 * FFT_N2), 2 * FFT_N2)
        kf_ref[rows, :] = (x * inv).astype(kf_ref.dtype)
        return c
    lax.fori_loop(0, FFT_K1, stage2, 0, unroll=DFT_UNROLL)


def _filter_spectrum(h3, w3, dl2, f1_full, gm):
    ncol = 2 * D_MODEL
    dt = CONV_DT
    nct = ncol // dt
    tiles = lambda a: a.reshape(HY_FW, 2, nct, dt).transpose(2, 0, 1, 3).reshape(nct, HY_FW, 2 * dt)
    w3h, w3l = _split_bf16(w3)
    w3cat = jnp.concatenate([tiles(w3h), tiles(w3h), tiles(w3l)], axis=1)
    return pl.pallas_call(
        _filt_dft_kernel,
        grid=(nct,),
        in_specs=[
            pl.BlockSpec((FFT_N2, 3 * HY_FW, FFT_N1), lambda j: (0, 0, 0)),
            pl.BlockSpec((None, 3 * HY_FW, 2 * dt), lambda j: (j, 0, 0)),
            pl.BlockSpec((1, dt), lambda j: (0, j)),
            pl.BlockSpec((2 * FFT_K1, FFT_N1), lambda j: (0, 0)),
            pl.BlockSpec((FFT_K1, 2 * FFT_N2, 2 * FFT_N2), lambda j: (0, 0, 0),
                         pipeline_mode=pl.Buffered(1)),
        ],
        out_specs=pl.BlockSpec((FFT_K1 * 2 * FFT_N2, dt), lambda j: (0, j)),
        out_shape=jax.ShapeDtypeStruct((FFT_K1 * 2 * FFT_N2, ncol), F32),
        scratch_shapes=[_stage_scratch(dt)],
        compiler_params=_cparams(("parallel",)),
        name="hyena_filter_spectrum",
    )(h3, w3cat, dl2, f1_full, gm)


def _longconv_kernel(z_ref, g_ref, fb_ref, kf_ref, f1_ref, gm_ref, gh_ref, fi_ref, o_ref, s_ref):
    _stage1(z_ref, f1_ref, s_ref)

    def mid(k1, c):
        x = jnp.dot(gm_ref[k1], _k1_load(s_ref, k1), preferred_element_type=F32)
        kf = kf_ref[pl.ds(pl.multiple_of(k1 * 2 * FFT_N2, 2 * FFT_N2), 2 * FFT_N2), :].astype(F32)
        xr, xi = x[:FFT_N2], x[FFT_N2:]
        kr, ki = kf[:FFT_N2], kf[FFT_N2:]
        y = jnp.concatenate([xr * kr - xi * ki, xr * ki + xi * kr], axis=0).astype(BF16)
        b = jnp.dot(gh_ref[k1], y, preferred_element_type=F32)
        _k1_store(s_ref, k1, b)
        return c
    lax.fori_loop(0, FFT_K1, mid, 0, unroll=2 * DFT_UNROLL)

    fb = fb_ref[...]

    def last(n2, c):
        y = jnp.dot(fi_ref[...], _slab_load(s_ref, n2), preferred_element_type=F32)
        o_ref[n2] = (g_ref[n2] * (y + fb * z_ref[n2])).astype(o_ref.dtype)
        return c
    lax.fori_loop(0, FFT_N2, last, 0, unroll=DFT_UNROLL)


def _long_conv_gate(z, zcol, g, gcol, fb, kf, kcol, f1_half, gm, gh, fi):
    bsz = z.shape[0]
    dt = CONV_DT
    nt = D_MODEL // dt
    n1 = FFT_N1 // 2
    const3 = lambda j, bi: (0, 0, 0)
    return pl.pallas_call(
        _longconv_kernel,
        grid=(nt, bsz),
        in_specs=[
            pl.BlockSpec((None, FFT_N2, n1, dt), lambda j, bi: (bi, 0, 0, zcol + j)),
            pl.BlockSpec((None, FFT_N2, n1, dt), lambda j, bi: (bi, 0, 0, gcol + j)),
            pl.BlockSpec((1, dt), lambda j, bi: (0, j)),
            pl.BlockSpec((FFT_K1 * 2 * FFT_N2, dt), lambda j, bi: (0, kcol + j),
                         pipeline_mode=pl.Buffered(1)),
            pl.BlockSpec((2 * FFT_K1, FFT_N1 // 2), lambda j, bi: (0, 0)),
            pl.BlockSpec((FFT_K1, 2 * FFT_N2, 2 * FFT_N2), const3, pipeline_mode=pl.Buffered(1)),
            pl.BlockSpec((FFT_K1, 2 * FFT_N2, 2 * FFT_N2), const3, pipeline_mode=pl.Buffered(1)),
            pl.BlockSpec((FFT_N1 // 2, 2 * FFT_K1), lambda j, bi: (0, 0)),
        ],
        out_specs=pl.BlockSpec((None, FFT_N2, n1, dt), lambda j, bi: (bi, 0, 0, j)),
        out_shape=jax.ShapeDtypeStruct((bsz, FFT_N2, n1, D_MODEL), BF16),
        scratch_shapes=[_stage_scratch(dt)],
        compiler_params=_cparams(("parallel", "arbitrary")),
        name="hyena_long_conv",
    )(z, g, fb, kf, f1_half, gm, gh, fi)


def _hyena_tail(tile, x_ref, z_ref, pm_ref, w_ref, b_ref):
    z = z_ref[...].reshape(x_ref.shape)
    a = jnp.dot(pm_ref[...], z, preferred_element_type=F32).astype(BF16)
    return x_ref[...] + jnp.dot(a, w_ref[...], preferred_element_type=F32) + b_ref[...]


def _hyena_tail_specs(tile):
    nb = FFN_TM // FFT_N2
    d = D_MODEL
    const = lambda s: (0, 0)
    return [
        pl.BlockSpec((None, FFT_N2, nb, d),
                     lambda s: (tile(s) // FFN_TILES_PER_SEQ, 0, tile(s) % FFN_TILES_PER_SEQ, 0)),
        pl.BlockSpec((FFN_TM, FFN_TM), const),
        pl.BlockSpec((d, d), const),
        pl.BlockSpec((1, d), const),
    ]


def _hyena_mixer(x, g_mix, w_in, b_in, cw, cb, fw1, fb1, fwi, fbi, fw3, freq, fbias, w_out, b_out, consts):
    f1_full, f1_half, gm, gh, fi, fq, dl2 = consts
    pc = _inproj_conv(x, g_mix[None], w_in.astype(BF16), b_in[None], cw, cb[None])
    h3 = _filter_mlp(fq, fw1, fb1, fwi, fbi, freq)
    kf = _filter_spectrum(h3, fw3, dl2, f1_full, gm)
    nt = D_MODEL // CONV_DT
    z1 = _long_conv_gate(pc, 2 * nt, pc, 0, fbias[0:1], kf, 0, f1_half, gm, gh, fi)
    z2 = _long_conv_gate(z1, 0, pc, nt, fbias[1:2], kf, nt, f1_half, gm, gh, fi)
    return (z2, _slab_perm(FFN_TM // FFT_N2).T, w_out.astype(BF16), b_out[None])


def _relbias_kernel(tab_ref, bucket_ref, band_ref, o_ref):
    bucket = bucket_ref[...]
    band = band_ref[...]
    for h in range(N_HEADS):
        acc = jnp.zeros(bucket.shape, F32)
        for k in range(REL_BUCKETS):
            acc = jnp.where(bucket == k, tab_ref[k, h], acc)
        o_ref[h] = jnp.where(band > 0, acc, -1e30)


def _rel_bias(rel_table):
    a = jnp.arange(BLOCK)[:, None]
    j = jnp.arange(3 * BLOCK)[None, :]
    rel = j - BLOCK - a
    half = REL_BUCKETS // 2
    exact = half // 2
    n = jnp.abs(rel)
    nf = jnp.maximum(n, 1).astype(F32)
    large = exact + (jnp.log(nf / exact) / math.log(REL_MAX_DIST / exact) * (half - exact)).astype(jnp.int32)
    large = jnp.minimum(large, half - 1)
    bucket = (jnp.where(rel > 0, half, 0) + jnp.where(n < exact, n, large)).astype(jnp.int32)
    band = (n <= WINDOW).astype(jnp.int32)
    return pl.pallas_call(
        _relbias_kernel,
        in_specs=[
            pl.BlockSpec(memory_space=pltpu.SMEM),
            pl.BlockSpec((BLOCK, 3 * BLOCK), lambda: (0, 0)),
            pl.BlockSpec((BLOCK, 3 * BLOCK), lambda: (0, 0)),
        ],
        out_specs=pl.BlockSpec((N_HEADS, BLOCK, 3 * BLOCK), lambda: (0, 0, 0)),
        out_shape=jax.ShapeDtypeStruct((N_HEADS, BLOCK, 3 * BLOCK), F32),
        name="rel_bias",
    )(rel_table, bucket, band)


QKV_CHUNK = 512


def _qkv_kernel(x_ref, g_ref, w_ref, gain_ref, flag_ref, bd_ref, o_ref):
    xn = _rms(x_ref[...], g_ref[...]).astype(BF16)
    for c in range(w_ref.shape[1] // QKV_CHUNK):
        cols = slice(c * QKV_CHUNK, (c + 1) * QKV_CHUNK)
        acc = jnp.dot(xn, w_ref[:, cols], preferred_element_type=F32)
        ssq = jnp.dot((acc * acc).astype(BF16), bd_ref[...], preferred_element_type=F32)
        normed = acc * lax.rsqrt(ssq * (1.0 / HEAD_DIM) + EPS) * gain_ref[:, cols]
        o_ref[:, cols] = jnp.where(flag_ref[:, cols] > 0, normed, acc).astype(o_ref.dtype)


def _qkv_proj(x2, g, w, gain, flag, tm=1024):
    m, k = x2.shape
    n = w.shape[1]
    tn = QKV_CHUNK
    eye = jnp.arange(tn)[:, None] // HEAD_DIM == jnp.arange(tn)[None, :] // HEAD_DIM
    const = lambda i: (0, 0)
    return pl.pallas_call(
        _qkv_kernel,
        grid=(m // tm,),
        in_specs=[
            pl.BlockSpec((tm, k), lambda i: (i, 0)),
            pl.BlockSpec((1, k), const),
            pl.BlockSpec((k, n), const),
            pl.BlockSpec((1, n), const),
            pl.BlockSpec((1, n), const),
            pl.BlockSpec((tn, tn), const),
        ],
        out_specs=pl.BlockSpec((tm, n), lambda i: (i, 0)),
        out_shape=jax.ShapeDtypeStruct((m, n), BF16),
        compiler_params=_cparams(("parallel",)),
        name="qkv_proj",
    )(x2, g, w, gain, flag, eye.astype(BF16))


ATTN_QSPLIT = 1


def _attn_kernel(q_ref, kp_ref, kc_ref, kn_ref, bias_ref, sink_ref, o_ref):
    kv = jnp.concatenate([kp_ref[...], kc_ref[...], kn_ref[...]], axis=0)
    lane = lax.broadcasted_iota(jnp.int32, (1, LANES), 1)
    keep = [(lane < HEAD_DIM).astype(F32).astype(BF16), (lane >= HEAD_DIM).astype(F32).astype(BF16)]
    ntile = N_KV // 2
    gsub = GROUP // ATTN_QSPLIT
    for j in range(ntile):
        kt = kv[:, j * LANES:(j + 1) * LANES]
        vt = kv[:, (ntile + j) * LANES:(ntile + j + 1) * LANES]
        for c in range(ATTN_QSPLIT):
            tiles = [GROUP * j + gsub * c + g for g in range(gsub)]
            qs = jnp.concatenate([q_ref[:, t * LANES:(t + 1) * LANES] for t in tiles], axis=0)
            cols = slice(c * gsub * BLOCK, (c + 1) * gsub * BLOCK)
            acc = None
            for half in range(2):
                kh = 2 * j + half
                lg = lax.dot_general(kt * keep[half], qs, (((1,), (1,)), ((), ())),
                                     preferred_element_type=F32)
                lg = lg + bias_ref[kh, :, cols]
                sink = sink_ref[kh, :, cols]
                m = jnp.maximum(jnp.max(lg, axis=0, keepdims=True), sink)
                p = jnp.exp2(lg - m)
                den = jnp.sum(p, axis=0, keepdims=True) + jnp.exp2(sink - m)
                o = lax.dot_general(vt * keep[half], p.astype(BF16), (((0,), (0,)), ((), ())),
                                    preferred_element_type=F32) * (1.0 / den)
                acc = o if acc is None else acc + o
            for g, t in enumerate(tiles):
                o_ref[t * LANES:(t + 1) * LANES, :] = acc[:, g * BLOCK:(g + 1) * BLOCK].astype(o_ref.dtype)


def _attention(qkv, bias_t, sink_row):
    bsz, seq, _ = qkv.shape
    nb = seq // BLOCK
    dq = N_HEADS * HEAD_DIM
    dkv = 2 * N_KV * HEAD_DIM
    kvb = dq // dkv
    return pl.pallas_call(
        _attn_kernel,
        grid=(bsz, nb),
        in_specs=[
            pl.BlockSpec((None, BLOCK, dq), lambda b, n: (b, n, 0)),
            pl.BlockSpec((None, BLOCK, dkv), lambda b, n: (b, jnp.maximum(n - 1, 0), kvb)),
            pl.BlockSpec((None, BLOCK, dkv), lambda b, n: (b, n, kvb)),
            pl.BlockSpec((None, BLOCK, dkv), lambda b, n: (b, jnp.minimum(n + 1, nb - 1), kvb)),
            pl.BlockSpec((None, N_KV, 3 * BLOCK, GROUP * BLOCK),
                         lambda b, n: (jnp.where(n == 0, 0, jnp.where(n == nb - 1, 2, 1)), 0, 0, 0)),
            pl.BlockSpec((N_KV, 1, GROUP * BLOCK), lambda b, n: (0, 0, 0)),
        ],
        out_specs=pl.BlockSpec((None, None, dq, BLOCK), lambda b, n: (b, n, 0, 0)),
        out_shape=jax.ShapeDtypeStruct((bsz, nb, dq, BLOCK), BF16),
        compiler_params=_cparams(("parallel", "parallel")),
        name="window_attention",
    )(qkv, qkv, qkv, qkv, bias_t, sink_row)


def _attn_tail(tile, x_ref, ot_ref, w_ref):
    ys = [lax.dot_general(ot_ref[k], w_ref[...], (((0,), (0,)), ((), ())), preferred_element_type=F32)
          for k in range(FFN_TM // BLOCK)]
    return x_ref[...] + jnp.concatenate(ys, axis=0)


def _attn_tail_specs(tile):
    dq = N_HEADS * HEAD_DIM
    return [
        pl.BlockSpec((FFN_TM // BLOCK, dq, BLOCK), lambda s: (tile(s), 0, 0)),
        pl.BlockSpec((dq, D_MODEL), lambda s: (0, 0)),
    ]


def _head_tile_perm():
    cols = []
    for tile in range(N_HEADS // 2):
        j, g = tile // GROUP, tile % GROUP
        for half in range(2):
            h = (2 * j + half) * GROUP + g
            cols.extend(range(h * HEAD_DIM, (h + 1) * HEAD_DIM))
    return jnp.asarray(cols, jnp.int32)


def _attn_mixer(x, g_mix, w_qkv, q_gain, k_gain, sink, w_o, rel_table):
    bsz, seq, d = x.shape
    x2 = x.reshape(bsz * seq, d)
    dq = N_HEADS * HEAD_DIM
    dk = N_KV * HEAD_DIM
    perm = _head_tile_perm()
    w = jnp.concatenate([w_qkv[:, :dq][:, perm], w_qkv[:, dq:]], axis=1).astype(BF16)
    log2e = math.log2(math.e)
    gain = jnp.concatenate([jnp.tile(q_gain * (HEAD_DIM ** -0.5 * log2e), N_HEADS), jnp.tile(k_gain, N_KV),
                            jnp.ones((dk,), F32)])[None]
    flag = jnp.concatenate([jnp.ones((dq + dk,), F32), jnp.zeros((dk,), F32)])[None]
    qkv = _qkv_proj(x2, g_mix[None], w, gain, flag)
    bias_t = _rel_bias(rel_table).reshape(N_KV, GROUP * BLOCK, 3 * BLOCK).transpose(0, 2, 1)
    key = jnp.arange(3 * BLOCK)[None, :, None]
    bias_t = jnp.stack([jnp.where(key < BLOCK, -1e30, bias_t), bias_t,
                        jnp.where(key >= 2 * BLOCK, -1e30, bias_t)]) * log2e
    sink_row = jnp.repeat(sink * log2e, BLOCK).reshape(N_KV, 1, GROUP * BLOCK)
    o_t = _attention(qkv.reshape(bsz, seq, dq + 2 * dk), bias_t, sink_row)
    return (o_t.reshape(bsz * (seq // BLOCK), dq, BLOCK), w_o[perm, :].astype(BF16))


def _pool_tail(tile, xc_ref, xp_ref, xn_ref, g_ref, w_ref, b_ref, sc_ref):
    tm = FFN_TM
    g = g_ref[...]
    rows = tm + 2 * SUBLANES
    t0 = (tile % FFN_TILES_PER_SEQ) * tm
    t = t0 - SUBLANES + lax.broadcasted_iota(jnp.int32, (rows, 1), 0)
    inside = (t >= 0) & (t < SEQ)
    xa = jnp.concatenate([_rms(xp_ref[...], g), _rms(xc_ref[...], g), _rms(xn_ref[...], g)], axis=0)
    xa = jnp.where(inside, xa, 0.0)
    tc = t[SUBLANES:SUBLANES + tm]
    outs = []
    for gi, wdt in enumerate(POOL_WINDOWS):
        r = wdt // 2
        xg = xa[:, gi * POOL_GROUP:(gi + 1) * POOL_GROUP]
        run = xg
        span = 1
        while span < 2 * r:
            run = run + pltpu.roll(run, rows - span, 0)
            span *= 2
        win = (pltpu.roll(run, r, 0) + pltpu.roll(xg, rows - r, 0))[SUBLANES:SUBLANES + tm]
        cnt = (jnp.minimum(tc + r + 1, SEQ) - jnp.maximum(tc - r, 0)).astype(F32)
        dlt = win * (1.0 / cnt) - xg[SUBLANES:SUBLANES + tm]
        outs.append(jnp.dot(dlt.astype(BF16), w_ref[gi], preferred_element_type=F32))
    y = (jnp.concatenate(outs, axis=-1) + b_ref[...]) * sc_ref[...]
    return xc_ref[...] + y


def _pool_tail_specs(tile, rows):
    d = D_MODEL
    hb = FFN_TM // SUBLANES
    last = rows // SUBLANES - 1
    const = lambda s: (0, 0)
    return [
        pl.BlockSpec((SUBLANES, d), lambda s: (jnp.maximum(tile(s) * hb - 1, 0), 0)),
        pl.BlockSpec((SUBLANES, d), lambda s: (jnp.minimum((tile(s) + 1) * hb, last), 0)),
        pl.BlockSpec((1, d), const),
        pl.BlockSpec((len(POOL_WINDOWS), POOL_GROUP, POOL_GROUP), lambda s: (0, 0, 0)),
        pl.BlockSpec((1, d), const),
        pl.BlockSpec((1, d), const),
    ]


def kernel(x, norm_mix, norm_ffn, hy_w_in, hy_b_in, hy_conv_w, hy_conv_b, hy_f_w1, hy_f_b1, hy_f_wi, hy_f_bi, hy_f_w3, hy_f_freq, hy_f_bias, hy_w_out, hy_b_out, at_w_qkv, at_q_gain, at_k_gain, at_sink, at_w_o, rel_table, pl_w, pl_b, pl_scale, ff_w_gate, ff_w_up, ff_w_down):
    bsz, seq, d = x.shape
    f1, gm = _dft_mats()
    f1_full = f1.astype(BF16)
    f1_half = f1[:, :FFT_N1 // 2].astype(BF16)
    fi = (f1[:, :FFT_N1 // 2].T * (2.0 / FFT_N)).astype(BF16)
    gh = jnp.swapaxes(gm, 1, 2).astype(BF16)
    gm = gm.astype(BF16)
    fq = jnp.linspace(1e-4, HY_BANDS - 1, HY_BANDS, dtype=F32)[None, :]
    deltas = jnp.abs(jnp.linspace(HY_MIN_DECAY, HY_MAX_DECAY, D_MODEL, dtype=F32))
    dl2 = jnp.concatenate([deltas, deltas])[None, :]
    consts = (f1_full, f1_half, gm, gh, fi, fq, dl2)

    x2 = x.reshape(bsz * seq, d)
    for i in range(DEPTH):
        kind, s = i % 3, i // 3
        x = x2.reshape(bsz, seq, d)
        if kind == 0:
            args = _hyena_mixer(x, norm_mix[i], hy_w_in[s], hy_b_in[s], hy_conv_w[s], hy_conv_b[s],
                                hy_f_w1[s], hy_f_b1[s], hy_f_wi[s], hy_f_bi[s], hy_f_w3[s],
                                hy_f_freq[s], hy_f_bias[s], hy_w_out[s], hy_b_out[s], consts)
            tail, specs = _hyena_tail, _hyena_tail_specs
        elif kind == 1:
            args = _attn_mixer(x, norm_mix[i], at_w_qkv[s], at_q_gain[s], at_k_gain[s], at_sink[s],
                               at_w_o[s], rel_table)
            tail, specs = _attn_tail, _attn_tail_specs
        else:
            args = (x2, x2, norm_mix[i][None], pl_w[s].astype(BF16), pl_b[s][None], pl_scale[s][None])
            tail, specs = _pool_tail, functools.partial(_pool_tail_specs, rows=bsz * seq)
        x2 = _ffn(x2, norm_ffn[i][None], ff_w_gate, ff_w_up, ff_w_down, i, tail, args, specs)
    return x2.reshape(bsz, seq, d)
```

```python
import functools
import math

import jax
import jax.numpy as jnp
from jax import lax
from jax.experimental import pallas as pl
from jax.experimental.pallas import tpu as pltpu

F32 = jnp.float32
BF16 = jnp.bfloat16

D_MODEL = 1024
BATCH = 4
SEQ = 8192
DEPTH = 4
EPS = 1e-6

HY_SHORT = 3
HY_EMB = 33
HY_BANDS = 16
HY_FW = 64
HY_INNER = 2
HY_MIN_DECAY = math.log(1e-2) / 1.5
HY_MAX_DECAY = math.log(1e-2) / 0.3

HEAD_DIM = 64
N_HEADS = 16
N_KV = 4
GROUP = 4
WINDOW = 128
BLOCK = 128
REL_BUCKETS = 32
REL_MAX_DIST = 128

POOL_WINDOWS = (2, 4, 8, 16)
POOL_GROUP = 256
D_FF = 2816

SUBLANES = 8
LANES = 128
VMEM_LIMIT = 56 * 1024 * 1024

FFT_N = 2 * SEQ
FFT_N1 = 256
FFT_N2 = 64
FFT_K1 = FFT_N1 // 2
SLAB = 2 * FFT_K1 + SUBLANES
CONV_DT = 128
DFT_UNROLL = 32


def _cparams(sem):
    return pltpu.CompilerParams(dimension_semantics=sem, vmem_limit_bytes=VMEM_LIMIT)


def _rms(x, g):
    return x * lax.rsqrt(jnp.mean(x * x, axis=-1, keepdims=True) + EPS) * g


FFN_CHUNK = 256


FFN_CHUNKS = D_FF // FFN_CHUNK


FFN_TM = 512
FFN_TILES_PER_SEQ = SEQ // FFN_TM


def _ffn_kernel(*refs, tail, n_tail):
    x_ref, g_ref, wg_ref, wu_ref, wd_ref = refs[:5]
    tail_refs = refs[5:5 + n_tail]
    o_ref, wgb_ref, wub_ref, wdb_ref, h_ref = refs[5 + n_tail:]
    s = pl.program_id(0)

    @pl.when(s < FFN_CHUNKS)
    def _():
        wgb_ref[s] = wg_ref[...].astype(BF16)
        wub_ref[s] = wu_ref[...].astype(BF16)
        wdb_ref[pl.ds(pl.multiple_of(s * FFN_CHUNK, FFN_CHUNK), FFN_CHUNK), :] = wd_ref[...].astype(BF16)

    @pl.when(s >= FFN_CHUNKS)
    def _():
        x = tail(s - FFN_CHUNKS, x_ref, *tail_refs)
        xn = _rms(x, g_ref[...]).astype(BF16)
        for c in range(FFN_CHUNKS):
            gate = jnp.dot(xn, wgb_ref[c], preferred_element_type=F32)
            up = jnp.dot(xn, wub_ref[c], preferred_element_type=F32)
            h_ref[:, c * FFN_CHUNK:(c + 1) * FFN_CHUNK] = (gate * jax.nn.sigmoid(gate) * up).astype(BF16)
        o_ref[...] = x + jnp.dot(h_ref[...], wdb_ref[...], preferred_element_type=F32)


def _ffn(x2, g, wg, wu, wd, layer, tail, tail_args, tail_specs):
    m, d = x2.shape
    f = wg.shape[2]
    tm = FFN_TM
    tile = lambda s: jnp.maximum(s - FFN_CHUNKS, 0)
    chunk = lambda s: jnp.minimum(s, FFN_CHUNKS - 1)
    return pl.pallas_call(
        functools.partial(_ffn_kernel, tail=tail, n_tail=len(tail_args)),
        grid=(FFN_CHUNKS + m // tm,),
        in_specs=[
            pl.BlockSpec((tm, d), lambda s: (tile(s), 0)),
            pl.BlockSpec((1, d), lambda s: (0, 0)),
            pl.BlockSpec((None, d, FFN_CHUNK), lambda s: (layer, 0, chunk(s))),
            pl.BlockSpec((None, d, FFN_CHUNK), lambda s: (layer, 0, chunk(s))),
            pl.BlockSpec((None, FFN_CHUNK, d), lambda s: (layer, chunk(s), 0)),
        ] + tail_specs(tile),
        out_specs=pl.BlockSpec((tm, d), lambda s: (tile(s), 0)),
        out_shape=jax.ShapeDtypeStruct((m, d), F32),
        scratch_shapes=[pltpu.VMEM((FFN_CHUNKS, d, FFN_CHUNK), BF16),
                        pltpu.VMEM((FFN_CHUNKS, d, FFN_CHUNK), BF16),
                        pltpu.VMEM((f, d), BF16),
                        pltpu.VMEM((tm, f), BF16)],
        compiler_params=_cparams(("arbitrary",)),
        name="mixer_tail_ffn",
    )(x2, g, wg, wu, wd, *tail_args)


INPROJ_N1 = 16
INPROJ_CHUNK = 512


def _slab_perm(nb):
    r = jnp.arange(FFT_N2 * nb)
    src = FFT_N2 * (r % nb) + r // nb
    return (src[:, None] == r[None, :]).astype(BF16)


def _inproj_kernel(xp_ref, xc_ref, xn_ref, g_ref, pm_ref, w_ref, b_ref, cw_ref, cb_ref, o_ref, xs_ref):
    i = pl.program_id(1)
    nb = INPROJ_N1
    tm = FFT_N2 * nb
    g = g_ref[...]
    xn = _rms(xc_ref[...], g).astype(BF16)
    xs_ref[:tm, :] = jnp.dot(pm_ref[...], xn, preferred_element_type=F32).astype(BF16)
    halo = jnp.concatenate([xp_ref[...], xn_ref[...]], axis=0)
    xs_ref[tm:, :] = _rms(halo, g).astype(BF16)

    xs = xs_ref[...]
    row = lax.broadcasted_iota(jnp.int32, (nb, 1), 0)
    inside_before = i > 0
    inside_after = i < pl.num_programs(1) - 1
    for c in range(w_ref.shape[1] // INPROJ_CHUNK):
        cols = slice(c * INPROJ_CHUNK, (c + 1) * INPROJ_CHUNK)
        p = jnp.dot(xs, w_ref[:, cols], preferred_element_type=F32) + b_ref[:, cols]
        cur = p[:tm]
        before = jnp.where(inside_before, p[tm + SUBLANES - 1:tm + SUBLANES], 0.0)
        after = jnp.where(inside_after, p[tm + SUBLANES:tm + SUBLANES + 1], 0.0)
        first = jnp.where(row == 0, before, pltpu.roll(cur[tm - nb:], 1, 0))
        last = jnp.where(row == nb - 1, after, pltpu.roll(cur[:nb], nb - 1, 0))
        dn = jnp.concatenate([first, cur[:tm - nb]], axis=0)
        up = jnp.concatenate([cur[nb:], last], axis=0)
        cw = cw_ref[:, cols]
        out = cb_ref[:, cols] + cw[0:1] * dn + cw[1:2] * cur + cw[2:3] * up
        o_ref[:, :, cols] = out.reshape(FFT_N2, nb, INPROJ_CHUNK).astype(o_ref.dtype)


def _inproj_conv(x, g, w, b, cw, cb):
    bsz, seq, d = x.shape
    n = w.shape[1]
    nb = INPROJ_N1
    tm = FFT_N2 * nb
    hb = tm // SUBLANES
    last = seq // SUBLANES - 1
    const = lambda bi, i: (0, 0)
    return pl.pallas_call(
        _inproj_kernel,
        grid=(bsz, seq // tm),
        in_specs=[
            pl.BlockSpec((None, SUBLANES, d), lambda bi, i: (bi, jnp.maximum(i * hb - 1, 0), 0)),
            pl.BlockSpec((None, tm, d), lambda bi, i: (bi, i, 0)),
            pl.BlockSpec((None, SUBLANES, d), lambda bi, i: (bi, jnp.minimum((i + 1) * hb, last), 0)),
            pl.BlockSpec((1, d), const),
            pl.BlockSpec((tm, tm), const, pipeline_mode=pl.Buffered(1)),
            pl.BlockSpec((d, n), const, pipeline_mode=pl.Buffered(1)),
            pl.BlockSpec((1, n), const),
            pl.BlockSpec((HY_SHORT, n), const),
            pl.BlockSpec((1, n), const),
        ],
        out_specs=pl.BlockSpec((None, FFT_N2, nb, n), lambda bi, i: (bi, 0, i, 0)),
        out_shape=jax.ShapeDtypeStruct((bsz, FFT_N2, seq // FFT_N2, n), BF16),
        scratch_shapes=[pltpu.VMEM((tm + 2 * SUBLANES, d), BF16)],
        compiler_params=_cparams(("parallel", "parallel")),
        name="hyena_inproj",
    )(x, x, x, g, _slab_perm(nb), w, b, cw, cb)


def _dft_mats():
    i32 = jnp.int32
    k1 = jnp.arange(FFT_K1, dtype=i32)[:, None]
    n1 = jnp.arange(FFT_N1, dtype=i32)[None, :]
    m = ((2 * k1 + 1) * n1) % (2 * FFT_N1)
    ang = m.astype(F32) * (math.pi / FFT_N1)
    f1 = jnp.concatenate([jnp.cos(ang), -jnp.sin(ang)], axis=0)
    k1 = jnp.arange(FFT_K1, dtype=i32)[:, None, None]
    k2 = jnp.arange(FFT_N2, dtype=i32)[None, :, None]
    n2 = jnp.arange(FFT_N2, dtype=i32)[None, None, :]
    m = ((2 * (k1 + FFT_N1 * k2) + 1) * n2) % (2 * FFT_N)
    ang = m.astype(F32) * (math.pi / FFT_N)
    c, s = jnp.cos(ang), jnp.sin(ang)
    gm = jnp.concatenate(
        [jnp.concatenate([c, s], axis=2), jnp.concatenate([-s, c], axis=2)], axis=1)
    return f1, gm


def _dot3(a3, b):
    hi, lo = _split_bf16(b)
    return jnp.dot(a3, jnp.concatenate([hi, hi, lo], axis=0), preferred_element_type=F32)


def _stack3_cols(a):
    hi, lo = _split_bf16(a)
    return jnp.concatenate([hi, lo, hi], axis=-1)


def _tap_lag(n):
    return jnp.where(n >= SEQ, FFT_N - n, n).astype(F32)


def _split_bf16(a):
    hi = a.astype(BF16)
    return hi, (a - hi.astype(F32)).astype(BF16)


def _filter_mlp_kernel(fq_ref, w1t_ref, w1b_ref, b1_ref, wi_ref, bi_ref, fr_ref, h3_ref, *, slabs):
    cols = slabs * FFT_N1
    c = lax.broadcasted_iota(jnp.int32, (1, cols), 1)
    n2 = pl.program_id(0) * slabs + c // FFT_N1
    pos = _tap_lag(FFT_N2 * (c % FFT_N1) + n2)
    t = pos * (1.0 / (SEQ - 1))
    ang = (2.0 * math.pi / SEQ) * pos * fq_ref[...]
    fr = fr_ref[...]
    bands = jnp.concatenate([jnp.cos(ang), -jnp.sin(ang)], axis=0)
    pre = w1t_ref[...] * t + _dot3(w1b_ref[...], bands)
    h = jnp.sin(fr * (pre + b1_ref[...]))
    for l in range(HY_INNER):
        h = jnp.sin(fr * (_dot3(wi_ref[l], h) + bi_ref[l]))
    hi, lo = _split_bf16(h)
    h3 = jnp.concatenate([hi, lo, hi], axis=0)
    for k in range(slabs):
        h3_ref[k] = h3[:, k * FFT_N1:(k + 1) * FFT_N1]


def _filter_mlp(fq, w1, b1, wi, bi, fr, slabs=8):
    w1t, w1b = w1[0:1].T, _stack3_cols(w1[1:].T)
    wi3 = _stack3_cols(jnp.swapaxes(wi, 1, 2))
    col = lambda a: a.reshape(-1, 1)
    const2 = lambda i: (0, 0)
    const3 = lambda i: (0, 0, 0)
    return pl.pallas_call(
        functools.partial(_filter_mlp_kernel, slabs=slabs),
        grid=(FFT_N2 // slabs,),
        in_specs=[
            pl.BlockSpec((HY_BANDS, 1), const2),
            pl.BlockSpec((HY_FW, 1), const2),
            pl.BlockSpec((HY_FW, 3 * 2 * HY_BANDS), const2),
            pl.BlockSpec((HY_FW, 1), const2),
            pl.BlockSpec((HY_INNER, HY_FW, 3 * HY_FW), const3),
            pl.BlockSpec((HY_INNER, HY_FW, 1), const3),
            pl.BlockSpec((HY_FW, 1), const2),
        ],
        out_specs=pl.BlockSpec((slabs, 3 * HY_FW, FFT_N1), lambda i: (i, 0, 0)),
        out_shape=jax.ShapeDtypeStruct((FFT_N2, 3 * HY_FW, FFT_N1), BF16),
        compiler_params=_cparams(("parallel",)),
        name="hyena_filter_mlp",
    )(col(fq), w1t, w1b, col(b1), wi3, bi[:, :, None], col(fr))


def _slab_store(s_ref, n2, a):
    rows = pl.ds(pl.multiple_of(n2 * SLAB, SUBLANES), 2 * FFT_K1)
    for l in range(s_ref.shape[0]):
        s_ref[l, rows, :] = a[:, l * LANES:(l + 1) * LANES]


def _slab_load(s_ref, n2):
    rows = pl.ds(pl.multiple_of(n2 * SLAB, SUBLANES), 2 * FFT_K1)
    return jnp.concatenate([s_ref[l, rows, :] for l in range(s_ref.shape[0])], axis=1).astype(BF16)


def _k1_load(s_ref, k1):
    re = pl.ds(k1, FFT_N2, stride=SLAB)
    im = pl.ds(FFT_K1 + k1, FFT_N2, stride=SLAB)
    tiles = [jnp.concatenate([s_ref[l, re, :], s_ref[l, im, :]], axis=0) for l in range(s_ref.shape[0])]
    return jnp.concatenate(tiles, axis=1).astype(BF16)


def _k1_store(s_ref, k1, b):
    for l in range(s_ref.shape[0]):
        cols = slice(l * LANES, (l + 1) * LANES)
        s_ref[l, pl.ds(k1, FFT_N2, stride=SLAB), :] = b[:FFT_N2, cols]
        s_ref[l, pl.ds(FFT_K1 + k1, FFT_N2, stride=SLAB), :] = b[FFT_N2:, cols]


def _stage_scratch(dt):
    return pltpu.VMEM((dt // LANES, FFT_N2 * SLAB, LANES), F32)


def _stage1(src_ref, f1_ref, s_ref):
    def body(n2, c):
        _slab_store(s_ref, n2, jnp.dot(f1_ref[...], src_ref[n2].astype(BF16), preferred_element_type=F32))
        return c
    lax.fori_loop(0, FFT_N2, body, 0, unroll=DFT_UNROLL)


def _filt_dft_kernel(h3_ref, w3_ref, dl_ref, f1_ref, gm_ref, kf_ref, s_ref):
    half = FFT_N1 // 2
    dt = dl_ref.shape[1]
    n1 = lax.broadcasted_iota(jnp.int32, (FFT_N1, 1), 0)
    dl = dl_ref[...]

    def body(n2, l1):
        both = lax.dot_general(h3_ref[n2], w3_ref[...], (((0,), (0,)), ((), ())),
                               preferred_element_type=F32)
        fwd = both[:half, :dt]
        bwd = both[half:, dt:]
        n = FFT_N2 * n1 + n2
        t = _tap_lag(n) * (1.0 / (SEQ - 1))
        taps = jnp.concatenate([fwd, -bwd], axis=0) * jnp.exp(-t * dl)
        taps = jnp.where(n == SEQ, 0.0, taps)
        _slab_store(s_ref, n2, jnp.dot(f1_ref[...], taps.astype(BF16), preferred_element_type=F32))
        return l1 + jnp.sum(jnp.abs(taps), axis=0, keepdims=True)
    l1 = lax.fori_loop(0, FFT_N2, body, jnp.zeros(dl.shape, F32), unroll=DFT_UNROLL)
    inv = 1.0 / l1

    def stage2(k1, c):
        x = jnp.dot(gm_ref[k1], _k1_load(s_ref, k1), preferred_element_type=F32)
        rows = pl.ds(pl.multiple_of(k1 * 2 * FFT_N2, 2 * FFT_N2), 2 * FFT_N2)
        kf_ref[rows, :] = (x * inv).astype(kf_ref.dtype)
        return c
    lax.fori_loop(0, FFT_K1, stage2, 0, unroll=DFT_UNROLL)


def _filter_spectrum(h3, w3, dl2, f1_full, gm):
    ncol = 2 * D_MODEL
    dt = CONV_DT
    nct = ncol // dt
    tiles = lambda a: a.reshape(HY_FW, 2, nct, dt).transpose(2, 0, 1, 3).reshape(nct, HY_FW, 2 * dt)
    w3h, w3l = _split_bf16(w3)
    w3cat = jnp.concatenate([tiles(w3h), tiles(w3h), tiles(w3l)], axis=1)
    return pl.pallas_call(
        _filt_dft_kernel,
        grid=(nct,),
        in_specs=[
            pl.BlockSpec((FFT_N2, 3 * HY_FW, FFT_N1), lambda j: (0, 0, 0)),
            pl.BlockSpec((None, 3 * HY_FW, 2 * dt), lambda j: (j, 0, 0)),
            pl.BlockSpec((1, dt), lambda j: (0, j)),
            pl.BlockSpec((2 * FFT_K1, FFT_N1), lambda j: (0, 0)),
            pl.BlockSpec((FFT_K1, 2 * FFT_N2, 2 * FFT_N2), lambda j: (0, 0, 0),
                         pipeline_mode=pl.Buffered(1)),
        ],
        out_specs=pl.BlockSpec((FFT_K1 * 2 * FFT_N2, dt), lambda j: (0, j)),
        out_shape=jax.ShapeDtypeStruct((FFT_K1 * 2 * FFT_N2, ncol), F32),
        scratch_shapes=[_stage_scratch(dt)],
        compiler_params=_cparams(("parallel",)),
        name="hyena_filter_spectrum",
    )(h3, w3cat, dl2, f1_full, gm)


def _longconv_kernel(z_ref, g_ref, fb_ref, kf_ref, f1_ref, gm_ref, gh_ref, fi_ref, o_ref, s_ref):
    _stage1(z_ref, f1_ref, s_ref)

    def mid(k1, c):
        x = jnp.dot(gm_ref[k1], _k1_load(s_ref, k1), preferred_element_type=F32)
        kf = kf_ref[pl.ds(pl.multiple_of(k1 * 2 * FFT_N2, 2 * FFT_N2), 2 * FFT_N2), :].astype(F32)
        xr, xi = x[:FFT_N2], x[FFT_N2:]
        kr, ki = kf[:FFT_N2], kf[FFT_N2:]
        y = jnp.concatenate([xr * kr - xi * ki, xr * ki + xi * kr], axis=0).astype(BF16)
        b = jnp.dot(gh_ref[k1], y, preferred_element_type=F32)
        _k1_store(s_ref, k1, b)
        return c
    lax.fori_loop(0, FFT_K1, mid, 0, unroll=2 * DFT_UNROLL)

    fb = fb_ref[...]

    def last(n2, c):
        y = jnp.dot(fi_ref[...], _slab_load(s_ref, n2), preferred_element_type=F32)
        o_ref[n2] = (g_ref[n2] * (y + fb * z_ref[n2])).astype(o_ref.dtype)
        return c
    lax.fori_loop(0, FFT_N2, last, 0, unroll=DFT_UNROLL)


def _long_conv_gate(z, zcol, g, gcol, fb, kf, kcol, f1_half, gm, gh, fi):
    bsz = z.shape[0]
    dt = CONV_DT
    nt = D_MODEL // dt
    n1 = FFT_N1 // 2
    const3 = lambda j, bi: (0, 0, 0)
    return pl.pallas_call(
        _longconv_kernel,
        grid=(nt, bsz),
        in_specs=[
            pl.BlockSpec((None, FFT_N2, n1, dt), lambda j, bi: (bi, 0, 0, zcol + j)),
            pl.BlockSpec((None, FFT_N2, n1, dt), lambda j, bi: (bi, 0, 0, gcol + j)),
            pl.BlockSpec((1, dt), lambda j, bi: (0, j)),
            pl.BlockSpec((FFT_K1 * 2 * FFT_N2, dt), lambda j, bi: (0, kcol + j),
                         pipeline_mode=pl.Buffered(1)),
            pl.BlockSpec((2 * FFT_K1, FFT_N1 // 2), lambda j, bi: (0, 0)),
            pl.BlockSpec((FFT_K1, 2 * FFT_N2, 2 * FFT_N2), const3, pipeline_mode=pl.Buffered(1)),
            pl.BlockSpec((FFT_K1, 2 * FFT_N2, 2 * FFT_N2), const3, pipeline_mode=pl.Buffered(1)),
            pl.BlockSpec((FFT_N1 // 2, 2 * FFT_K1), lambda j, bi: (0, 0)),
        ],
        out_specs=pl.BlockSpec((None, FFT_N2, n1, dt), lambda j, bi: (bi, 0, 0, j)),
        out_shape=jax.ShapeDtypeStruct((bsz, FFT_N2, n1, D_MODEL), BF16),
        scratch_shapes=[_stage_scratch(dt)],
        compiler_params=_cparams(("parallel", "arbitrary")),
        name="hyena_long_conv",
    )(z, g, fb, kf, f1_half, gm, gh, fi)


def _hyena_tail(tile, x_ref, z_ref, pm_ref, w_ref, b_ref):
    z = z_ref[...].reshape(x_ref.shape)
    a = jnp.dot(pm_ref[...], z, preferred_element_type=F32).astype(BF16)
    return x_ref[...] + jnp.dot(a, w_ref[...], preferred_element_type=F32) + b_ref[...]


def _hyena_tail_specs(tile):
    nb = FFN_TM // FFT_N2
    d = D_MODEL
    const = lambda s: (0, 0)
    return [
        pl.BlockSpec((None, FFT_N2, nb, d),
                     lambda s: (tile(s) // FFN_TILES_PER_SEQ, 0, tile(s) % FFN_TILES_PER_SEQ, 0)),
        pl.BlockSpec((FFN_TM, FFN_TM), const),
        pl.BlockSpec((d, d), const),
        pl.BlockSpec((1, d), const),
    ]


def _hyena_mixer(x, g_mix, w_in, b_in, cw, cb, fw1, fb1, fwi, fbi, fw3, freq, fbias, w_out, b_out, consts):
    f1_full, f1_half, gm, gh, fi, fq, dl2 = consts
    pc = _inproj_conv(x, g_mix[None], w_in.astype(BF16), b_in[None], cw, cb[None])
    h3 = _filter_mlp(fq, fw1, fb1, fwi, fbi, freq)
    kf = _filter_spectrum(h3, fw3, dl2, f1_full, gm)
    nt = D_MODEL // CONV_DT
    z1 = _long_conv_gate(pc, 2 * nt, pc, 0, fbias[0:1], kf, 0, f1_half, gm, gh, fi)
    z2 = _long_conv_gate(z1, 0, pc, nt, fbias[1:2], kf, nt, f1_half, gm, gh, fi)
    return (z2, _slab_perm(FFN_TM // FFT_N2).T, w_out.astype(BF16), b_out[None])


def _relbias_kernel(tab_ref, bucket_ref, band_ref, o_ref):
    bucket = bucket_ref[...]
    band = band_ref[...]
    for h in range(N_HEADS):
        acc = jnp.zeros(bucket.shape, F32)
        for k in range(REL_BUCKETS):
            acc = jnp.where(bucket == k, tab_ref[k, h], acc)
        o_ref[h] = jnp.where(band > 0, acc, -1e30)


def _rel_bias(rel_table):
    a = jnp.arange(BLOCK)[:, None]
    j = jnp.arange(3 * BLOCK)[None, :]
    rel = j - BLOCK - a
    half = REL_BUCKETS // 2
    exact = half // 2
    n = jnp.abs(rel)
    nf = jnp.maximum(n, 1).astype(F32)
    large = exact + (jnp.log(nf / exact) / math.log(REL_MAX_DIST / exact) * (half - exact)).astype(jnp.int32)
    large = jnp.minimum(large, half - 1)
    bucket = (jnp.where(rel > 0, half, 0) + jnp.where(n < exact, n, large)).astype(jnp.int32)
    band = (n <= WINDOW).astype(jnp.int32)
    return pl.pallas_call(
        _relbias_kernel,
        in_specs=[
            pl.BlockSpec(memory_space=pltpu.SMEM),
            pl.BlockSpec((BLOCK, 3 * BLOCK), lambda: (0, 0)),
            pl.BlockSpec((BLOCK, 3 * BLOCK), lambda: (0, 0)),
        ],
        out_specs=pl.BlockSpec((N_HEADS, BLOCK, 3 * BLOCK), lambda: (0, 0, 0)),
        out_shape=jax.ShapeDtypeStruct((N_HEADS, BLOCK, 3 * BLOCK), F32),
        name="rel_bias",
    )(rel_table, bucket, band)


QKV_CHUNK = 512


def _qkv_kernel(x_ref, g_ref, w_ref, gain_ref, flag_ref, bd_ref, o_ref):
    xn = _rms(x_ref[...], g_ref[...]).astype(BF16)
    for c in range(w_ref.shape[1] // QKV_CHUNK):
        cols = slice(c * QKV_CHUNK, (c + 1) * QKV_CHUNK)
        acc = jnp.dot(xn, w_ref[:, cols], preferred_element_type=F32)
        ssq = jnp.dot((acc * acc).astype(BF16), bd_ref[...], preferred_element_type=F32)
        normed = acc * lax.rsqrt(ssq * (1.0 / HEAD_DIM) + EPS) * gain_ref[:, cols]
        o_ref[:, cols] = jnp.where(flag_ref[:, cols] > 0, normed, acc).astype(o_ref.dtype)


def _qkv_proj(x2, g, w, gain, flag, tm=1024):
    m, k = x2.shape
    n = w.shape[1]
    tn = QKV_CHUNK
    eye = jnp.arange(tn)[:, None] // HEAD_DIM == jnp.arange(tn)[None, :] // HEAD_DIM
    const = lambda i: (0, 0)
    return pl.pallas_call(
        _qkv_kernel,
        grid=(m // tm,),
        in_specs=[
            pl.BlockSpec((tm, k), lambda i: (i, 0)),
            pl.BlockSpec((1, k), const),
            pl.BlockSpec((k, n), const),
            pl.BlockSpec((1, n), const),
            pl.BlockSpec((1, n), const),
            pl.BlockSpec((tn, tn), const),
        ],
        out_specs=pl.BlockSpec((tm, n), lambda i: (i, 0)),
        out_shape=jax.ShapeDtypeStruct((m, n), BF16),
        compiler_params=_cparams(("parallel",)),
        name="qkv_proj",
    )(x2, g, w, gain, flag, eye.astype(BF16))


ATTN_QBLOCKS = 2


def _attn_kernel(q_ref, kp_ref, kc_ref, kn_ref, bias0_ref, bias1_ref, sink_ref, o_ref):
    kvall = jnp.concatenate([kp_ref[...], kc_ref[...], kn_ref[...]], axis=0)
    lane = lax.broadcasted_iota(jnp.int32, (1, LANES), 1)
    keep = [(lane < HEAD_DIM).astype(F32).astype(BF16), (lane >= HEAD_DIM).astype(F32).astype(BF16)]
    ntile = N_KV // 2
    for qb, bias_ref in enumerate((bias0_ref, bias1_ref)):
        kv = kvall[qb * BLOCK:(qb + 3) * BLOCK]
        rows = slice(qb * BLOCK, (qb + 1) * BLOCK)
        for j in range(ntile):
            kt = kv[:, j * LANES:(j + 1) * LANES]
            vt = kv[:, (ntile + j) * LANES:(ntile + j + 1) * LANES]
            tiles = [GROUP * j + g for g in range(GROUP)]
            qs = jnp.concatenate([q_ref[rows, t * LANES:(t + 1) * LANES] for t in tiles], axis=0)
            acc = None
            for half in range(2):
                kh = 2 * j + half
                lg = lax.dot_general(kt * keep[half], qs, (((1,), (1,)), ((), ())),
                                     preferred_element_type=F32)
                lg = lg + bias_ref[kh]
                sink = sink_ref[kh]
                m = jnp.maximum(jnp.max(lg, axis=0, keepdims=True), sink)
                p = jnp.exp2(lg - m)
                den = jnp.sum(p, axis=0, keepdims=True) + jnp.exp2(sink - m)
                o = lax.dot_general(vt * keep[half], p.astype(BF16), (((0,), (0,)), ((), ())),
                                    preferred_element_type=F32) * (1.0 / den)
                acc = o if acc is None else acc + o
            for g, t in enumerate(tiles):
                o_ref[qb, t * LANES:(t + 1) * LANES, :] = (
                    acc[:, g * BLOCK:(g + 1) * BLOCK].astype(o_ref.dtype))


def _attention(qkv, bias_t, sink_row):
    bsz, seq, _ = qkv.shape
    nb = seq // BLOCK
    dq = N_HEADS * HEAD_DIM
    dkv = 2 * N_KV * HEAD_DIM
    kvb = dq // dkv
    qb = ATTN_QBLOCKS
    steps = nb // qb
    bias_spec = lambda variant: pl.BlockSpec((None, N_KV, 3 * BLOCK, GROUP * BLOCK),
                                             lambda b, n: (variant(n), 0, 0, 0))
    return pl.pallas_call(
        _attn_kernel,
        grid=(bsz, steps),
        in_specs=[
            pl.BlockSpec((None, qb * BLOCK, dq), lambda b, n: (b, n, 0)),
            pl.BlockSpec((None, BLOCK, dkv), lambda b, n: (b, jnp.maximum(qb * n - 1, 0), kvb)),
            pl.BlockSpec((None, qb * BLOCK, dkv), lambda b, n: (b, n, kvb)),
            pl.BlockSpec((None, BLOCK, dkv), lambda b, n: (b, jnp.minimum(qb * n + qb, nb - 1), kvb)),
            bias_spec(lambda n: jnp.where(n == 0, 0, 1)),
            bias_spec(lambda n: jnp.where(n == steps - 1, 2, 1)),
            pl.BlockSpec((N_KV, 1, GROUP * BLOCK), lambda b, n: (0, 0, 0)),
        ],
        out_specs=pl.BlockSpec((None, qb, dq, BLOCK), lambda b, n: (b, n, 0, 0)),
        out_shape=jax.ShapeDtypeStruct((bsz, nb, dq, BLOCK), BF16),
        compiler_params=_cparams(("parallel", "parallel")),
        name="window_attention",
    )(qkv, qkv, qkv, qkv, bias_t, bias_t, sink_row)


def _attn_tail(tile, x_ref, ot_ref, w_ref):
    ys = [lax.dot_general(ot_ref[k], w_ref[...], (((0,), (0,)), ((), ())), preferred_element_type=F32)
          for k in range(FFN_TM // BLOCK)]
    return x_ref[...] + jnp.concatenate(ys, axis=0)


def _attn_tail_specs(tile):
    dq = N_HEADS * HEAD_DIM
    return [
        pl.BlockSpec((FFN_TM // BLOCK, dq, BLOCK), lambda s: (tile(s), 0, 0)),
        pl.BlockSpec((dq, D_MODEL), lambda s: (0, 0)),
    ]


def _head_tile_perm():
    cols = []
    for tile in range(N_HEADS // 2):
        j, g = tile // GROUP, tile % GROUP
        for half in range(2):
            h = (2 * j + half) * GROUP + g
            cols.extend(range(h * HEAD_DIM, (h + 1) * HEAD_DIM))
    return jnp.asarray(cols, jnp.int32)


def _attn_mixer(x, g_mix, w_qkv, q_gain, k_gain, sink, w_o, rel_table):
    bsz, seq, d = x.shape
    x2 = x.reshape(bsz * seq, d)
    dq = N_HEADS * HEAD_DIM
    dk = N_KV * HEAD_DIM
    perm = _head_tile_perm()
    w = jnp.concatenate([w_qkv[:, :dq][:, perm], w_qkv[:, dq:]], axis=1).astype(BF16)
    log2e = math.log2(math.e)
    gain = jnp.concatenate([jnp.tile(q_gain * (HEAD_DIM ** -0.5 * log2e), N_HEADS), jnp.tile(k_gain, N_KV),
                            jnp.ones((dk,), F32)])[None]
    flag = jnp.concatenate([jnp.ones((dq + dk,), F32), jnp.zeros((dk,), F32)])[None]
    qkv = _qkv_proj(x2, g_mix[None], w, gain, flag)
    bias_t = _rel_bias(rel_table).reshape(N_KV, GROUP * BLOCK, 3 * BLOCK).transpose(0, 2, 1)
    key = jnp.arange(3 * BLOCK)[None, :, None]
    bias_t = jnp.stack([jnp.where(key < BLOCK, -1e30, bias_t), bias_t,
                        jnp.where(key >= 2 * BLOCK, -1e30, bias_t)]) * log2e
    sink_row = jnp.repeat(sink * log2e, BLOCK).reshape(N_KV, 1, GROUP * BLOCK)
    o_t = _attention(qkv.reshape(bsz, seq, dq + 2 * dk), bias_t, sink_row)
    return (o_t.reshape(bsz * (seq // BLOCK), dq, BLOCK), w_o[perm, :].astype(BF16))


def _pool_tail(tile, xc_ref, xp_ref, xn_ref, g_ref, w_ref, b_ref, sc_ref):
    tm = FFN_TM
    g = g_ref[...]
    rows = tm + 2 * SUBLANES
    t0 = (tile % FFN_TILES_PER_SEQ) * tm
    t = t0 - SUBLANES + lax.broadcasted_iota(jnp.int32, (rows, 1), 0)
    inside = (t >= 0) & (t < SEQ)
    xa = jnp.concatenate([_rms(xp_ref[...], g), _rms(xc_ref[...], g), _rms(xn_ref[...], g)], axis=0)
    xa = jnp.where(inside, xa, 0.0)
    tc = t[SUBLANES:SUBLANES + tm]
    outs = []
    for gi, wdt in enumerate(POOL_WINDOWS):
        r = wdt // 2
        xg = xa[:, gi * POOL_GROUP:(gi + 1) * POOL_GROUP]
        run = xg
        span = 1
        while span < 2 * r:
            run = run + pltpu.roll(run, rows - span, 0)
            span *= 2
        win = (pltpu.roll(run, r, 0) + pltpu.roll(xg, rows - r, 0))[SUBLANES:SUBLANES + tm]
        cnt = (jnp.minimum(tc + r + 1, SEQ) - jnp.maximum(tc - r, 0)).astype(F32)
        dlt = win * (1.0 / cnt) - xg[SUBLANES:SUBLANES + tm]
        outs.append(jnp.dot(dlt.astype(BF16), w_ref[gi], preferred_element_type=F32))
    y = (jnp.concatenate(outs, axis=-1) + b_ref[...]) * sc_ref[...]
    return xc_ref[...] + y


def _pool_tail_specs(tile, rows):
    d = D_MODEL
    hb = FFN_TM // SUBLANES
    last = rows // SUBLANES - 1
    const = lambda s: (0, 0)
    return [
        pl.BlockSpec((SUBLANES, d), lambda s: (jnp.maximum(tile(s) * hb - 1, 0), 0)),
        pl.BlockSpec((SUBLANES, d), lambda s: (jnp.minimum((tile(s) + 1) * hb, last), 0)),
        pl.BlockSpec((1, d), const),
        pl.BlockSpec((len(POOL_WINDOWS), POOL_GROUP, POOL_GROUP), lambda s: (0, 0, 0)),
        pl.BlockSpec((1, d), const),
        pl.BlockSpec((1, d), const),
    ]


def kernel(x, norm_mix, norm_ffn, hy_w_in, hy_b_in, hy_conv_w, hy_conv_b, hy_f_w1, hy_f_b1, hy_f_wi, hy_f_bi, hy_f_w3, hy_f_freq, hy_f_bias, hy_w_out, hy_b_out, at_w_qkv, at_q_gain, at_k_gain, at_sink, at_w_o, rel_table, pl_w, pl_b, pl_scale, ff_w_gate, ff_w_up, ff_w_down):
    bsz, seq, d = x.shape
    f1, gm = _dft_mats()
    f1_full = f1.astype(BF16)
    f1_half = f1[:, :FFT_N1 // 2].astype(BF16)
    fi = (f1[:, :FFT_N1 // 2].T * (2.0 / FFT_N)).astype(BF16)
    gh = jnp.swapaxes(gm, 1, 2).astype(BF16)
    gm = gm.astype(BF16)
    fq = jnp.linspace(1e-4, HY_BANDS - 1, HY_BANDS, dtype=F32)[None, :]
    deltas = jnp.abs(jnp.linspace(HY_MIN_DECAY, HY_MAX_DECAY, D_MODEL, dtype=F32))
    dl2 = jnp.concatenate([deltas, deltas])[None, :]
    consts = (f1_full, f1_half, gm, gh, fi, fq, dl2)

    x2 = x.reshape(bsz * seq, d)
    for i in range(DEPTH):
        kind, s = i % 3, i // 3
        x = x2.reshape(bsz, seq, d)
        if kind == 0:
            args = _hyena_mixer(x, norm_mix[i], hy_w_in[s], hy_b_in[s], hy_conv_w[s], hy_conv_b[s],
                                hy_f_w1[s], hy_f_b1[s], hy_f_wi[s], hy_f_bi[s], hy_f_w3[s],
                                hy_f_freq[s], hy_f_bias[s], hy_w_out[s], hy_b_out[s], consts)
            tail, specs = _hyena_tail, _hyena_tail_specs
        elif kind == 1:
            args = _attn_mixer(x, norm_mix[i], at_w_qkv[s], at_q_gain[s], at_k_gain[s], at_sink[s],
                               at_w_o[s], rel_table)
            tail, specs = _attn_tail, _attn_tail_specs
        else:
            args = (x2, x2, norm_mix[i][None], pl_w[s].astype(BF16), pl_b[s][None], pl_scale[s][None])
            tail, specs = _pool_tail, functools.partial(_pool_tail_specs, rows=bsz * seq)
        x2 = _ffn(x2, norm_ffn[i][None], ff_w_gate, ff_w_up, ff_w_down, i, tail, args, specs)
    return x2.reshape(bsz, seq, d)
```

```python
import functools
import math

import jax
import jax.numpy as jnp
from jax import lax
from jax.experimental import pallas as pl
from jax.experimental.pallas import tpu as pltpu

F32 = jnp.float32
BF16 = jnp.bfloat16

D_MODEL = 1024
BATCH = 4
SEQ = 8192
DEPTH = 4
EPS = 1e-6

HY_SHORT = 3
HY_EMB = 33
HY_BANDS = 16
HY_FW = 64
HY_INNER = 2
HY_MIN_DECAY = math.log(1e-2) / 1.5
HY_MAX_DECAY = math.log(1e-2) / 0.3

HEAD_DIM = 64
N_HEADS = 16
N_KV = 4
GROUP = 4
WINDOW = 128
BLOCK = 128
REL_BUCKETS = 32
REL_MAX_DIST = 128

POOL_WINDOWS = (2, 4, 8, 16)
POOL_GROUP = 256
D_FF = 2816

SUBLANES = 8
LANES = 128
VMEM_LIMIT = 56 * 1024 * 1024

FFT_N = 2 * SEQ
FFT_N1 = 256
FFT_N2 = 64
FFT_K1 = FFT_N1 // 2
SLAB = 2 * FFT_K1 + SUBLANES
CONV_DT = 128
DFT_UNROLL = 32


def _cparams(sem):
    return pltpu.CompilerParams(dimension_semantics=sem, vmem_limit_bytes=VMEM_LIMIT)


def _rms(x, g):
    return x * lax.rsqrt(jnp.mean(x * x, axis=-1, keepdims=True) + EPS) * g


FFN_CHUNK = 256


FFN_CHUNKS = D_FF // FFN_CHUNK


FFN_TM = 512
FFN_TILES_PER_SEQ = SEQ // FFN_TM


def _ffn_kernel(*refs, tail, n_tail):
    x_ref, g_ref, wg_ref, wu_ref, wd_ref = refs[:5]
    tail_refs = refs[5:5 + n_tail]
    o_ref, wgb_ref, wub_ref, wdb_ref, h_ref = refs[5 + n_tail:]
    s = pl.program_id(0)

    @pl.when(s < FFN_CHUNKS)
    def _():
        wgb_ref[s] = wg_ref[...].astype(BF16)
        wub_ref[s] = wu_ref[...].astype(BF16)
        wdb_ref[pl.ds(pl.multiple_of(s * FFN_CHUNK, FFN_CHUNK), FFN_CHUNK), :] = wd_ref[...].astype(BF16)

    @pl.when(s >= FFN_CHUNKS)
    def _():
        x = tail(s - FFN_CHUNKS, x_ref, *tail_refs)
        xn = _rms(x, g_ref[...]).astype(BF16)
        for c in range(FFN_CHUNKS):
            gate = jnp.dot(xn, wgb_ref[c], preferred_element_type=F32)
            up = jnp.dot(xn, wub_ref[c], preferred_element_type=F32)
            h_ref[:, c * FFN_CHUNK:(c + 1) * FFN_CHUNK] = (gate * jax.nn.sigmoid(gate) * up).astype(BF16)
        o_ref[...] = x + jnp.dot(h_ref[...], wdb_ref[...], preferred_element_type=F32)


def _ffn(x2, g, wg, wu, wd, layer, tail, tail_args, tail_specs):
    m, d = x2.shape
    f = wg.shape[2]
    tm = FFN_TM
    tile = lambda s: jnp.maximum(s - FFN_CHUNKS, 0)
    chunk = lambda s: jnp.minimum(s, FFN_CHUNKS - 1)
    return pl.pallas_call(
        functools.partial(_ffn_kernel, tail=tail, n_tail=len(tail_args)),
        grid=(FFN_CHUNKS + m // tm,),
        in_specs=[
            pl.BlockSpec((tm, d), lambda s: (tile(s), 0)),
            pl.BlockSpec((1, d), lambda s: (0, 0)),
            pl.BlockSpec((None, d, FFN_CHUNK), lambda s: (layer, 0, chunk(s))),
            pl.BlockSpec((None, d, FFN_CHUNK), lambda s: (layer, 0, chunk(s))),
            pl.BlockSpec((None, FFN_CHUNK, d), lambda s: (layer, chunk(s), 0)),
        ] + tail_specs(tile),
        out_specs=pl.BlockSpec((tm, d), lambda s: (tile(s), 0)),
        out_shape=jax.ShapeDtypeStruct((m, d), F32),
        scratch_shapes=[pltpu.VMEM((FFN_CHUNKS, d, FFN_CHUNK), BF16),
                        pltpu.VMEM((FFN_CHUNKS, d, FFN_CHUNK), BF16),
                        pltpu.VMEM((f, d), BF16),
                        pltpu.VMEM((tm, f), BF16)],
        compiler_params=_cparams(("arbitrary",)),
        name="mixer_tail_ffn",
    )(x2, g, wg, wu, wd, *tail_args)


INPROJ_N1 = 16
INPROJ_CHUNK = 512


def _slab_perm(nb):
    r = jnp.arange(FFT_N2 * nb)
    src = FFT_N2 * (r % nb) + r // nb
    return (src[:, None] == r[None, :]).astype(BF16)


def _inproj_kernel(xp_ref, xc_ref, xn_ref, g_ref, pm_ref, w_ref, b_ref, cw_ref, cb_ref, o_ref, xs_ref):
    i = pl.program_id(1)
    nb = INPROJ_N1
    tm = FFT_N2 * nb
    g = g_ref[...]
    xn = _rms(xc_ref[...], g).astype(BF16)
    xs_ref[:tm, :] = jnp.dot(pm_ref[...], xn, preferred_element_type=F32).astype(BF16)
    halo = jnp.concatenate([xp_ref[...], xn_ref[...]], axis=0)
    xs_ref[tm:, :] = _rms(halo, g).astype(BF16)

    xs = xs_ref[...]
    row = lax.broadcasted_iota(jnp.int32, (nb, 1), 0)
    inside_before = i > 0
    inside_after = i < pl.num_programs(1) - 1
    for c in range(w_ref.shape[1] // INPROJ_CHUNK):
        cols = slice(c * INPROJ_CHUNK, (c + 1) * INPROJ_CHUNK)
        p = jnp.dot(xs, w_ref[:, cols], preferred_element_type=F32) + b_ref[:, cols]
        cur = p[:tm]
        before = jnp.where(inside_before, p[tm + SUBLANES - 1:tm + SUBLANES], 0.0)
        after = jnp.where(inside_after, p[tm + SUBLANES:tm + SUBLANES + 1], 0.0)
        first = jnp.where(row == 0, before, pltpu.roll(cur[tm - nb:], 1, 0))
        last = jnp.where(row == nb - 1, after, pltpu.roll(cur[:nb], nb - 1, 0))
        dn = jnp.concatenate([first, cur[:tm - nb]], axis=0)
        up = jnp.concatenate([cur[nb:], last], axis=0)
        cw = cw_ref[:, cols]
        out = cb_ref[:, cols] + cw[0:1] * dn + cw[1:2] * cur + cw[2:3] * up
        o_ref[:, :, cols] = out.reshape(FFT_N2, nb, INPROJ_CHUNK).astype(o_ref.dtype)


def _inproj_conv(x, g, w, b, cw, cb):
    bsz, seq, d = x.shape
    n = w.shape[1]
    nb = INPROJ_N1
    tm = FFT_N2 * nb
    hb = tm // SUBLANES
    last = seq // SUBLANES - 1
    const = lambda bi, i: (0, 0)
    return pl.pallas_call(
        _inproj_kernel,
        grid=(bsz, seq // tm),
        in_specs=[
            pl.BlockSpec((None, SUBLANES, d), lambda bi, i: (bi, jnp.maximum(i * hb - 1, 0), 0)),
            pl.BlockSpec((None, tm, d), lambda bi, i: (bi, i, 0)),
            pl.BlockSpec((None, SUBLANES, d), lambda bi, i: (bi, jnp.minimum((i + 1) * hb, last), 0)),
            pl.BlockSpec((1, d), const),
            pl.BlockSpec((tm, tm), const, pipeline_mode=pl.Buffered(1)),
            pl.BlockSpec((d, n), const, pipeline_mode=pl.Buffered(1)),
            pl.BlockSpec((1, n), const),
            pl.BlockSpec((HY_SHORT, n), const),
            pl.BlockSpec((1, n), const),
        ],
        out_specs=pl.BlockSpec((None, FFT_N2, nb, n), lambda bi, i: (bi, 0, i, 0)),
        out_shape=jax.ShapeDtypeStruct((bsz, FFT_N2, seq // FFT_N2, n), BF16),
        scratch_shapes=[pltpu.VMEM((tm + 2 * SUBLANES, d), BF16)],
        compiler_params=_cparams(("parallel", "parallel")),
        name="hyena_inproj",
    )(x, x, x, g, _slab_perm(nb), w, b, cw, cb)


def _dft_mats():
    i32 = jnp.int32
    k1 = jnp.arange(FFT_K1, dtype=i32)[:, None]
    n1 = jnp.arange(FFT_N1, dtype=i32)[None, :]
    m = ((2 * k1 + 1) * n1) % (2 * FFT_N1)
    ang = m.astype(F32) * (math.pi / FFT_N1)
    f1 = jnp.concatenate([jnp.cos(ang), -jnp.sin(ang)], axis=0)
    k1 = jnp.arange(FFT_K1, dtype=i32)[:, None, None]
    k2 = jnp.arange(FFT_N2, dtype=i32)[None, :, None]
    n2 = jnp.arange(FFT_N2, dtype=i32)[None, None, :]
    m = ((2 * (k1 + FFT_N1 * k2) + 1) * n2) % (2 * FFT_N)
    ang = m.astype(F32) * (math.pi / FFT_N)
    c, s = jnp.cos(ang), jnp.sin(ang)
    gm = jnp.concatenate(
        [jnp.concatenate([c, s], axis=2), jnp.concatenate([-s, c], axis=2)], axis=1)
    return f1, gm


def _dot3(a3, b):
    hi, lo = _split_bf16(b)
    return jnp.dot(a3, jnp.concatenate([hi, hi, lo], axis=0), preferred_element_type=F32)


def _stack3_cols(a):
    hi, lo = _split_bf16(a)
    return jnp.concatenate([hi, lo, hi], axis=-1)


def _tap_lag(n):
    return jnp.where(n >= SEQ, FFT_N - n, n).astype(F32)


def _split_bf16(a):
    hi = a.astype(BF16)
    return hi, (a - hi.astype(F32)).astype(BF16)


def _filter_mlp_kernel(fq_ref, w1t_ref, w1b_ref, b1_ref, wi_ref, bi_ref, fr_ref, h3_ref, *, slabs):
    cols = slabs * FFT_N1
    c = lax.broadcasted_iota(jnp.int32, (1, cols), 1)
    n2 = pl.program_id(0) * slabs + c // FFT_N1
    pos = _tap_lag(FFT_N2 * (c % FFT_N1) + n2)
    t = pos * (1.0 / (SEQ - 1))
    ang = (2.0 * math.pi / SEQ) * pos * fq_ref[...]
    fr = fr_ref[...]
    bands = jnp.concatenate([jnp.cos(ang), -jnp.sin(ang)], axis=0)
    pre = w1t_ref[...] * t + _dot3(w1b_ref[...], bands)
    h = jnp.sin(fr * (pre + b1_ref[...]))
    for l in range(HY_INNER):
        h = jnp.sin(fr * (_dot3(wi_ref[l], h) + bi_ref[l]))
    hi, lo = _split_bf16(h)
    h3 = jnp.concatenate([hi, lo, hi], axis=0)
    for k in range(slabs):
        h3_ref[k] = h3[:, k * FFT_N1:(k + 1) * FFT_N1]


def _filter_mlp(fq, w1, b1, wi, bi, fr, slabs=8):
    w1t, w1b = w1[0:1].T, _stack3_cols(w1[1:].T)
    wi3 = _stack3_cols(jnp.swapaxes(wi, 1, 2))
    col = lambda a: a.reshape(-1, 1)
    const2 = lambda i: (0, 0)
    const3 = lambda i: (0, 0, 0)
    return pl.pallas_call(
        functools.partial(_filter_mlp_kernel, slabs=slabs),
        grid=(FFT_N2 // slabs,),
        in_specs=[
            pl.BlockSpec((HY_BANDS, 1), const2),
            pl.BlockSpec((HY_FW, 1), const2),
            pl.BlockSpec((HY_FW, 3 * 2 * HY_BANDS), const2),
            pl.BlockSpec((HY_FW, 1), const2),
            pl.BlockSpec((HY_INNER, HY_FW, 3 * HY_FW), const3),
            pl.BlockSpec((HY_INNER, HY_FW, 1), const3),
            pl.BlockSpec((HY_FW, 1), const2),
        ],
        out_specs=pl.BlockSpec((slabs, 3 * HY_FW, FFT_N1), lambda i: (i, 0, 0)),
        out_shape=jax.ShapeDtypeStruct((FFT_N2, 3 * HY_FW, FFT_N1), BF16),
        compiler_params=_cparams(("parallel",)),
        name="hyena_filter_mlp",
    )(col(fq), w1t, w1b, col(b1), wi3, bi[:, :, None], col(fr))


def _slab_store(s_ref, n2, a):
    rows = pl.ds(pl.multiple_of(n2 * SLAB, SUBLANES), 2 * FFT_K1)
    for l in range(s_ref.shape[0]):
        s_ref[l, rows, :] = a[:, l * LANES:(l + 1) * LANES]


def _slab_load(s_ref, n2):
    rows = pl.ds(pl.multiple_of(n2 * SLAB, SUBLANES), 2 * FFT_K1)
    return jnp.concatenate([s_ref[l, rows, :] for l in range(s_ref.shape[0])], axis=1).astype(BF16)


def _k1_load(s_ref, k1):
    re = pl.ds(k1, FFT_N2, stride=SLAB)
    im = pl.ds(FFT_K1 + k1, FFT_N2, stride=SLAB)
    tiles = [jnp.concatenate([s_ref[l, re, :], s_ref[l, im, :]], axis=0) for l in range(s_ref.shape[0])]
    return jnp.concatenate(tiles, axis=1).astype(BF16)


def _k1_store(s_ref, k1, b):
    for l in range(s_ref.shape[0]):
        cols = slice(l * LANES, (l + 1) * LANES)
        s_ref[l, pl.ds(k1, FFT_N2, stride=SLAB), :] = b[:FFT_N2, cols]
        s_ref[l, pl.ds(FFT_K1 + k1, FFT_N2, stride=SLAB), :] = b[FFT_N2:, cols]


def _stage_scratch(dt):
    return pltpu.VMEM((dt // LANES, FFT_N2 * SLAB, LANES), F32)


def _stage1(src_ref, f1_ref, s_ref):
    def body(n2, c):
        _slab_store(s_ref, n2, jnp.dot(f1_ref[...], src_ref[n2].astype(BF16), preferred_element_type=F32))
        return c
    lax.fori_loop(0, FFT_N2, body, 0, unroll=DFT_UNROLL)


def _filt_dft_kernel(h3_ref, w3_ref, dl_ref, f1_ref, gm_ref, kf_ref, s_ref):
    half = FFT_N1 // 2
    dt = dl_ref.shape[1]
    n1 = lax.broadcasted_iota(jnp.int32, (FFT_N1, 1), 0)
    dl = dl_ref[...]

    def body(n2, l1):
        both = lax.dot_general(h3_ref[n2], w3_ref[...], (((0,), (0,)), ((), ())),
                               preferred_element_type=F32)
        fwd = both[:half, :dt]
        bwd = both[half:, dt:]
        n = FFT_N2 * n1 + n2
        t = _tap_lag(n) * (1.0 / (SEQ - 1))
        taps = jnp.concatenate([fwd, -bwd], axis=0) * jnp.exp(-t * dl)
        taps = jnp.where(n == SEQ, 0.0, taps)
        _slab_store(s_ref, n2, jnp.dot(f1_ref[...], taps.astype(BF16), preferred_element_type=F32))
        return l1 + jnp.sum(jnp.abs(taps), axis=0, keepdims=True)
    l1 = lax.fori_loop(0, FFT_N2, body, jnp.zeros(dl.shape, F32), unroll=DFT_UNROLL)
    inv = 1.0 / l1

    def stage2(k1, c):
        x = jnp.dot(gm_ref[k1], _k1_load(s_ref, k1), preferred_element_type=F32)
        rows = pl.ds(pl.multiple_of(k1 * 2 * FFT_N2, 2 * FFT_N2), 2 * FFT_N2)
        kf_ref[rows, :] = (x * inv).astype(kf_ref.dtype)
        return c
    lax.fori_loop(0, FFT_K1, stage2, 0, unroll=DFT_UNROLL)


def _filter_spectrum(h3, w3, dl2, f1_full, gm):
    ncol = 2 * D_MODEL
    dt = CONV_DT
    nct = ncol // dt
    tiles = lambda a: a.reshape(HY_FW, 2, nct, dt).transpose(2, 0, 1, 3).reshape(nct, HY_FW, 2 * dt)
    w3h, w3l = _split_bf16(w3)
    w3cat = jnp.concatenate([tiles(w3h), tiles(w3h), tiles(w3l)], axis=1)
    return pl.pallas_call(
        _filt_dft_kernel,
        grid=(nct,),
        in_specs=[
            pl.BlockSpec((FFT_N2, 3 * HY_FW, FFT_N1), lambda j: (0, 0, 0)),
            pl.BlockSpec((None, 3 * HY_FW, 2 * dt), lambda j: (j, 0, 0)),
            pl.BlockSpec((1, dt), lambda j: (0, j)),
            pl.BlockSpec((2 * FFT_K1, FFT_N1), lambda j: (0, 0)),
            pl.BlockSpec((FFT_K1, 2 * FFT_N2, 2 * FFT_N2), lambda j: (0, 0, 0),
                         pipeline_mode=pl.Buffered(1)),
        ],
        out_specs=pl.BlockSpec((FFT_K1 * 2 * FFT_N2, dt), lambda j: (0, j)),
        out_shape=jax.ShapeDtypeStruct((FFT_K1 * 2 * FFT_N2, ncol), F32),
        scratch_shapes=[_stage_scratch(dt)],
        compiler_params=_cparams(("parallel",)),
        name="hyena_filter_spectrum",
    )(h3, w3cat, dl2, f1_full, gm)


def _longconv_kernel(z_ref, g_ref, fb_ref, kf_ref, f1_ref, gm_ref, gh_ref, fi_ref, o_ref, s_ref):
    _stage1(z_ref, f1_ref, s_ref)

    def mid(k1, c):
        x = jnp.dot(gm_ref[k1], _k1_load(s_ref, k1), preferred_element_type=F32)
        kf = kf_ref[pl.ds(pl.multiple_of(k1 * 2 * FFT_N2, 2 * FFT_N2), 2 * FFT_N2), :].astype(F32)
        xr, xi = x[:FFT_N2], x[FFT_N2:]
        kr, ki = kf[:FFT_N2], kf[FFT_N2:]
        y = jnp.concatenate([xr * kr - xi * ki, xr * ki + xi * kr], axis=0).astype(BF16)
        b = jnp.dot(gh_ref[k1], y, preferred_element_type=F32)
        _k1_store(s_ref, k1, b)
        return c
    lax.fori_loop(0, FFT_K1, mid, 0, unroll=2 * DFT_UNROLL)

    fb = fb_ref[...]

    def last(n2, c):
        y = jnp.dot(fi_ref[...], _slab_load(s_ref, n2), preferred_element_type=F32)
        o_ref[n2] = (g_ref[n2] * (y + fb * z_ref[n2])).astype(o_ref.dtype)
        return c
    lax.fori_loop(0, FFT_N2, last, 0, unroll=DFT_UNROLL)


def _long_conv_gate(z, zcol, g, gcol, fb, kf, kcol, f1_half, gm, gh, fi):
    bsz = z.shape[0]
    dt = CONV_DT
    nt = D_MODEL // dt
    n1 = FFT_N1 // 2
    const3 = lambda j, bi: (0, 0, 0)
    return pl.pallas_call(
        _longconv_kernel,
        grid=(nt, bsz),
        in_specs=[
            pl.BlockSpec((None, FFT_N2, n1, dt), lambda j, bi: (bi, 0, 0, zcol + j)),
            pl.BlockSpec((None, FFT_N2, n1, dt), lambda j, bi: (bi, 0, 0, gcol + j)),
            pl.BlockSpec((1, dt), lambda j, bi: (0, j)),
            pl.BlockSpec((FFT_K1 * 2 * FFT_N2, dt), lambda j, bi: (0, kcol + j)),
            pl.BlockSpec((2 * FFT_K1, FFT_N1 // 2), lambda j, bi: (0, 0)),
            pl.BlockSpec((FFT_K1, 2 * FFT_N2, 2 * FFT_N2), const3, pipeline_mode=pl.Buffered(1)),
            pl.BlockSpec((FFT_K1, 2 * FFT_N2, 2 * FFT_N2), const3, pipeline_mode=pl.Buffered(1)),
            pl.BlockSpec((FFT_N1 // 2, 2 * FFT_K1), lambda j, bi: (0, 0)),
        ],
        out_specs=pl.BlockSpec((None, FFT_N2, n1, dt), lambda j, bi: (bi, 0, 0, j)),
        out_shape=jax.ShapeDtypeStruct((bsz, FFT_N2, n1, D_MODEL), BF16),
        scratch_shapes=[_stage_scratch(dt)],
        compiler_params=_cparams(("parallel", "arbitrary")),
        name="hyena_long_conv",
    )(z, g, fb, kf, f1_half, gm, gh, fi)


def _hyena_tail(tile, x_ref, z_ref, pm_ref, w_ref, b_ref):
    z = z_ref[...].reshape(x_ref.shape)
    a = jnp.dot(pm_ref[...], z, preferred_element_type=F32).astype(BF16)
    return x_ref[...] + jnp.dot(a, w_ref[...], preferred_element_type=F32) + b_ref[...]


def _hyena_tail_specs(tile):
    nb = FFN_TM // FFT_N2
    d = D_MODEL
    const = lambda s: (0, 0)
    return [
        pl.BlockSpec((None, FFT_N2, nb, d),
                     lambda s: (tile(s) // FFN_TILES_PER_SEQ, 0, tile(s) % FFN_TILES_PER_SEQ, 0)),
        pl.BlockSpec((FFN_TM, FFN_TM), const),
        pl.BlockSpec((d, d), const),
        pl.BlockSpec((1, d), const),
    ]


def _hyena_mixer(x, g_mix, w_in, b_in, cw, cb, fw1, fb1, fwi, fbi, fw3, freq, fbias, w_out, b_out, consts):
    f1_full, f1_half, gm, gh, fi, fq, dl2 = consts
    pc = _inproj_conv(x, g_mix[None], w_in.astype(BF16), b_in[None], cw, cb[None])
    h3 = _filter_mlp(fq, fw1, fb1, fwi, fbi, freq)
    kf = _filter_spectrum(h3, fw3, dl2, f1_full, gm)
    nt = D_MODEL // CONV_DT
    z1 = _long_conv_gate(pc, 2 * nt, pc, 0, fbias[0:1], kf, 0, f1_half, gm, gh, fi)
    z2 = _long_conv_gate(z1, 0, pc, nt, fbias[1:2], kf, nt, f1_half, gm, gh, fi)
    return (z2, _slab_perm(FFN_TM // FFT_N2).T, w_out.astype(BF16), b_out[None])


def _relbias_kernel(tab_ref, bucket_ref, band_ref, o_ref):
    bucket = bucket_ref[...]
    band = band_ref[...]
    for h in range(N_HEADS):
        acc = jnp.zeros(bucket.shape, F32)
        for k in range(REL_BUCKETS):
            acc = jnp.where(bucket == k, tab_ref[k, h], acc)
        o_ref[h] = jnp.where(band > 0, acc, -1e30)


def _rel_bias(rel_table):
    a = jnp.arange(BLOCK)[:, None]
    j = jnp.arange(3 * BLOCK)[None, :]
    rel = j - BLOCK - a
    half = REL_BUCKETS // 2
    exact = half // 2
    n = jnp.abs(rel)
    nf = jnp.maximum(n, 1).astype(F32)
    large = exact + (jnp.log(nf / exact) / math.log(REL_MAX_DIST / exact) * (half - exact)).astype(jnp.int32)
    large = jnp.minimum(large, half - 1)
    bucket = (jnp.where(rel > 0, half, 0) + jnp.where(n < exact, n, large)).astype(jnp.int32)
    band = (n <= WINDOW).astype(jnp.int32)
    return pl.pallas_call(
        _relbias_kernel,
        in_specs=[
            pl.BlockSpec(memory_space=pltpu.SMEM),
            pl.BlockSpec((BLOCK, 3 * BLOCK), lambda: (0, 0)),
            pl.BlockSpec((BLOCK, 3 * BLOCK), lambda: (0, 0)),
        ],
        out_specs=pl.BlockSpec((N_HEADS, BLOCK, 3 * BLOCK), lambda: (0, 0, 0)),
        out_shape=jax.ShapeDtypeStruct((N_HEADS, BLOCK, 3 * BLOCK), F32),
        name="rel_bias",
    )(rel_table, bucket, band)


QKV_CHUNK = 512


def _qkv_kernel(x_ref, g_ref, w_ref, gain_ref, flag_ref, bd_ref, o_ref):
    xn = _rms(x_ref[...], g_ref[...]).astype(BF16)
    for c in range(w_ref.shape[1] // QKV_CHUNK):
        cols = slice(c * QKV_CHUNK, (c + 1) * QKV_CHUNK)
        acc = jnp.dot(xn, w_ref[:, cols], preferred_element_type=F32)
        ssq = jnp.dot((acc * acc).astype(BF16), bd_ref[...], preferred_element_type=F32)
        normed = acc * lax.rsqrt(ssq * (1.0 / HEAD_DIM) + EPS) * gain_ref[:, cols]
        o_ref[:, cols] = jnp.where(flag_ref[:, cols] > 0, normed, acc).astype(o_ref.dtype)


def _qkv_proj(x2, g, w, gain, flag, tm=1024):
    m, k = x2.shape
    n = w.shape[1]
    tn = QKV_CHUNK
    eye = jnp.arange(tn)[:, None] // HEAD_DIM == jnp.arange(tn)[None, :] // HEAD_DIM
    const = lambda i: (0, 0)
    return pl.pallas_call(
        _qkv_kernel,
        grid=(m // tm,),
        in_specs=[
            pl.BlockSpec((tm, k), lambda i: (i, 0)),
            pl.BlockSpec((1, k), const),
            pl.BlockSpec((k, n), const),
            pl.BlockSpec((1, n), const),
            pl.BlockSpec((1, n), const),
            pl.BlockSpec((tn, tn), const),
        ],
        out_specs=pl.BlockSpec((tm, n), lambda i: (i, 0)),
        out_shape=jax.ShapeDtypeStruct((m, n), BF16),
        compiler_params=_cparams(("parallel",)),
        name="qkv_proj",
    )(x2, g, w, gain, flag, eye.astype(BF16))


ATTN_QBLOCKS = 2


def _attn_kernel(q_ref, kp_ref, kc_ref, kn_ref, bias0_ref, bias1_ref, sink_ref, o_ref):
    kvall = jnp.concatenate([kp_ref[...], kc_ref[...], kn_ref[...]], axis=0)
    lane = lax.broadcasted_iota(jnp.int32, (1, LANES), 1)
    keep = [(lane < HEAD_DIM).astype(F32).astype(BF16), (lane >= HEAD_DIM).astype(F32).astype(BF16)]
    ntile = N_KV // 2
    for qb, bias_ref in enumerate((bias0_ref, bias1_ref)):
        kv = kvall[qb * BLOCK:(qb + 3) * BLOCK]
        rows = slice(qb * BLOCK, (qb + 1) * BLOCK)
        for j in range(ntile):
            kt = kv[:, j * LANES:(j + 1) * LANES]
            vt = kv[:, (ntile + j) * LANES:(ntile + j + 1) * LANES]
            tiles = [GROUP * j + g for g in range(GROUP)]
            qs = jnp.concatenate([q_ref[rows, t * LANES:(t + 1) * LANES] for t in tiles], axis=0)
            acc = None
            for half in range(2):
                kh = 2 * j + half
                lg = lax.dot_general(kt * keep[half], qs, (((1,), (1,)), ((), ())),
                                     preferred_element_type=F32)
                lg = lg + bias_ref[kh]
                sink = sink_ref[kh]
                m = jnp.maximum(jnp.max(lg, axis=0, keepdims=True), sink)
                p = jnp.exp2(lg - m)
                den = jnp.sum(p, axis=0, keepdims=True) + jnp.exp2(sink - m)
                o = lax.dot_general(vt * keep[half], p.astype(BF16), (((0,), (0,)), ((), ())),
                                    preferred_element_type=F32) * (1.0 / den)
                acc = o if acc is None else acc + o
            for g, t in enumerate(tiles):
                o_ref[qb, t * LANES:(t + 1) * LANES, :] = (
                    acc[:, g * BLOCK:(g + 1) * BLOCK].astype(o_ref.dtype))


def _attention(qkv, bias_t, sink_row):
    bsz, seq, _ = qkv.shape
    nb = seq // BLOCK
    dq = N_HEADS * HEAD_DIM
    dkv = 2 * N_KV * HEAD_DIM
    kvb = dq // dkv
    qb = ATTN_QBLOCKS
    steps = nb // qb
    bias_spec = lambda variant: pl.BlockSpec((None, N_KV, 3 * BLOCK, GROUP * BLOCK),
                                             lambda b, n: (variant(n), 0, 0, 0))
    return pl.pallas_call(
        _attn_kernel,
        grid=(bsz, steps),
        in_specs=[
            pl.BlockSpec((None, qb * BLOCK, dq), lambda b, n: (b, n, 0)),
            pl.BlockSpec((None, BLOCK, dkv), lambda b, n: (b, jnp.maximum(qb * n - 1, 0), kvb)),
            pl.BlockSpec((None, qb * BLOCK, dkv), lambda b, n: (b, n, kvb)),
            pl.BlockSpec((None, BLOCK, dkv), lambda b, n: (b, jnp.minimum(qb * n + qb, nb - 1), kvb)),
            bias_spec(lambda n: jnp.where(n == 0, 0, 1)),
            bias_spec(lambda n: jnp.where(n == steps - 1, 2, 1)),
            pl.BlockSpec((N_KV, 1, GROUP * BLOCK), lambda b, n: (0, 0, 0)),
        ],
        out_specs=pl.BlockSpec((None, qb, dq, BLOCK), lambda b, n: (b, n, 0, 0)),
        out_shape=jax.ShapeDtypeStruct((bsz, nb, dq, BLOCK), BF16),
        compiler_params=_cparams(("parallel", "parallel")),
        name="window_attention",
    )(qkv, qkv, qkv, qkv, bias_t, bias_t, sink_row)


def _attn_tail(tile, x_ref, ot_ref, w_ref):
    ys = [lax.dot_general(ot_ref[k], w_ref[...], (((0,), (0,)), ((), ())), preferred_element_type=F32)
          for k in range(FFN_TM // BLOCK)]
    return x_ref[...] + jnp.concatenate(ys, axis=0)


def _attn_tail_specs(tile):
    dq = N_HEADS * HEAD_DIM
    return [
        pl.BlockSpec((FFN_TM // BLOCK, dq, BLOCK), lambda s: (tile(s), 0, 0)),
        pl.BlockSpec((dq, D_MODEL), lambda s: (0, 0)),
    ]


def _head_tile_perm():
    cols = []
    for tile in range(N_HEADS // 2):
        j, g = tile // GROUP, tile % GROUP
        for half in range(2):
            h = (2 * j + half) * GROUP + g
            cols.extend(range(h * HEAD_DIM, (h + 1) * HEAD_DIM))
    return jnp.asarray(cols, jnp.int32)


def _attn_mixer(x, g_mix, w_qkv, q_gain, k_gain, sink, w_o, rel_table):
    bsz, seq, d = x.shape
    x2 = x.reshape(bsz * seq, d)
    dq = N_HEADS * HEAD_DIM
    dk = N_KV * HEAD_DIM
    perm = _head_tile_perm()
    w = jnp.concatenate([w_qkv[:, :dq][:, perm], w_qkv[:, dq:]], axis=1).astype(BF16)
    log2e = math.log2(math.e)
    gain = jnp.concatenate([jnp.tile(q_gain * (HEAD_DIM ** -0.5 * log2e), N_HEADS), jnp.tile(k_gain, N_KV),
                            jnp.ones((dk,), F32)])[None]
    flag = jnp.concatenate([jnp.ones((dq + dk,), F32), jnp.zeros((dk,), F32)])[None]
    qkv = _qkv_proj(x2, g_mix[None], w, gain, flag)
    bias_t = _rel_bias(rel_table).reshape(N_KV, GROUP * BLOCK, 3 * BLOCK).transpose(0, 2, 1)
    key = jnp.arange(3 * BLOCK)[None, :, None]
    bias_t = jnp.stack([jnp.where(key < BLOCK, -1e30, bias_t), bias_t,
                        jnp.where(key >= 2 * BLOCK, -1e30, bias_t)]) * log2e
    sink_row = jnp.repeat(sink * log2e, BLOCK).reshape(N_KV, 1, GROUP * BLOCK)
    o_t = _attention(qkv.reshape(bsz, seq, dq + 2 * dk), bias_t, sink_row)
    return (o_t.reshape(bsz * (seq // BLOCK), dq, BLOCK), w_o[perm, :].astype(BF16))


def _pool_tail(tile, xc_ref, xp_ref, xn_ref, g_ref, w_ref, b_ref, sc_ref):
    tm = FFN_TM
    g = g_ref[...]
    rows = tm + 2 * SUBLANES
    t0 = (tile % FFN_TILES_PER_SEQ) * tm
    t = t0 - SUBLANES + lax.broadcasted_iota(jnp.int32, (rows, 1), 0)
    inside = (t >= 0) & (t < SEQ)
    xa = jnp.concatenate([_rms(xp_ref[...], g), _rms(xc_ref[...], g), _rms(xn_ref[...], g)], axis=0)
    xa = jnp.where(inside, xa, 0.0)
    tc = t[SUBLANES:SUBLANES + tm]
    outs = []
    for gi, wdt in enumerate(POOL_WINDOWS):
        r = wdt // 2
        xg = xa[:, gi * POOL_GROUP:(gi + 1) * POOL_GROUP]
        run = xg
        span = 1
        while span < 2 * r:
            run = run + pltpu.roll(run, rows - span, 0)
            span *= 2
        win = (pltpu.roll(run, r, 0) + pltpu.roll(xg, rows - r, 0))[SUBLANES:SUBLANES + tm]
        cnt = (jnp.minimum(tc + r + 1, SEQ) - jnp.maximum(tc - r, 0)).astype(F32)
        dlt = win * (1.0 / cnt) - xg[SUBLANES:SUBLANES + tm]
        outs.append(jnp.dot(dlt.astype(BF16), w_ref[gi], preferred_element_type=F32))
    y = (jnp.concatenate(outs, axis=-1) + b_ref[...]) * sc_ref[...]
    return xc_ref[...] + y


def _pool_tail_specs(tile, rows):
    d = D_MODEL
    hb = FFN_TM // SUBLANES
    last = rows // SUBLANES - 1
    const = lambda s: (0, 0)
    return [
        pl.BlockSpec((SUBLANES, d), lambda s: (jnp.maximum(tile(s) * hb - 1, 0), 0)),
        pl.BlockSpec((SUBLANES, d), lambda s: (jnp.minimum((tile(s) + 1) * hb, last), 0)),
        pl.BlockSpec((1, d), const),
        pl.BlockSpec((len(POOL_WINDOWS), POOL_GROUP, POOL_GROUP), lambda s: (0, 0, 0)),
        pl.BlockSpec((1, d), const),
        pl.BlockSpec((1, d), const),
    ]


def kernel(x, norm_mix, norm_ffn, hy_w_in, hy_b_in, hy_conv_w, hy_conv_b, hy_f_w1, hy_f_b1, hy_f_wi, hy_f_bi, hy_f_w3, hy_f_freq, hy_f_bias, hy_w_out, hy_b_out, at_w_qkv, at_q_gain, at_k_gain, at_sink, at_w_o, rel_table, pl_w, pl_b, pl_scale, ff_w_gate, ff_w_up, ff_w_down):
    bsz, seq, d = x.shape
    f1, gm = _dft_mats()
    f1_full = f1.astype(BF16)
    f1_half = f1[:, :FFT_N1 // 2].astype(BF16)
    fi = (f1[:, :FFT_N1 // 2].T * (2.0 / FFT_N)).astype(BF16)
    gh = jnp.swapaxes(gm, 1, 2).astype(BF16)
    gm = gm.astype(BF16)
    fq = jnp.linspace(1e-4, HY_BANDS - 1, HY_BANDS, dtype=F32)[None, :]
    deltas = jnp.abs(jnp.linspace(HY_MIN_DECAY, HY_MAX_DECAY, D_MODEL, dtype=F32))
    dl2 = jnp.concatenate([deltas, deltas])[None, :]
    consts = (f1_full, f1_half, gm, gh, fi, fq, dl2)

    x2 = x.reshape(bsz * seq, d)
    for i in range(DEPTH):
        kind, s = i % 3, i // 3
        x = x2.reshape(bsz, seq, d)
        if kind == 0:
            args = _hyena_mixer(x, norm_mix[i], hy_w_in[s], hy_b_in[s], hy_conv_w[s], hy_conv_b[s],
                                hy_f_w1[s], hy_f_b1[s], hy_f_wi[s], hy_f_bi[s], hy_f_w3[s],
                                hy_f_freq[s], hy_f_bias[s], hy_w_out[s], hy_b_out[s], consts)
            tail, specs = _hyena_tail, _hyena_tail_specs
        elif kind == 1:
            args = _attn_mixer(x, norm_mix[i], at_w_qkv[s], at_q_gain[s], at_k_gain[s], at_sink[s],
                               at_w_o[s], rel_table)
            tail, specs = _attn_tail, _attn_tail_specs
        else:
            args = (x2, x2, norm_mix[i][None], pl_w[s].astype(BF16), pl_b[s][None], pl_scale[s][None])
            tail, specs = _pool_tail, functools.partial(_pool_tail_specs, rows=bsz * seq)
        x2 = _ffn(x2, norm_ffn[i][None], ff_w_gate, ff_w_up, ff_w_down, i, tail, args, specs)
    return x2.reshape(bsz, seq, d)
```

```python
import functools
import math

import jax
import jax.numpy as jnp
from jax import lax
from jax.experimental import pallas as pl
from jax.experimental.pallas import tpu as pltpu

F32 = jnp.float32
BF16 = jnp.bfloat16

D_MODEL = 1024
SEQ = 8192
DEPTH = 4
EPS = 1e-6

HY_SHORT = 3
HY_BANDS = 16
HY_FW = 64
HY_INNER = 2
HY_MIN_DECAY = math.log(1e-2) / 1.5
HY_MAX_DECAY = math.log(1e-2) / 0.3

HEAD_DIM = 64
N_HEADS = 16
N_KV = 4
GROUP = 4
WINDOW = 128
BLOCK = 128
REL_BUCKETS = 32
REL_MAX_DIST = 128

POOL_WINDOWS = (2, 4, 8, 16)
POOL_GROUP = 256
D_FF = 2816

SUBLANES = 8
LANES = 128
VMEM_LIMIT = 56 * 1024 * 1024

FFT_N = 2 * SEQ
FFT_N1 = 256
FFT_N2 = 64
FFT_K1 = FFT_N1 // 2
SLAB = 2 * FFT_K1 + SUBLANES
CONV_DT = 128
DFT_UNROLL = 32


def _cparams(sem):
    return pltpu.CompilerParams(dimension_semantics=sem, vmem_limit_bytes=VMEM_LIMIT)


def _rms(x, g):
    return x * lax.rsqrt(jnp.mean(x * x, axis=-1, keepdims=True) + EPS) * g


FFN_CHUNK = 256
FFN_CHUNKS = D_FF // FFN_CHUNK
FFN_TM = 512
FFN_TILES_PER_SEQ = SEQ // FFN_TM


def _ffn_kernel(*refs, tail, n_tail):
    x_ref, g_ref, wg_ref, wu_ref, wd_ref = refs[:5]
    tail_refs = refs[5:5 + n_tail]
    o_ref, wgb_ref, wub_ref, wdb_ref, h_ref = refs[5 + n_tail:]
    s = pl.program_id(0)

    @pl.when(s < FFN_CHUNKS)
    def _():
        wgb_ref[s] = wg_ref[...].astype(BF16)
        wub_ref[s] = wu_ref[...].astype(BF16)
        wdb_ref[pl.ds(pl.multiple_of(s * FFN_CHUNK, FFN_CHUNK), FFN_CHUNK), :] = wd_ref[...].astype(BF16)

    @pl.when(s >= FFN_CHUNKS)
    def _():
        x = tail(s - FFN_CHUNKS, x_ref, *tail_refs)
        xn = _rms(x, g_ref[...]).astype(BF16)
        for c in range(FFN_CHUNKS):
            gate = jnp.dot(xn, wgb_ref[c], preferred_element_type=F32)
            up = jnp.dot(xn, wub_ref[c], preferred_element_type=F32)
            h_ref[:, c * FFN_CHUNK:(c + 1) * FFN_CHUNK] = (gate * jax.nn.sigmoid(gate) * up).astype(BF16)
        o_ref[...] = x + jnp.dot(h_ref[...], wdb_ref[...], preferred_element_type=F32)


def _ffn(x2, g, wg, wu, wd, layer, tail, tail_args, tail_specs):
    m, d = x2.shape
    f = wg.shape[2]
    tm = FFN_TM
    tile = lambda s: jnp.maximum(s - FFN_CHUNKS, 0)
    chunk = lambda s: jnp.minimum(s, FFN_CHUNKS - 1)
    return pl.pallas_call(
        functools.partial(_ffn_kernel, tail=tail, n_tail=len(tail_args)),
        grid=(FFN_CHUNKS + m // tm,),
        in_specs=[
            pl.BlockSpec((tm, d), lambda s: (tile(s), 0)),
            pl.BlockSpec((1, d), lambda s: (0, 0)),
            pl.BlockSpec((None, d, FFN_CHUNK), lambda s: (layer, 0, chunk(s))),
            pl.BlockSpec((None, d, FFN_CHUNK), lambda s: (layer, 0, chunk(s))),
            pl.BlockSpec((None, FFN_CHUNK, d), lambda s: (layer, chunk(s), 0)),
        ] + tail_specs(tile),
        out_specs=pl.BlockSpec((tm, d), lambda s: (tile(s), 0)),
        out_shape=jax.ShapeDtypeStruct((m, d), F32),
        scratch_shapes=[pltpu.VMEM((FFN_CHUNKS, d, FFN_CHUNK), BF16),
                        pltpu.VMEM((FFN_CHUNKS, d, FFN_CHUNK), BF16),
                        pltpu.VMEM((f, d), BF16),
                        pltpu.VMEM((tm, f), BF16)],
        compiler_params=_cparams(("arbitrary",)),
        name="mixer_tail_ffn",
    )(x2, g, wg, wu, wd, *tail_args)


INPROJ_N1 = 16
INPROJ_CHUNK = 512


def _slab_perm(nb):
    r = jnp.arange(FFT_N2 * nb)
    src = FFT_N2 * (r % nb) + r // nb
    return (src[:, None] == r[None, :]).astype(BF16)


def _inproj_kernel(xp_ref, xc_ref, xn_ref, g_ref, pm_ref, w_ref, b_ref, cw_ref, cb_ref, o_ref, xs_ref):
    i = pl.program_id(1)
    nb = INPROJ_N1
    tm = FFT_N2 * nb
    g = g_ref[...]
    xn = _rms(xc_ref[...], g).astype(BF16)
    xs_ref[:tm, :] = jnp.dot(pm_ref[...], xn, preferred_element_type=F32).astype(BF16)
    halo = jnp.concatenate([xp_ref[...], xn_ref[...]], axis=0)
    xs_ref[tm:, :] = _rms(halo, g).astype(BF16)

    xs = xs_ref[...]
    row = lax.broadcasted_iota(jnp.int32, (nb, 1), 0)
    inside_before = i > 0
    inside_after = i < pl.num_programs(1) - 1
    for c in range(w_ref.shape[1] // INPROJ_CHUNK):
        cols = slice(c * INPROJ_CHUNK, (c + 1) * INPROJ_CHUNK)
        p = jnp.dot(xs, w_ref[:, cols], preferred_element_type=F32) + b_ref[:, cols]
        cur = p[:tm]
        before = jnp.where(inside_before, p[tm + SUBLANES - 1:tm + SUBLANES], 0.0)
        after = jnp.where(inside_after, p[tm + SUBLANES:tm + SUBLANES + 1], 0.0)
        first = jnp.where(row == 0, before, pltpu.roll(cur[tm - nb:], 1, 0))
        last = jnp.where(row == nb - 1, after, pltpu.roll(cur[:nb], nb - 1, 0))
        dn = jnp.concatenate([first, cur[:tm - nb]], axis=0)
        up = jnp.concatenate([cur[nb:], last], axis=0)
        cw = cw_ref[:, cols]
        out = cb_ref[:, cols] + cw[0:1] * dn + cw[1:2] * cur + cw[2:3] * up
        o_ref[:, :, cols] = out.reshape(FFT_N2, nb, INPROJ_CHUNK).astype(o_ref.dtype)


def _inproj_conv(x, g, w, b, cw, cb):
    bsz, seq, d = x.shape
    n = w.shape[1]
    nb = INPROJ_N1
    tm = FFT_N2 * nb
    hb = tm // SUBLANES
    last = seq // SUBLANES - 1
    const = lambda bi, i: (0, 0)
    return pl.pallas_call(
        _inproj_kernel,
        grid=(bsz, seq // tm),
        in_specs=[
            pl.BlockSpec((None, SUBLANES, d), lambda bi, i: (bi, jnp.maximum(i * hb - 1, 0), 0)),
            pl.BlockSpec((None, tm, d), lambda bi, i: (bi, i, 0)),
            pl.BlockSpec((None, SUBLANES, d), lambda bi, i: (bi, jnp.minimum((i + 1) * hb, last), 0)),
            pl.BlockSpec((1, d), const),
            pl.BlockSpec((tm, tm), const, pipeline_mode=pl.Buffered(1)),
            pl.BlockSpec((d, n), const, pipeline_mode=pl.Buffered(1)),
            pl.BlockSpec((1, n), const),
            pl.BlockSpec((HY_SHORT, n), const),
            pl.BlockSpec((1, n), const),
        ],
        out_specs=pl.BlockSpec((None, FFT_N2, nb, n), lambda bi, i: (bi, 0, i, 0)),
        out_shape=jax.ShapeDtypeStruct((bsz, FFT_N2, seq // FFT_N2, n), BF16),
        scratch_shapes=[pltpu.VMEM((tm + 2 * SUBLANES, d), BF16)],
        compiler_params=_cparams(("parallel", "parallel")),
        name="hyena_inproj",
    )(x, x, x, g, _slab_perm(nb), w, b, cw, cb)


def _dft_mats():
    i32 = jnp.int32
    k1 = jnp.arange(FFT_K1, dtype=i32)[:, None]
    n1 = jnp.arange(FFT_N1, dtype=i32)[None, :]
    m = ((2 * k1 + 1) * n1) % (2 * FFT_N1)
    ang = m.astype(F32) * (math.pi / FFT_N1)
    f1 = jnp.concatenate([jnp.cos(ang), -jnp.sin(ang)], axis=0)
    k1 = jnp.arange(FFT_K1, dtype=i32)[:, None, None]
    k2 = jnp.arange(FFT_N2, dtype=i32)[None, :, None]
    n2 = jnp.arange(FFT_N2, dtype=i32)[None, None, :]
    m = ((2 * (k1 + FFT_N1 * k2) + 1) * n2) % (2 * FFT_N)
    ang = m.astype(F32) * (math.pi / FFT_N)
    c, s = jnp.cos(ang), jnp.sin(ang)
    gm = jnp.concatenate(
        [jnp.concatenate([c, s], axis=2), jnp.concatenate([-s, c], axis=2)], axis=1)
    return f1, gm


def _dot3(a3, b):
    hi, lo = _split_bf16(b)
    return jnp.dot(a3, jnp.concatenate([hi, hi, lo], axis=0), preferred_element_type=F32)


def _stack3_cols(a):
    hi, lo = _split_bf16(a)
    return jnp.concatenate([hi, lo, hi], axis=-1)


def _tap_lag(n):
    return jnp.where(n >= SEQ, FFT_N - n, n).astype(F32)


def _split_bf16(a):
    hi = a.astype(BF16)
    return hi, (a - hi.astype(F32)).astype(BF16)


def _filter_mlp_kernel(fq_ref, w1t_ref, w1b_ref, b1_ref, wi_ref, bi_ref, fr_ref, h3_ref, *, slabs):
    cols = slabs * FFT_N1
    c = lax.broadcasted_iota(jnp.int32, (1, cols), 1)
    n2 = pl.program_id(0) * slabs + c // FFT_N1
    pos = _tap_lag(FFT_N2 * (c % FFT_N1) + n2)
    t = pos * (1.0 / (SEQ - 1))
    ang = (2.0 * math.pi / SEQ) * pos * fq_ref[...]
    fr = fr_ref[...]
    bands = jnp.concatenate([jnp.cos(ang), -jnp.sin(ang)], axis=0)
    pre = w1t_ref[...] * t + _dot3(w1b_ref[...], bands)
    h = jnp.sin(fr * (pre + b1_ref[...]))
    for l in range(HY_INNER):
        h = jnp.sin(fr * (_dot3(wi_ref[l], h) + bi_ref[l]))
    hi, lo = _split_bf16(h)
    h3 = jnp.concatenate([hi, lo, hi], axis=0)
    for k in range(slabs):
        h3_ref[k] = h3[:, k * FFT_N1:(k + 1) * FFT_N1]


def _filter_mlp(fq, w1, b1, wi, bi, fr, slabs=8):
    w1t, w1b = w1[0:1].T, _stack3_cols(w1[1:].T)
    wi3 = _stack3_cols(jnp.swapaxes(wi, 1, 2))
    col = lambda a: a.reshape(-1, 1)
    const2 = lambda i: (0, 0)
    const3 = lambda i: (0, 0, 0)
    return pl.pallas_call(
        functools.partial(_filter_mlp_kernel, slabs=slabs),
        grid=(FFT_N2 // slabs,),
        in_specs=[
            pl.BlockSpec((HY_BANDS, 1), const2),
            pl.BlockSpec((HY_FW, 1), const2),
            pl.BlockSpec((HY_FW, 3 * 2 * HY_BANDS), const2),
            pl.BlockSpec((HY_FW, 1), const2),
            pl.BlockSpec((HY_INNER, HY_FW, 3 * HY_FW), const3),
            pl.BlockSpec((HY_INNER, HY_FW, 1), const3),
            pl.BlockSpec((HY_FW, 1), const2),
        ],
        out_specs=pl.BlockSpec((slabs, 3 * HY_FW, FFT_N1), lambda i: (i, 0, 0)),
        out_shape=jax.ShapeDtypeStruct((FFT_N2, 3 * HY_FW, FFT_N1), BF16),
        compiler_params=_cparams(("parallel",)),
        name="hyena_filter_mlp",
    )(col(fq), w1t, w1b, col(b1), wi3, bi[:, :, None], col(fr))


def _slab_store(s_ref, n2, a):
    rows = pl.ds(pl.multiple_of(n2 * SLAB, SUBLANES), 2 * FFT_K1)
    for l in range(s_ref.shape[0]):
        s_ref[l, rows, :] = a[:, l * LANES:(l + 1) * LANES]


def _slab_load(s_ref, n2):
    rows = pl.ds(pl.multiple_of(n2 * SLAB, SUBLANES), 2 * FFT_K1)
    return jnp.concatenate([s_ref[l, rows, :] for l in range(s_ref.shape[0])], axis=1).astype(BF16)


def _k1_load(s_ref, k1):
    re = pl.ds(k1, FFT_N2, stride=SLAB)
    im = pl.ds(FFT_K1 + k1, FFT_N2, stride=SLAB)
    tiles = [jnp.concatenate([s_ref[l, re, :], s_ref[l, im, :]], axis=0) for l in range(s_ref.shape[0])]
    return jnp.concatenate(tiles, axis=1).astype(BF16)


def _k1_store(s_ref, k1, b):
    for l in range(s_ref.shape[0]):
        cols = slice(l * LANES, (l + 1) * LANES)
        s_ref[l, pl.ds(k1, FFT_N2, stride=SLAB), :] = b[:FFT_N2, cols]
        s_ref[l, pl.ds(FFT_K1 + k1, FFT_N2, stride=SLAB), :] = b[FFT_N2:, cols]


def _stage_scratch(dt):
    return pltpu.VMEM((dt // LANES, FFT_N2 * SLAB, LANES), F32)


def _stage1(src_ref, f1_ref, s_ref):
    def body(n2, c):
        _slab_store(s_ref, n2, jnp.dot(f1_ref[...], src_ref[n2].astype(BF16), preferred_element_type=F32))
        return c
    lax.fori_loop(0, FFT_N2, body, 0, unroll=DFT_UNROLL)


def _filt_dft_kernel(h3_ref, w3_ref, dl_ref, f1_ref, gm_ref, kf_ref, s_ref):
    half = FFT_N1 // 2
    dt = dl_ref.shape[1]
    n1 = lax.broadcasted_iota(jnp.int32, (FFT_N1, 1), 0)
    dl = dl_ref[...]

    def body(n2, l1):
        both = lax.dot_general(h3_ref[n2], w3_ref[...], (((0,), (0,)), ((), ())),
                               preferred_element_type=F32)
        fwd = both[:half, :dt]
        bwd = both[half:, dt:]
        n = FFT_N2 * n1 + n2
        t = _tap_lag(n) * (1.0 / (SEQ - 1))
        taps = jnp.concatenate([fwd, -bwd], axis=0) * jnp.exp(-t * dl)
        taps = jnp.where(n == SEQ, 0.0, taps)
        _slab_store(s_ref, n2, jnp.dot(f1_ref[...], taps.astype(BF16), preferred_element_type=F32))
        return l1 + jnp.sum(jnp.abs(taps), axis=0, keepdims=True)
    l1 = lax.fori_loop(0, FFT_N2, body, jnp.zeros(dl.shape, F32), unroll=DFT_UNROLL)
    inv = 1.0 / l1

    def stage2(k1, c):
        x = jnp.dot(gm_ref[k1], _k1_load(s_ref, k1), preferred_element_type=F32)
        rows = pl.ds(pl.multiple_of(k1 * 2 * FFT_N2, 2 * FFT_N2), 2 * FFT_N2)
        kf_ref[rows, :] = x * inv
        return c
    lax.fori_loop(0, FFT_K1, stage2, 0, unroll=DFT_UNROLL)


def _filter_spectrum(h3, w3, dl2, f1_full, gm):
    ncol = 2 * D_MODEL
    dt = CONV_DT
    nct = ncol // dt
    tiles = lambda a: a.reshape(HY_FW, 2, nct, dt).transpose(2, 0, 1, 3).reshape(nct, HY_FW, 2 * dt)
    w3h, w3l = _split_bf16(w3)
    w3cat = jnp.concatenate([tiles(w3h), tiles(w3h), tiles(w3l)], axis=1)
    return pl.pallas_call(
        _filt_dft_kernel,
        grid=(nct,),
        in_specs=[
            pl.BlockSpec((FFT_N2, 3 * HY_FW, FFT_N1), lambda j: (0, 0, 0)),
            pl.BlockSpec((None, 3 * HY_FW, 2 * dt), lambda j: (j, 0, 0)),
            pl.BlockSpec((1, dt), lambda j: (0, j)),
            pl.BlockSpec((2 * FFT_K1, FFT_N1), lambda j: (0, 0)),
            pl.BlockSpec((FFT_K1, 2 * FFT_N2, 2 * FFT_N2), lambda j: (0, 0, 0),
                         pipeline_mode=pl.Buffered(1)),
        ],
        out_specs=pl.BlockSpec((FFT_K1 * 2 * FFT_N2, dt), lambda j: (0, j)),
        out_shape=jax.ShapeDtypeStruct((FFT_K1 * 2 * FFT_N2, ncol), F32),
        scratch_shapes=[_stage_scratch(dt)],
        compiler_params=_cparams(("parallel",)),
        name="hyena_filter_spectrum",
    )(h3, w3cat, dl2, f1_full, gm)


def _longconv_kernel(z_ref, g_ref, fb_ref, kf_ref, f1_ref, gm_ref, gh_ref, fi_ref, o_ref, s_ref):
    _stage1(z_ref, f1_ref, s_ref)

    def mid(k1, c):
        x = jnp.dot(gm_ref[k1], _k1_load(s_ref, k1), preferred_element_type=F32)
        kf = kf_ref[pl.ds(pl.multiple_of(k1 * 2 * FFT_N2, 2 * FFT_N2), 2 * FFT_N2), :]
        xr, xi = x[:FFT_N2], x[FFT_N2:]
        kr, ki = kf[:FFT_N2], kf[FFT_N2:]
        y = jnp.concatenate([xr * kr - xi * ki, xr * ki + xi * kr], axis=0).astype(BF16)
        b = jnp.dot(gh_ref[k1], y, preferred_element_type=F32)
        _k1_store(s_ref, k1, b)
        return c
    lax.fori_loop(0, FFT_K1, mid, 0, unroll=2 * DFT_UNROLL)

    fb = fb_ref[...]

    def last(n2, c):
        y = jnp.dot(fi_ref[...], _slab_load(s_ref, n2), preferred_element_type=F32)
        o_ref[n2] = (g_ref[n2] * (y + fb * z_ref[n2])).astype(o_ref.dtype)
        return c
    lax.fori_loop(0, FFT_N2, last, 0, unroll=DFT_UNROLL)


def _long_conv_gate(z, zcol, g, gcol, fb, kf, kcol, f1_half, gm, gh, fi):
    bsz = z.shape[0]
    dt = CONV_DT
    nt = D_MODEL // dt
    n1 = FFT_N1 // 2
    const3 = lambda j, bi: (0, 0, 0)
    return pl.pallas_call(
        _longconv_kernel,
        grid=(nt, bsz),
        in_specs=[
            pl.BlockSpec((None, FFT_N2, n1, dt), lambda j, bi: (bi, 0, 0, zcol + j)),
            pl.BlockSpec((None, FFT_N2, n1, dt), lambda j, bi: (bi, 0, 0, gcol + j)),
            pl.BlockSpec((1, dt), lambda j, bi: (0, j)),
            pl.BlockSpec((FFT_K1 * 2 * FFT_N2, dt), lambda j, bi: (0, kcol + j)),
            pl.BlockSpec((2 * FFT_K1, FFT_N1 // 2), lambda j, bi: (0, 0)),
            pl.BlockSpec((FFT_K1, 2 * FFT_N2, 2 * FFT_N2), const3, pipeline_mode=pl.Buffered(1)),
            pl.BlockSpec((FFT_K1, 2 * FFT_N2, 2 * FFT_N2), const3, pipeline_mode=pl.Buffered(1)),
            pl.BlockSpec((FFT_N1 // 2, 2 * FFT_K1), lambda j, bi: (0, 0)),
        ],
        out_specs=pl.BlockSpec((None, FFT_N2, n1, dt), lambda j, bi: (bi, 0, 0, j)),
        out_shape=jax.ShapeDtypeStruct((bsz, FFT_N2, n1, D_MODEL), BF16),
        scratch_shapes=[_stage_scratch(dt)],
        compiler_params=_cparams(("parallel", "arbitrary")),
        name="hyena_long_conv",
    )(z, g, fb, kf, f1_half, gm, gh, fi)


def _hyena_tail(tile, x_ref, z_ref, pm_ref, w_ref, b_ref):
    z = z_ref[...].reshape(x_ref.shape)
    a = jnp.dot(pm_ref[...], z, preferred_element_type=F32).astype(BF16)
    return x_ref[...] + jnp.dot(a, w_ref[...], preferred_element_type=F32) + b_ref[...]


def _hyena_tail_specs(tile):
    nb = FFN_TM // FFT_N2
    d = D_MODEL
    const = lambda s: (0, 0)
    return [
        pl.BlockSpec((None, FFT_N2, nb, d),
                     lambda s: (tile(s) // FFN_TILES_PER_SEQ, 0, tile(s) % FFN_TILES_PER_SEQ, 0)),
        pl.BlockSpec((FFN_TM, FFN_TM), const),
        pl.BlockSpec((d, d), const),
        pl.BlockSpec((1, d), const),
    ]


def _hyena_mixer(x, g_mix, w_in, b_in, cw, cb, fw1, fb1, fwi, fbi, fw3, freq, fbias, w_out, b_out, consts):
    f1_full, f1_half, gm, gh, fi, fq, dl2 = consts
    pc = _inproj_conv(x, g_mix[None], w_in.astype(BF16), b_in[None], cw, cb[None])
    h3 = _filter_mlp(fq, fw1, fb1, fwi, fbi, freq)
    kf = _filter_spectrum(h3, fw3, dl2, f1_full, gm)
    nt = D_MODEL // CONV_DT
    z1 = _long_conv_gate(pc, 2 * nt, pc, 0, fbias[0:1], kf, 0, f1_half, gm, gh, fi)
    z2 = _long_conv_gate(z1, 0, pc, nt, fbias[1:2], kf, nt, f1_half, gm, gh, fi)
    return (z2, _slab_perm(FFN_TM // FFT_N2).T, w_out.astype(BF16), b_out[None])


def _relbias_kernel(tab_ref, bucket_ref, band_ref, o_ref):
    bucket = bucket_ref[...]
    band = band_ref[...]
    for h in range(N_HEADS):
        acc = jnp.zeros(bucket.shape, F32)
        for k in range(REL_BUCKETS):
            acc = jnp.where(bucket == k, tab_ref[k, h], acc)
        o_ref[h] = jnp.where(band > 0, acc, -1e30)


def _rel_bias(rel_table):
    a = jnp.arange(BLOCK)[:, None]
    j = jnp.arange(3 * BLOCK)[None, :]
    rel = j - BLOCK - a
    half = REL_BUCKETS // 2
    exact = half // 2
    n = jnp.abs(rel)
    nf = jnp.maximum(n, 1).astype(F32)
    large = exact + (jnp.log(nf / exact) / math.log(REL_MAX_DIST / exact) * (half - exact)).astype(jnp.int32)
    large = jnp.minimum(large, half - 1)
    bucket = (jnp.where(rel > 0, half, 0) + jnp.where(n < exact, n, large)).astype(jnp.int32)
    band = (n <= WINDOW).astype(jnp.int32)
    return pl.pallas_call(
        _relbias_kernel,
        in_specs=[
            pl.BlockSpec(memory_space=pltpu.SMEM),
            pl.BlockSpec((BLOCK, 3 * BLOCK), lambda: (0, 0)),
            pl.BlockSpec((BLOCK, 3 * BLOCK), lambda: (0, 0)),
        ],
        out_specs=pl.BlockSpec((N_HEADS, BLOCK, 3 * BLOCK), lambda: (0, 0, 0)),
        out_shape=jax.ShapeDtypeStruct((N_HEADS, BLOCK, 3 * BLOCK), F32),
        name="rel_bias",
    )(rel_table, bucket, band)


QKV_CHUNK = 512


def _qkv_kernel(x_ref, g_ref, w_ref, gain_ref, flag_ref, bd_ref, o_ref):
    xn = _rms(x_ref[...], g_ref[...]).astype(BF16)
    for c in range(w_ref.shape[1] // QKV_CHUNK):
        cols = slice(c * QKV_CHUNK, (c + 1) * QKV_CHUNK)
        acc = jnp.dot(xn, w_ref[:, cols], preferred_element_type=F32)
        ssq = jnp.dot((acc * acc).astype(BF16), bd_ref[...], preferred_element_type=F32)
        normed = acc * lax.rsqrt(ssq * (1.0 / HEAD_DIM) + EPS) * gain_ref[:, cols]
        o_ref[:, cols] = jnp.where(flag_ref[:, cols] > 0, normed, acc).astype(o_ref.dtype)


def _qkv_proj(x2, g, w, gain, flag, tm=1024):
    m, k = x2.shape
    n = w.shape[1]
    tn = QKV_CHUNK
    eye = jnp.arange(tn)[:, None] // HEAD_DIM == jnp.arange(tn)[None, :] // HEAD_DIM
    const = lambda i: (0, 0)
    return pl.pallas_call(
        _qkv_kernel,
        grid=(m // tm,),
        in_specs=[
            pl.BlockSpec((tm, k), lambda i: (i, 0)),
            pl.BlockSpec((1, k), const),
            pl.BlockSpec((k, n), const),
            pl.BlockSpec((1, n), const),
            pl.BlockSpec((1, n), const),
            pl.BlockSpec((tn, tn), const),
        ],
        out_specs=pl.BlockSpec((tm, n), lambda i: (i, 0)),
        out_shape=jax.ShapeDtypeStruct((m, n), BF16),
        compiler_params=_cparams(("parallel",)),
        name="qkv_proj",
    )(x2, g, w, gain, flag, eye.astype(BF16))


ATTN_QBLOCKS = 2


def _attn_kernel(q_ref, kp_ref, kc_ref, kn_ref, bias0_ref, bias1_ref, sink_ref, o_ref):
    kvall = jnp.concatenate([kp_ref[...], kc_ref[...], kn_ref[...]], axis=0)
    lane = lax.broadcasted_iota(jnp.int32, (1, LANES), 1)
    keep = [(lane < HEAD_DIM).astype(F32).astype(BF16), (lane >= HEAD_DIM).astype(F32).astype(BF16)]
    ntile = N_KV // 2
    for qb, bias_ref in enumerate((bias0_ref, bias1_ref)):
        kv = kvall[qb * BLOCK:(qb + 3) * BLOCK]
        rows = slice(qb * BLOCK, (qb + 1) * BLOCK)
        for j in range(ntile):
            kt = kv[:, j * LANES:(j + 1) * LANES]
            vt = kv[:, (ntile + j) * LANES:(ntile + j + 1) * LANES]
            tiles = [GROUP * j + g for g in range(GROUP)]
            qs = jnp.concatenate([q_ref[rows, t * LANES:(t + 1) * LANES] for t in tiles], axis=0)
            acc = None
            for half in range(2):
                kh = 2 * j + half
                lg = lax.dot_general(kt * keep[half], qs, (((1,), (1,)), ((), ())),
                                     preferred_element_type=F32)
                lg = lg + bias_ref[kh]
                sink = sink_ref[kh]
                m = jnp.maximum(jnp.max(lg, axis=0, keepdims=True), sink)
                p = jnp.exp2(lg - m)
                den = jnp.sum(p, axis=0, keepdims=True) + jnp.exp2(sink - m)
                o = lax.dot_general(vt * keep[half], p.astype(BF16), (((0,), (0,)), ((), ())),
                                    preferred_element_type=F32) * (1.0 / den)
                acc = o if acc is None else acc + o
            for g, t in enumerate(tiles):
                o_ref[qb, t * LANES:(t + 1) * LANES, :] = (
                    acc[:, g * BLOCK:(g + 1) * BLOCK].astype(o_ref.dtype))


def _attention(qkv, bias_t, sink_row):
    bsz, seq, _ = qkv.shape
    nb = seq // BLOCK
    dq = N_HEADS * HEAD_DIM
    dkv = 2 * N_KV * HEAD_DIM
    kvb = dq // dkv
    qb = ATTN_QBLOCKS
    steps = nb // qb
    bias_spec = lambda variant: pl.BlockSpec((None, N_KV, 3 * BLOCK, GROUP * BLOCK),
                                             lambda b, n: (variant(n), 0, 0, 0))
    return pl.pallas_call(
        _attn_kernel,
        grid=(bsz, steps),
        in_specs=[
            pl.BlockSpec((None, qb * BLOCK, dq), lambda b, n: (b, n, 0)),
            pl.BlockSpec((None, BLOCK, dkv), lambda b, n: (b, jnp.maximum(qb * n - 1, 0), kvb)),
            pl.BlockSpec((None, qb * BLOCK, dkv), lambda b, n: (b, n, kvb)),
            pl.BlockSpec((None, BLOCK, dkv), lambda b, n: (b, jnp.minimum(qb * n + qb, nb - 1), kvb)),
            bias_spec(lambda n: jnp.where(n == 0, 0, 1)),
            bias_spec(lambda n: jnp.where(n == steps - 1, 2, 1)),
            pl.BlockSpec((N_KV, 1, GROUP * BLOCK), lambda b, n: (0, 0, 0)),
        ],
        out_specs=pl.BlockSpec((None, qb, dq, BLOCK), lambda b, n: (b, n, 0, 0)),
        out_shape=jax.ShapeDtypeStruct((bsz, nb, dq, BLOCK), BF16),
        compiler_params=_cparams(("parallel", "parallel")),
        name="window_attention",
    )(qkv, qkv, qkv, qkv, bias_t, bias_t, sink_row)


def _attn_tail(tile, x_ref, ot_ref, w_ref):
    ys = [lax.dot_general(ot_ref[k], w_ref[...], (((0,), (0,)), ((), ())), preferred_element_type=F32)
          for k in range(FFN_TM // BLOCK)]
    return x_ref[...] + jnp.concatenate(ys, axis=0)


def _attn_tail_specs(tile):
    dq = N_HEADS * HEAD_DIM
    return [
        pl.BlockSpec((FFN_TM // BLOCK, dq, BLOCK), lambda s: (tile(s), 0, 0)),
        pl.BlockSpec((dq, D_MODEL), lambda s: (0, 0)),
    ]


def _head_tile_perm():
    cols = []
    for tile in range(N_HEADS // 2):
        j, g = tile // GROUP, tile % GROUP
        for half in range(2):
            h = (2 * j + half) * GROUP + g
            cols.extend(range(h * HEAD_DIM, (h + 1) * HEAD_DIM))
    return jnp.asarray(cols, jnp.int32)


def _attn_mixer(x, g_mix, w_qkv, q_gain, k_gain, sink, w_o, rel_table):
    bsz, seq, d = x.shape
    x2 = x.reshape(bsz * seq, d)
    dq = N_HEADS * HEAD_DIM
    dk = N_KV * HEAD_DIM
    perm = _head_tile_perm()
    w = jnp.concatenate([w_qkv[:, :dq][:, perm], w_qkv[:, dq:]], axis=1).astype(BF16)
    log2e = math.log2(math.e)
    gain = jnp.concatenate([jnp.tile(q_gain * (HEAD_DIM ** -0.5 * log2e), N_HEADS), jnp.tile(k_gain, N_KV),
                            jnp.ones((dk,), F32)])[None]
    flag = jnp.concatenate([jnp.ones((dq + dk,), F32), jnp.zeros((dk,), F32)])[None]
    qkv = _qkv_proj(x2, g_mix[None], w, gain, flag)
    bias_t = _rel_bias(rel_table).reshape(N_KV, GROUP * BLOCK, 3 * BLOCK).transpose(0, 2, 1)
    key = jnp.arange(3 * BLOCK)[None, :, None]
    bias_t = jnp.stack([jnp.where(key < BLOCK, -1e30, bias_t), bias_t,
                        jnp.where(key >= 2 * BLOCK, -1e30, bias_t)]) * log2e
    sink_row = jnp.repeat(sink * log2e, BLOCK).reshape(N_KV, 1, GROUP * BLOCK)
    o_t = _attention(qkv.reshape(bsz, seq, dq + 2 * dk), bias_t, sink_row)
    return (o_t.reshape(bsz * (seq // BLOCK), dq, BLOCK), w_o[perm, :].astype(BF16))


def _pool_tail(tile, xc_ref, xp_ref, xn_ref, g_ref, w_ref, b_ref, sc_ref):
    tm = FFN_TM
    g = g_ref[...]
    rows = tm + 2 * SUBLANES
    t0 = (tile % FFN_TILES_PER_SEQ) * tm
    t = t0 - SUBLANES + lax.broadcasted_iota(jnp.int32, (rows, 1), 0)
    inside = (t >= 0) & (t < SEQ)
    xa = jnp.concatenate([_rms(xp_ref[...], g), _rms(xc_ref[...], g), _rms(xn_ref[...], g)], axis=0)
    xa = jnp.where(inside, xa, 0.0)
    tc = t[SUBLANES:SUBLANES + tm]
    outs = []
    for gi, wdt in enumerate(POOL_WINDOWS):
        r = wdt // 2
        xg = xa[:, gi * POOL_GROUP:(gi + 1) * POOL_GROUP]
        run = xg
        span = 1
        while span < 2 * r:
            run = run + pltpu.roll(run, rows - span, 0)
            span *= 2
        win = (pltpu.roll(run, r, 0) + pltpu.roll(xg, rows - r, 0))[SUBLANES:SUBLANES + tm]
        cnt = (jnp.minimum(tc + r + 1, SEQ) - jnp.maximum(tc - r, 0)).astype(F32)
        dlt = win * (1.0 / cnt) - xg[SUBLANES:SUBLANES + tm]
        outs.append(jnp.dot(dlt.astype(BF16), w_ref[gi], preferred_element_type=F32))
    y = (jnp.concatenate(outs, axis=-1) + b_ref[...]) * sc_ref[...]
    return xc_ref[...] + y


def _pool_tail_specs(tile, rows):
    d = D_MODEL
    hb = FFN_TM // SUBLANES
    last = rows // SUBLANES - 1
    const = lambda s: (0, 0)
    return [
        pl.BlockSpec((SUBLANES, d), lambda s: (jnp.maximum(tile(s) * hb - 1, 0), 0)),
        pl.BlockSpec((SUBLANES, d), lambda s: (jnp.minimum((tile(s) + 1) * hb, last), 0)),
        pl.BlockSpec((1, d), const),
        pl.BlockSpec((len(POOL_WINDOWS), POOL_GROUP, POOL_GROUP), lambda s: (0, 0, 0)),
        pl.BlockSpec((1, d), const),
        pl.BlockSpec((1, d), const),
    ]


def kernel(x, norm_mix, norm_ffn, hy_w_in, hy_b_in, hy_conv_w, hy_conv_b, hy_f_w1, hy_f_b1, hy_f_wi, hy_f_bi, hy_f_w3, hy_f_freq, hy_f_bias, hy_w_out, hy_b_out, at_w_qkv, at_q_gain, at_k_gain, at_sink, at_w_o, rel_table, pl_w, pl_b, pl_scale, ff_w_gate, ff_w_up, ff_w_down):
    bsz, seq, d = x.shape
    f1, gm = _dft_mats()
    f1_full = f1.astype(BF16)
    f1_half = f1[:, :FFT_N1 // 2].astype(BF16)
    fi = (f1[:, :FFT_N1 // 2].T * (2.0 / FFT_N)).astype(BF16)
    gh = jnp.swapaxes(gm, 1, 2).astype(BF16)
    gm = gm.astype(BF16)
    fq = jnp.linspace(1e-4, HY_BANDS - 1, HY_BANDS, dtype=F32)[None, :]
    deltas = jnp.abs(jnp.linspace(HY_MIN_DECAY, HY_MAX_DECAY, D_MODEL, dtype=F32))
    dl2 = jnp.concatenate([deltas, deltas])[None, :]
    consts = (f1_full, f1_half, gm, gh, fi, fq, dl2)

    x2 = x.reshape(bsz * seq, d)
    for i in range(DEPTH):
        kind, s = i % 3, i // 3
        x = x2.reshape(bsz, seq, d)
        if kind == 0:
            args = _hyena_mixer(x, norm_mix[i], hy_w_in[s], hy_b_in[s], hy_conv_w[s], hy_conv_b[s],
                                hy_f_w1[s], hy_f_b1[s], hy_f_wi[s], hy_f_bi[s], hy_f_w3[s],
                                hy_f_freq[s], hy_f_bias[s], hy_w_out[s], hy_b_out[s], consts)
            tail, specs = _hyena_tail, _hyena_tail_specs
        elif kind == 1:
            args = _attn_mixer(x, norm_mix[i], at_w_qkv[s], at_q_gain[s], at_k_gain[s], at_sink[s],
                               at_w_o[s], rel_table)
            tail, specs = _attn_tail, _attn_tail_specs
        else:
            args = (x2, x2, norm_mix[i][None], pl_w[s].astype(BF16), pl_b[s][None], pl_scale[s][None])
            tail, specs = _pool_tail, functools.partial(_pool_tail_specs, rows=bsz * seq)
        x2 = _ffn(x2, norm_ffn[i][None], ff_w_gate, ff_w_up, ff_w_down, i, tail, args, specs)
    return x2.reshape(bsz, seq, d)
```

```python
import functools
import math

import jax
import jax.numpy as jnp
from jax import lax
from jax.experimental import pallas as pl
from jax.experimental.pallas import tpu as pltpu

F32 = jnp.float32
BF16 = jnp.bfloat16

D_MODEL = 1024
SEQ = 8192
DEPTH = 4
EPS = 1e-6

HY_SHORT = 3
HY_BANDS = 16
HY_FW = 64
HY_INNER = 2
HY_MIN_DECAY = math.log(1e-2) / 1.5
HY_MAX_DECAY = math.log(1e-2) / 0.3

HEAD_DIM = 64
N_HEADS = 16
N_KV = 4
GROUP = 4
WINDOW = 128
BLOCK = 128
REL_BUCKETS = 32
REL_MAX_DIST = 128

POOL_WINDOWS = (2, 4, 8, 16)
POOL_GROUP = 256
D_FF = 2816

SUBLANES = 8
LANES = 128
VMEM_LIMIT = 56 * 1024 * 1024

FFT_N = 2 * SEQ
FFT_N1 = 256
FFT_N2 = 64
FFT_K1 = FFT_N1 // 2
SLAB = 2 * FFT_K1 + SUBLANES
CONV_DT = 128
DFT_UNROLL = 32


def _cparams(sem):
    return pltpu.CompilerParams(dimension_semantics=sem, vmem_limit_bytes=VMEM_LIMIT)


def _rms(x, g):
    return x * lax.rsqrt(jnp.mean(x * x, axis=-1, keepdims=True) + EPS) * g


FFN_CHUNK = 256
FFN_CHUNKS = D_FF // FFN_CHUNK
FFN_TM = 512
FFN_TILES_PER_SEQ = SEQ // FFN_TM


def _ffn_kernel(*refs, tail, n_tail):
    x_ref, g_ref, wg_ref, wu_ref, wd_ref = refs[:5]
    tail_refs = refs[5:5 + n_tail]
    o_ref, wgb_ref, wub_ref, wdb_ref, h_ref = refs[5 + n_tail:]
    s = pl.program_id(0)

    @pl.when(s < FFN_CHUNKS)
    def _():
        wgb_ref[s] = wg_ref[...].astype(BF16)
        wub_ref[s] = wu_ref[...].astype(BF16)
        wdb_ref[pl.ds(pl.multiple_of(s * FFN_CHUNK, FFN_CHUNK), FFN_CHUNK), :] = wd_ref[...].astype(BF16)

    @pl.when(s >= FFN_CHUNKS)
    def _():
        x = tail(s - FFN_CHUNKS, x_ref, *tail_refs)
        xn = _rms(x, g_ref[...]).astype(BF16)
        for c in range(FFN_CHUNKS):
            gate = jnp.dot(xn, wgb_ref[c], preferred_element_type=F32)
            up = jnp.dot(xn, wub_ref[c], preferred_element_type=F32)
            h_ref[:, c * FFN_CHUNK:(c + 1) * FFN_CHUNK] = (gate * jax.nn.sigmoid(gate) * up).astype(BF16)
        o_ref[...] = x + jnp.dot(h_ref[...], wdb_ref[...], preferred_element_type=F32)


def _ffn(x2, g, wg, wu, wd, layer, tail, tail_args, tail_specs):
    m, d = x2.shape
    f = wg.shape[2]
    tm = FFN_TM
    tile = lambda s: jnp.maximum(s - FFN_CHUNKS, 0)
    chunk = lambda s: jnp.minimum(s, FFN_CHUNKS - 1)
    return pl.pallas_call(
        functools.partial(_ffn_kernel, tail=tail, n_tail=len(tail_args)),
        grid=(FFN_CHUNKS + m // tm,),
        in_specs=[
            pl.BlockSpec((tm, d), lambda s: (tile(s), 0)),
            pl.BlockSpec((1, d), lambda s: (0, 0)),
            pl.BlockSpec((None, d, FFN_CHUNK), lambda s: (layer, 0, chunk(s))),
            pl.BlockSpec((None, d, FFN_CHUNK), lambda s: (layer, 0, chunk(s))),
            pl.BlockSpec((None, FFN_CHUNK, d), lambda s: (layer, chunk(s), 0)),
        ] + tail_specs(tile),
        out_specs=pl.BlockSpec((tm, d), lambda s: (tile(s), 0)),
        out_shape=jax.ShapeDtypeStruct((m, d), F32),
        scratch_shapes=[pltpu.VMEM((FFN_CHUNKS, d, FFN_CHUNK), BF16),
                        pltpu.VMEM((FFN_CHUNKS, d, FFN_CHUNK), BF16),
                        pltpu.VMEM((f, d), BF16),
                        pltpu.VMEM((tm, f), BF16)],
        compiler_params=_cparams(("arbitrary",)),
        name="mixer_tail_ffn",
    )(x2, g, wg, wu, wd, *tail_args)


INPROJ_N1 = 16
INPROJ_CHUNK = 512


def _slab_perm(nb):
    r = jnp.arange(FFT_N2 * nb)
    src = FFT_N2 * (r % nb) + r // nb
    return (src[:, None] == r[None, :]).astype(BF16)


def _inproj_kernel(xp_ref, xc_ref, xn_ref, g_ref, pm_ref, w_ref, b_ref, cw_ref, cb_ref, o_ref, xs_ref):
    i = pl.program_id(1)
    nb = INPROJ_N1
    tm = FFT_N2 * nb
    g = g_ref[...]
    xn = _rms(xc_ref[...], g).astype(BF16)
    xs_ref[:tm, :] = jnp.dot(pm_ref[...], xn, preferred_element_type=F32).astype(BF16)
    halo = jnp.concatenate([xp_ref[...], xn_ref[...]], axis=0)
    xs_ref[tm:, :] = _rms(halo, g).astype(BF16)

    xs = xs_ref[...]
    row = lax.broadcasted_iota(jnp.int32, (nb, 1), 0)
    inside_before = i > 0
    inside_after = i < pl.num_programs(1) - 1
    for c in range(w_ref.shape[1] // INPROJ_CHUNK):
        cols = slice(c * INPROJ_CHUNK, (c + 1) * INPROJ_CHUNK)
        p = jnp.dot(xs, w_ref[:, cols], preferred_element_type=F32) + b_ref[:, cols]
        cur = p[:tm]
        before = jnp.where(inside_before, p[tm + SUBLANES - 1:tm + SUBLANES], 0.0)
        after = jnp.where(inside_after, p[tm + SUBLANES:tm + SUBLANES + 1], 0.0)
        first = jnp.where(row == 0, before, pltpu.roll(cur[tm - nb:], 1, 0))
        last = jnp.where(row == nb - 1, after, pltpu.roll(cur[:nb], nb - 1, 0))
        dn = jnp.concatenate([first, cur[:tm - nb]], axis=0)
        up = jnp.concatenate([cur[nb:], last], axis=0)
        cw = cw_ref[:, cols]
        out = cb_ref[:, cols] + cw[0:1] * dn + cw[1:2] * cur + cw[2:3] * up
        o_ref[:, :, cols] = out.reshape(FFT_N2, nb, INPROJ_CHUNK).astype(o_ref.dtype)


def _inproj_conv(x, g, w, b, cw, cb):
    bsz, seq, d = x.shape
    n = w.shape[1]
    nb = INPROJ_N1
    tm = FFT_N2 * nb
    hb = tm // SUBLANES
    last = seq // SUBLANES - 1
    const = lambda bi, i: (0, 0)
    return pl.pallas_call(
        _inproj_kernel,
        grid=(bsz, seq // tm),
        in_specs=[
            pl.BlockSpec((None, SUBLANES, d), lambda bi, i: (bi, jnp.maximum(i * hb - 1, 0), 0)),
            pl.BlockSpec((None, tm, d), lambda bi, i: (bi, i, 0)),
            pl.BlockSpec((None, SUBLANES, d), lambda bi, i: (bi, jnp.minimum((i + 1) * hb, last), 0)),
            pl.BlockSpec((1, d), const),
            pl.BlockSpec((tm, tm), const, pipeline_mode=pl.Buffered(1)),
            pl.BlockSpec((d, n), const, pipeline_mode=pl.Buffered(1)),
            pl.BlockSpec((1, n), const),
            pl.BlockSpec((HY_SHORT, n), const),
            pl.BlockSpec((1, n), const),
        ],
        out_specs=pl.BlockSpec((None, FFT_N2, nb, n), lambda bi, i: (bi, 0, i, 0)),
        out_shape=jax.ShapeDtypeStruct((bsz, FFT_N2, seq // FFT_N2, n), BF16),
        scratch_shapes=[pltpu.VMEM((tm + 2 * SUBLANES, d), BF16)],
        compiler_params=_cparams(("parallel", "parallel")),
        name="hyena_inproj",
    )(x, x, x, g, _slab_perm(nb), w, b, cw, cb)


def _dft_mats():
    i32 = jnp.int32
    k1 = jnp.arange(FFT_K1, dtype=i32)[:, None]
    n1 = jnp.arange(FFT_N1, dtype=i32)[None, :]
    m = ((2 * k1 + 1) * n1) % (2 * FFT_N1)
    ang = m.astype(F32) * (math.pi / FFT_N1)
    f1 = jnp.concatenate([jnp.cos(ang), -jnp.sin(ang)], axis=0)
    k1 = jnp.arange(FFT_K1, dtype=i32)[:, None, None]
    k2 = jnp.arange(FFT_N2, dtype=i32)[None, :, None]
    n2 = jnp.arange(FFT_N2, dtype=i32)[None, None, :]
    m = ((2 * (k1 + FFT_N1 * k2) + 1) * n2) % (2 * FFT_N)
    ang = m.astype(F32) * (math.pi / FFT_N)
    c, s = jnp.cos(ang), jnp.sin(ang)
    gm = jnp.concatenate(
        [jnp.concatenate([c, s], axis=2), jnp.concatenate([-s, c], axis=2)], axis=1)
    return f1, gm


def _dot3(a3, b):
    hi, lo = _split_bf16(b)
    return jnp.dot(a3, jnp.concatenate([hi, hi, lo], axis=0), preferred_element_type=F32)


def _stack3_cols(a):
    hi, lo = _split_bf16(a)
    return jnp.concatenate([hi, lo, hi], axis=-1)


def _tap_lag(n):
    return jnp.where(n >= SEQ, FFT_N - n, n).astype(F32)


def _split_bf16(a):
    hi = a.astype(BF16)
    return hi, (a - hi.astype(F32)).astype(BF16)


def _filter_mlp_kernel(fq_ref, w1t_ref, w1b_ref, b1_ref, wi_ref, bi_ref, fr_ref, h3_ref, *, slabs):
    cols = slabs * FFT_N1
    c = lax.broadcasted_iota(jnp.int32, (1, cols), 1)
    n2 = pl.program_id(0) * slabs + c // FFT_N1
    pos = _tap_lag(FFT_N2 * (c % FFT_N1) + n2)
    t = pos * (1.0 / (SEQ - 1))
    ang = (2.0 * math.pi / SEQ) * pos * fq_ref[...]
    fr = fr_ref[...]
    bands = jnp.concatenate([jnp.cos(ang), -jnp.sin(ang)], axis=0)
    pre = w1t_ref[...] * t + _dot3(w1b_ref[...], bands)
    h = jnp.sin(fr * (pre + b1_ref[...]))
    for l in range(HY_INNER):
        h = jnp.sin(fr * (_dot3(wi_ref[l], h) + bi_ref[l]))
    hi, lo = _split_bf16(h)
    h3 = jnp.concatenate([hi, lo, hi], axis=0)
    for k in range(slabs):
        h3_ref[k] = h3[:, k * FFT_N1:(k + 1) * FFT_N1]


def _filter_mlp(fq, w1, b1, wi, bi, fr, slabs=8):
    w1t, w1b = w1[0:1].T, _stack3_cols(w1[1:].T)
    wi3 = _stack3_cols(jnp.swapaxes(wi, 1, 2))
    col = lambda a: a.reshape(-1, 1)
    const2 = lambda i: (0, 0)
    const3 = lambda i: (0, 0, 0)
    return pl.pallas_call(
        functools.partial(_filter_mlp_kernel, slabs=slabs),
        grid=(FFT_N2 // slabs,),
        in_specs=[
            pl.BlockSpec((HY_BANDS, 1), const2),
            pl.BlockSpec((HY_FW, 1), const2),
            pl.BlockSpec((HY_FW, 3 * 2 * HY_BANDS), const2),
            pl.BlockSpec((HY_FW, 1), const2),
            pl.BlockSpec((HY_INNER, HY_FW, 3 * HY_FW), const3),
            pl.BlockSpec((HY_INNER, HY_FW, 1), const3),
            pl.BlockSpec((HY_FW, 1), const2),
        ],
        out_specs=pl.BlockSpec((slabs, 3 * HY_FW, FFT_N1), lambda i: (i, 0, 0)),
        out_shape=jax.ShapeDtypeStruct((FFT_N2, 3 * HY_FW, FFT_N1), BF16),
        compiler_params=_cparams(("parallel",)),
        name="hyena_filter_mlp",
    )(col(fq), w1t, w1b, col(b1), wi3, bi[:, :, None], col(fr))


def _slab_store(s_ref, n2, a):
    rows = pl.ds(pl.multiple_of(n2 * SLAB, SUBLANES), 2 * FFT_K1)
    for l in range(s_ref.shape[0]):
        s_ref[l, rows, :] = a[:, l * LANES:(l + 1) * LANES]


def _slab_load(s_ref, n2):
    rows = pl.ds(pl.multiple_of(n2 * SLAB, SUBLANES), 2 * FFT_K1)
    return jnp.concatenate([s_ref[l, rows, :] for l in range(s_ref.shape[0])], axis=1).astype(BF16)


def _k1_load(s_ref, k1):
    re = pl.ds(k1, FFT_N2, stride=SLAB)
    im = pl.ds(FFT_K1 + k1, FFT_N2, stride=SLAB)
    tiles = [jnp.concatenate([s_ref[l, re, :], s_ref[l, im, :]], axis=0) for l in range(s_ref.shape[0])]
    return jnp.concatenate(tiles, axis=1).astype(BF16)


def _k1_store(s_ref, k1, b):
    for l in range(s_ref.shape[0]):
        cols = slice(l * LANES, (l + 1) * LANES)
        s_ref[l, pl.ds(k1, FFT_N2, stride=SLAB), :] = b[:FFT_N2, cols]
        s_ref[l, pl.ds(FFT_K1 + k1, FFT_N2, stride=SLAB), :] = b[FFT_N2:, cols]


def _stage_scratch(dt):
    return pltpu.VMEM((dt // LANES, FFT_N2 * SLAB, LANES), F32)


def _stage1(src_ref, f1_ref, s_ref):
    def body(n2, c):
        _slab_store(s_ref, n2, jnp.dot(f1_ref[...], src_ref[n2].astype(BF16), preferred_element_type=F32))
        return c
    lax.fori_loop(0, FFT_N2, body, 0, unroll=DFT_UNROLL)


def _filt_dft_kernel(h3_ref, w3_ref, dl_ref, f1_ref, gm_ref, kf_ref, s_ref):
    half = FFT_N1 // 2
    dt = dl_ref.shape[1]
    n1 = lax.broadcasted_iota(jnp.int32, (FFT_N1, 1), 0)
    dl = dl_ref[...]

    def body(n2, l1):
        both = lax.dot_general(h3_ref[n2], w3_ref[...], (((0,), (0,)), ((), ())),
                               preferred_element_type=F32)
        fwd = both[:half, :dt]
        bwd = both[half:, dt:]
        n = FFT_N2 * n1 + n2
        t = _tap_lag(n) * (1.0 / (SEQ - 1))
        taps = jnp.concatenate([fwd, -bwd], axis=0) * jnp.exp(-t * dl)
        taps = jnp.where(n == SEQ, 0.0, taps)
        _slab_store(s_ref, n2, jnp.dot(f1_ref[...], taps.astype(BF16), preferred_element_type=F32))
        return l1 + jnp.sum(jnp.abs(taps), axis=0, keepdims=True)
    l1 = lax.fori_loop(0, FFT_N2, body, jnp.zeros(dl.shape, F32), unroll=DFT_UNROLL)
    inv = 1.0 / l1

    def stage2(k1, c):
        x = jnp.dot(gm_ref[k1], _k1_load(s_ref, k1), preferred_element_type=F32)
        rows = pl.ds(pl.multiple_of(k1 * 2 * FFT_N2, 2 * FFT_N2), 2 * FFT_N2)
        kf_ref[rows, :] = x * inv
        return c
    lax.fori_loop(0, FFT_K1, stage2, 0, unroll=DFT_UNROLL)


def _filter_spectrum(h3, w3, dl2, f1_full, gm):
    ncol = 2 * D_MODEL
    dt = CONV_DT
    nct = ncol // dt
    tiles = lambda a: a.reshape(HY_FW, 2, nct, dt).transpose(2, 0, 1, 3).reshape(nct, HY_FW, 2 * dt)
    w3h, w3l = _split_bf16(w3)
    w3cat = jnp.concatenate([tiles(w3h), tiles(w3h), tiles(w3l)], axis=1)
    return pl.pallas_call(
        _filt_dft_kernel,
        grid=(nct,),
        in_specs=[
            pl.BlockSpec((FFT_N2, 3 * HY_FW, FFT_N1), lambda j: (0, 0, 0)),
            pl.BlockSpec((None, 3 * HY_FW, 2 * dt), lambda j: (j, 0, 0)),
            pl.BlockSpec((1, dt), lambda j: (0, j)),
            pl.BlockSpec((2 * FFT_K1, FFT_N1), lambda j: (0, 0)),
            pl.BlockSpec((FFT_K1, 2 * FFT_N2, 2 * FFT_N2), lambda j: (0, 0, 0),
                         pipeline_mode=pl.Buffered(1)),
        ],
        out_specs=pl.BlockSpec((FFT_K1 * 2 * FFT_N2, dt), lambda j: (0, j)),
        out_shape=jax.ShapeDtypeStruct((FFT_K1 * 2 * FFT_N2, ncol), F32),
        scratch_shapes=[_stage_scratch(dt)],
        compiler_params=_cparams(("parallel",)),
        name="hyena_filter_spectrum",
    )(h3, w3cat, dl2, f1_full, gm)


def _longconv_kernel(z_ref, g_ref, fb_ref, kf_ref, f1_ref, gm_ref, gh_ref, fi_ref, o_ref, s_ref):
    _stage1(z_ref, f1_ref, s_ref)

    def mid(k1, c):
        x = jnp.dot(gm_ref[k1], _k1_load(s_ref, k1), preferred_element_type=F32)
        kf = kf_ref[pl.ds(pl.multiple_of(k1 * 2 * FFT_N2, 2 * FFT_N2), 2 * FFT_N2), :]
        xr, xi = x[:FFT_N2], x[FFT_N2:]
        kr, ki = kf[:FFT_N2], kf[FFT_N2:]
        y = jnp.concatenate([xr * kr - xi * ki, xr * ki + xi * kr], axis=0).astype(BF16)
        b = jnp.dot(gh_ref[k1], y, preferred_element_type=F32)
        _k1_store(s_ref, k1, b)
        return c
    lax.fori_loop(0, FFT_K1, mid, 0, unroll=2 * DFT_UNROLL)

    fb = fb_ref[...]

    def last(n2, c):
        y = jnp.dot(fi_ref[...], _slab_load(s_ref, n2), preferred_element_type=F32)
        o_ref[n2] = (g_ref[n2] * (y + fb * z_ref[n2])).astype(o_ref.dtype)
        return c
    lax.fori_loop(0, FFT_N2, last, 0, unroll=DFT_UNROLL)


def _long_conv_gate(z, zcol, g, gcol, fb, kf, kcol, f1_half, gm, gh, fi):
    bsz = z.shape[0]
    dt = CONV_DT
    nt = D_MODEL // dt
    n1 = FFT_N1 // 2
    const3 = lambda j, bi: (0, 0, 0)
    return pl.pallas_call(
        _longconv_kernel,
        grid=(nt, bsz),
        in_specs=[
            pl.BlockSpec((None, FFT_N2, n1, dt), lambda j, bi: (bi, 0, 0, zcol + j)),
            pl.BlockSpec((None, FFT_N2, n1, dt), lambda j, bi: (bi, 0, 0, gcol + j)),
            pl.BlockSpec((1, dt), lambda j, bi: (0, j)),
            pl.BlockSpec((FFT_K1 * 2 * FFT_N2, dt), lambda j, bi: (0, kcol + j)),
            pl.BlockSpec((2 * FFT_K1, FFT_N1 // 2), lambda j, bi: (0, 0)),
            pl.BlockSpec((FFT_K1, 2 * FFT_N2, 2 * FFT_N2), const3, pipeline_mode=pl.Buffered(1)),
            pl.BlockSpec((FFT_K1, 2 * FFT_N2, 2 * FFT_N2), const3, pipeline_mode=pl.Buffered(1)),
            pl.BlockSpec((FFT_N1 // 2, 2 * FFT_K1), lambda j, bi: (0, 0)),
        ],
        out_specs=pl.BlockSpec((None, FFT_N2, n1, dt), lambda j, bi: (bi, 0, 0, j)),
        out_shape=jax.ShapeDtypeStruct((bsz, FFT_N2, n1, D_MODEL), BF16),
        scratch_shapes=[_stage_scratch(dt)],
        compiler_params=_cparams(("parallel", "arbitrary")),
        name="hyena_long_conv",
    )(z, g, fb, kf, f1_half, gm, gh, fi)


def _hyena_tail(tile, x_ref, z_ref, pm_ref, w_ref, b_ref):
    z = z_ref[...].reshape(x_ref.shape)
    a = jnp.dot(pm_ref[...], z, preferred_element_type=F32).astype(BF16)
    return x_ref[...] + jnp.dot(a, w_ref[...], preferred_element_type=F32) + b_ref[...]


def _hyena_tail_specs(tile):
    nb = FFN_TM // FFT_N2
    d = D_MODEL
    const = lambda s: (0, 0)
    return [
        pl.BlockSpec((None, FFT_N2, nb, d),
                     lambda s: (tile(s) // FFN_TILES_PER_SEQ, 0, tile(s) % FFN_TILES_PER_SEQ, 0)),
        pl.BlockSpec((FFN_TM, FFN_TM), const),
        pl.BlockSpec((d, d), const),
        pl.BlockSpec((1, d), const),
    ]


def _hyena_mixer(x, g_mix, w_in, b_in, cw, cb, fw1, fb1, fwi, fbi, fw3, freq, fbias, w_out, b_out, consts):
    f1_full, f1_half, gm, gh, fi, fq, dl2 = consts
    pc = _inproj_conv(x, g_mix[None], w_in.astype(BF16), b_in[None], cw, cb[None])
    h3 = _filter_mlp(fq, fw1, fb1, fwi, fbi, freq)
    kf = _filter_spectrum(h3, fw3, dl2, f1_full, gm)
    nt = D_MODEL // CONV_DT
    z1 = _long_conv_gate(pc, 2 * nt, pc, 0, fbias[0:1], kf, 0, f1_half, gm, gh, fi)
    z2 = _long_conv_gate(z1, 0, pc, nt, fbias[1:2], kf, nt, f1_half, gm, gh, fi)
    return (z2, _slab_perm(FFN_TM // FFT_N2).T, w_out.astype(BF16), b_out[None])


def _relbias_kernel(tab_ref, bucket_ref, band_ref, o_ref):
    bucket = bucket_ref[...]
    band = band_ref[...]
    for h in range(N_HEADS):
        acc = jnp.zeros(bucket.shape, F32)
        for k in range(REL_BUCKETS):
            acc = jnp.where(bucket == k, tab_ref[k, h], acc)
        o_ref[h] = jnp.where(band > 0, acc, -1e30)


def _rel_bias(rel_table):
    a = jnp.arange(BLOCK)[:, None]
    j = jnp.arange(3 * BLOCK)[None, :]
    rel = j - BLOCK - a
    half = REL_BUCKETS // 2
    exact = half // 2
    n = jnp.abs(rel)
    nf = jnp.maximum(n, 1).astype(F32)
    large = exact + (jnp.log(nf / exact) / math.log(REL_MAX_DIST / exact) * (half - exact)).astype(jnp.int32)
    large = jnp.minimum(large, half - 1)
    bucket = (jnp.where(rel > 0, half, 0) + jnp.where(n < exact, n, large)).astype(jnp.int32)
    band = (n <= WINDOW).astype(jnp.int32)
    return pl.pallas_call(
        _relbias_kernel,
        in_specs=[
            pl.BlockSpec(memory_space=pltpu.SMEM),
            pl.BlockSpec((BLOCK, 3 * BLOCK), lambda: (0, 0)),
            pl.BlockSpec((BLOCK, 3 * BLOCK), lambda: (0, 0)),
        ],
        out_specs=pl.BlockSpec((N_HEADS, BLOCK, 3 * BLOCK), lambda: (0, 0, 0)),
        out_shape=jax.ShapeDtypeStruct((N_HEADS, BLOCK, 3 * BLOCK), F32),
        name="rel_bias",
    )(rel_table, bucket, band)


QKV_CHUNK = 512


def _qkv_kernel(x_ref, g_ref, w_ref, gain_ref, flag_ref, bd_ref, o_ref):
    xn = _rms(x_ref[...], g_ref[...]).astype(BF16)
    for c in range(w_ref.shape[1] // QKV_CHUNK):
        cols = slice(c * QKV_CHUNK, (c + 1) * QKV_CHUNK)
        acc = jnp.dot(xn, w_ref[:, cols], preferred_element_type=F32)
        ssq = jnp.dot((acc * acc).astype(BF16), bd_ref[...], preferred_element_type=F32)
        normed = acc * lax.rsqrt(ssq * (1.0 / HEAD_DIM) + EPS) * gain_ref[:, cols]
        o_ref[:, cols] = jnp.where(flag_ref[:, cols] > 0, normed, acc).astype(o_ref.dtype)


def _qkv_proj(x2, g, w, gain, flag, tm=1024):
    m, k = x2.shape
    n = w.shape[1]
    tn = QKV_CHUNK
    eye = jnp.arange(tn)[:, None] // HEAD_DIM == jnp.arange(tn)[None, :] // HEAD_DIM
    const = lambda i: (0, 0)
    return pl.pallas_call(
        _qkv_kernel,
        grid=(m // tm,),
        in_specs=[
            pl.BlockSpec((tm, k), lambda i: (i, 0)),
            pl.BlockSpec((1, k), const),
            pl.BlockSpec((k, n), const),
            pl.BlockSpec((1, n), const),
            pl.BlockSpec((1, n), const),
            pl.BlockSpec((tn, tn), const),
        ],
        out_specs=pl.BlockSpec((tm, n), lambda i: (i, 0)),
        out_shape=jax.ShapeDtypeStruct((m, n), BF16),
        compiler_params=_cparams(("parallel",)),
        name="qkv_proj",
    )(x2, g, w, gain, flag, eye.astype(BF16))


ATTN_QBLOCKS = 4


def _attn_kernel(q_ref, kp_ref, kc_ref, kn_ref, bias_first_ref, bias_mid_ref, bias_last_ref, sink_ref,
                 o_ref):
    kvall = jnp.concatenate([kp_ref[...], kc_ref[...], kn_ref[...]], axis=0)
    lane = lax.broadcasted_iota(jnp.int32, (1, LANES), 1)
    keep = [(lane < HEAD_DIM).astype(F32).astype(BF16), (lane >= HEAD_DIM).astype(F32).astype(BF16)]
    ntile = N_KV // 2
    bias_refs = [bias_first_ref] + [bias_mid_ref] * (ATTN_QBLOCKS - 2) + [bias_last_ref]
    for qb, bias_ref in enumerate(bias_refs):
        kv = kvall[qb * BLOCK:(qb + 3) * BLOCK]
        rows = slice(qb * BLOCK, (qb + 1) * BLOCK)
        for j in range(ntile):
            kt = kv[:, j * LANES:(j + 1) * LANES]
            vt = kv[:, (ntile + j) * LANES:(ntile + j + 1) * LANES]
            tiles = [GROUP * j + g for g in range(GROUP)]
            qs = jnp.concatenate([q_ref[rows, t * LANES:(t + 1) * LANES] for t in tiles], axis=0)
            acc = None
            for half in range(2):
                kh = 2 * j + half
                lg = lax.dot_general(kt * keep[half], qs, (((1,), (1,)), ((), ())),
                                     preferred_element_type=F32)
                lg = lg + bias_ref[kh]
                sink = sink_ref[kh]
                m = jnp.maximum(jnp.max(lg, axis=0, keepdims=True), sink)
                p = jnp.exp2(lg - m)
                den = jnp.sum(p, axis=0, keepdims=True) + jnp.exp2(sink - m)
                o = lax.dot_general(vt * keep[half], p.astype(BF16), (((0,), (0,)), ((), ())),
                                    preferred_element_type=F32) * (1.0 / den)
                acc = o if acc is None else acc + o
            for g, t in enumerate(tiles):
                o_ref[qb, t * LANES:(t + 1) * LANES, :] = (
                    acc[:, g * BLOCK:(g + 1) * BLOCK].astype(o_ref.dtype))


def _attention(qkv, bias_t, sink_row):
    bsz, seq, _ = qkv.shape
    nb = seq // BLOCK
    dq = N_HEADS * HEAD_DIM
    dkv = 2 * N_KV * HEAD_DIM
    kvb = dq // dkv
    qb = ATTN_QBLOCKS
    steps = nb // qb
    bias_spec = lambda variant: pl.BlockSpec((None, N_KV, 3 * BLOCK, GROUP * BLOCK),
                                             lambda b, n: (variant(n), 0, 0, 0))
    return pl.pallas_call(
        _attn_kernel,
        grid=(bsz, steps),
        in_specs=[
            pl.BlockSpec((None, qb * BLOCK, dq), lambda b, n: (b, n, 0)),
            pl.BlockSpec((None, BLOCK, dkv), lambda b, n: (b, jnp.maximum(qb * n - 1, 0), kvb)),
            pl.BlockSpec((None, qb * BLOCK, dkv), lambda b, n: (b, n, kvb)),
            pl.BlockSpec((None, BLOCK, dkv), lambda b, n: (b, jnp.minimum(qb * n + qb, nb - 1), kvb)),
            bias_spec(lambda n: jnp.where(n == 0, 0, 1)),
            bias_spec(lambda n: 1),
            bias_spec(lambda n: jnp.where(n == steps - 1, 2, 1)),
            pl.BlockSpec((N_KV, 1, GROUP * BLOCK), lambda b, n: (0, 0, 0)),
        ],
        out_specs=pl.BlockSpec((None, qb, dq, BLOCK), lambda b, n: (b, n, 0, 0)),
        out_shape=jax.ShapeDtypeStruct((bsz, nb, dq, BLOCK), BF16),
        compiler_params=_cparams(("parallel", "parallel")),
        name="window_attention",
    )(qkv, qkv, qkv, qkv, bias_t, bias_t, bias_t, sink_row)


def _attn_tail(tile, x_ref, ot_ref, w_ref):
    ys = [lax.dot_general(ot_ref[k], w_ref[...], (((0,), (0,)), ((), ())), preferred_element_type=F32)
          for k in range(FFN_TM // BLOCK)]
    return x_ref[...] + jnp.concatenate(ys, axis=0)


def _attn_tail_specs(tile):
    dq = N_HEADS * HEAD_DIM
    return [
        pl.BlockSpec((FFN_TM // BLOCK, dq, BLOCK), lambda s: (tile(s), 0, 0)),
        pl.BlockSpec((dq, D_MODEL), lambda s: (0, 0)),
    ]


def _head_tile_perm():
    cols = []
    for tile in range(N_HEADS // 2):
        j, g = tile // GROUP, tile % GROUP
        for half in range(2):
            h = (2 * j + half) * GROUP + g
            cols.extend(range(h * HEAD_DIM, (h + 1) * HEAD_DIM))
    return jnp.asarray(cols, jnp.int32)


def _attn_mixer(x, g_mix, w_qkv, q_gain, k_gain, sink, w_o, rel_table):
    bsz, seq, d = x.shape
    x2 = x.reshape(bsz * seq, d)
    dq = N_HEADS * HEAD_DIM
    dk = N_KV * HEAD_DIM
    perm = _head_tile_perm()
    w = jnp.concatenate([w_qkv[:, :dq][:, perm], w_qkv[:, dq:]], axis=1).astype(BF16)
    log2e = math.log2(math.e)
    gain = jnp.concatenate([jnp.tile(q_gain * (HEAD_DIM ** -0.5 * log2e), N_HEADS), jnp.tile(k_gain, N_KV),
                            jnp.ones((dk,), F32)])[None]
    flag = jnp.concatenate([jnp.ones((dq + dk,), F32), jnp.zeros((dk,), F32)])[None]
    qkv = _qkv_proj(x2, g_mix[None], w, gain, flag)
    bias_t = _rel_bias(rel_table).reshape(N_KV, GROUP * BLOCK, 3 * BLOCK).transpose(0, 2, 1)
    key = jnp.arange(3 * BLOCK)[None, :, None]
    bias_t = jnp.stack([jnp.where(key < BLOCK, -1e30, bias_t), bias_t,
                        jnp.where(key >= 2 * BLOCK, -1e30, bias_t)]) * log2e
    sink_row = jnp.repeat(sink * log2e, BLOCK).reshape(N_KV, 1, GROUP * BLOCK)
    o_t = _attention(qkv.reshape(bsz, seq, dq + 2 * dk), bias_t, sink_row)
    return (o_t.reshape(bsz * (seq // BLOCK), dq, BLOCK), w_o[perm, :].astype(BF16))


def _pool_tail(tile, xc_ref, xp_ref, xn_ref, g_ref, w_ref, b_ref, sc_ref):
    tm = FFN_TM
    g = g_ref[...]
    rows = tm + 2 * SUBLANES
    t0 = (tile % FFN_TILES_PER_SEQ) * tm
    t = t0 - SUBLANES + lax.broadcasted_iota(jnp.int32, (rows, 1), 0)
    inside = (t >= 0) & (t < SEQ)
    xa = jnp.concatenate([_rms(xp_ref[...], g), _rms(xc_ref[...], g), _rms(xn_ref[...], g)], axis=0)
    xa = jnp.where(inside, xa, 0.0)
    tc = t[SUBLANES:SUBLANES + tm]
    outs = []
    for gi, wdt in enumerate(POOL_WINDOWS):
        r = wdt // 2
        xg = xa[:, gi * POOL_GROUP:(gi + 1) * POOL_GROUP]
        run = xg
        span = 1
        while span < 2 * r:
            run = run + pltpu.roll(run, rows - span, 0)
            span *= 2
        win = (pltpu.roll(run, r, 0) + pltpu.roll(xg, rows - r, 0))[SUBLANES:SUBLANES + tm]
        cnt = (jnp.minimum(tc + r + 1, SEQ) - jnp.maximum(tc - r, 0)).astype(F32)
        dlt = win * (1.0 / cnt) - xg[SUBLANES:SUBLANES + tm]
        outs.append(jnp.dot(dlt.astype(BF16), w_ref[gi], preferred_element_type=F32))
    y = (jnp.concatenate(outs, axis=-1) + b_ref[...]) * sc_ref[...]
    return xc_ref[...] + y


def _pool_tail_specs(tile, rows):
    d = D_MODEL
    hb = FFN_TM // SUBLANES
    last = rows // SUBLANES - 1
    const = lambda s: (0, 0)
    return [
        pl.BlockSpec((SUBLANES, d), lambda s: (jnp.maximum(tile(s) * hb - 1, 0), 0)),
        pl.BlockSpec((SUBLANES, d), lambda s: (jnp.minimum((tile(s) + 1) * hb, last), 0)),
        pl.BlockSpec((1, d), const),
        pl.BlockSpec((len(POOL_WINDOWS), POOL_GROUP, POOL_GROUP), lambda s: (0, 0, 0)),
        pl.BlockSpec((1, d), const),
        pl.BlockSpec((1, d), const),
    ]


def kernel(x, norm_mix, norm_ffn, hy_w_in, hy_b_in, hy_conv_w, hy_conv_b, hy_f_w1, hy_f_b1, hy_f_wi, hy_f_bi, hy_f_w3, hy_f_freq, hy_f_bias, hy_w_out, hy_b_out, at_w_qkv, at_q_gain, at_k_gain, at_sink, at_w_o, rel_table, pl_w, pl_b, pl_scale, ff_w_gate, ff_w_up, ff_w_down):
    bsz, seq, d = x.shape
    f1, gm = _dft_mats()
    f1_full = f1.astype(BF16)
    f1_half = f1[:, :FFT_N1 // 2].astype(BF16)
    fi = (f1[:, :FFT_N1 // 2].T * (2.0 / FFT_N)).astype(BF16)
    gh = jnp.swapaxes(gm, 1, 2).astype(BF16)
    gm = gm.astype(BF16)
    fq = jnp.linspace(1e-4, HY_BANDS - 1, HY_BANDS, dtype=F32)[None, :]
    deltas = jnp.abs(jnp.linspace(HY_MIN_DECAY, HY_MAX_DECAY, D_MODEL, dtype=F32))
    dl2 = jnp.concatenate([deltas, deltas])[None, :]
    consts = (f1_full, f1_half, gm, gh, fi, fq, dl2)

    x2 = x.reshape(bsz * seq, d)
    for i in range(DEPTH):
        kind, s = i % 3, i // 3
        x = x2.reshape(bsz, seq, d)
        if kind == 0:
            args = _hyena_mixer(x, norm_mix[i], hy_w_in[s], hy_b_in[s], hy_conv_w[s], hy_conv_b[s],
                                hy_f_w1[s], hy_f_b1[s], hy_f_wi[s], hy_f_bi[s], hy_f_w3[s],
                                hy_f_freq[s], hy_f_bias[s], hy_w_out[s], hy_b_out[s], consts)
            tail, specs = _hyena_tail, _hyena_tail_specs
        elif kind == 1:
            args = _attn_mixer(x, norm_mix[i], at_w_qkv[s], at_q_gain[s], at_k_gain[s], at_sink[s],
                               at_w_o[s], rel_table)
            tail, specs = _attn_tail, _attn_tail_specs
        else:
            args = (x2, x2, norm_mix[i][None], pl_w[s].astype(BF16), pl_b[s][None], pl_scale[s][None])
            tail, specs = _pool_tail, functools.partial(_pool_tail_specs, rows=bsz * seq)
        x2 = _ffn(x2, norm_ffn[i][None], ff_w_gate, ff_w_up, ff_w_down, i, tail, args, specs)
    return x2.reshape(bsz, seq, d)
```

```python
import functools
import math

import jax
import jax.numpy as jnp
from jax import lax
from jax.experimental import pallas as pl
from jax.experimental.pallas import tpu as pltpu

F32 = jnp.float32
BF16 = jnp.bfloat16

D_MODEL = 1024
SEQ = 8192
DEPTH = 4
EPS = 1e-6

HY_SHORT = 3
HY_BANDS = 16
HY_FW = 64
HY_INNER = 2
HY_MIN_DECAY = math.log(1e-2) / 1.5
HY_MAX_DECAY = math.log(1e-2) / 0.3

HEAD_DIM = 64
N_HEADS = 16
N_KV = 4
GROUP = 4
WINDOW = 128
BLOCK = 128
REL_BUCKETS = 32
REL_MAX_DIST = 128

POOL_WINDOWS = (2, 4, 8, 16)
POOL_GROUP = 256
D_FF = 2816

SUBLANES = 8
LANES = 128
VMEM_LIMIT = 56 * 1024 * 1024

FFT_N = 2 * SEQ
FFT_N1 = 256
FFT_N2 = 64
FFT_K1 = FFT_N1 // 2
SLAB = 2 * FFT_K1 + SUBLANES
CONV_DT = 128
DFT_UNROLL = 64


def _cparams(sem):
    return pltpu.CompilerParams(dimension_semantics=sem, vmem_limit_bytes=VMEM_LIMIT)


def _rms(x, g):
    return x * lax.rsqrt(jnp.mean(x * x, axis=-1, keepdims=True) + EPS) * g


FFN_CHUNK = 256
FFN_CHUNKS = D_FF // FFN_CHUNK
FFN_TM = 512
FFN_TILES_PER_SEQ = SEQ // FFN_TM


def _ffn_kernel(*refs, tail, n_tail):
    x_ref, g_ref, wg_ref, wu_ref, wd_ref = refs[:5]
    tail_refs = refs[5:5 + n_tail]
    o_ref, wgb_ref, wub_ref, wdb_ref, h_ref = refs[5 + n_tail:]
    s = pl.program_id(0)

    @pl.when(s < FFN_CHUNKS)
    def _():
        wgb_ref[s] = wg_ref[...].astype(BF16)
        wub_ref[s] = wu_ref[...].astype(BF16)
        wdb_ref[pl.ds(pl.multiple_of(s * FFN_CHUNK, FFN_CHUNK), FFN_CHUNK), :] = wd_ref[...].astype(BF16)

    @pl.when(s >= FFN_CHUNKS)
    def _():
        x = tail(s - FFN_CHUNKS, x_ref, *tail_refs)
        xn = _rms(x, g_ref[...]).astype(BF16)
        for c in range(FFN_CHUNKS):
            gate = jnp.dot(xn, wgb_ref[c], preferred_element_type=F32)
            up = jnp.dot(xn, wub_ref[c], preferred_element_type=F32)
            h_ref[:, c * FFN_CHUNK:(c + 1) * FFN_CHUNK] = (gate * jax.nn.sigmoid(gate) * up).astype(BF16)
        o_ref[...] = x + jnp.dot(h_ref[...], wdb_ref[...], preferred_element_type=F32)


def _ffn(x2, g, wg, wu, wd, layer, tail, tail_args, tail_specs):
    m, d = x2.shape
    f = wg.shape[2]
    tm = FFN_TM
    tile = lambda s: jnp.maximum(s - FFN_CHUNKS, 0)
    chunk = lambda s: jnp.minimum(s, FFN_CHUNKS - 1)
    return pl.pallas_call(
        functools.partial(_ffn_kernel, tail=tail, n_tail=len(tail_args)),
        grid=(FFN_CHUNKS + m // tm,),
        in_specs=[
            pl.BlockSpec((tm, d), lambda s: (tile(s), 0)),
            pl.BlockSpec((1, d), lambda s: (0, 0)),
            pl.BlockSpec((None, d, FFN_CHUNK), lambda s: (layer, 0, chunk(s))),
            pl.BlockSpec((None, d, FFN_CHUNK), lambda s: (layer, 0, chunk(s))),
            pl.BlockSpec((None, FFN_CHUNK, d), lambda s: (layer, chunk(s), 0)),
        ] + tail_specs(tile),
        out_specs=pl.BlockSpec((tm, d), lambda s: (tile(s), 0)),
        out_shape=jax.ShapeDtypeStruct((m, d), F32),
        scratch_shapes=[pltpu.VMEM((FFN_CHUNKS, d, FFN_CHUNK), BF16),
                        pltpu.VMEM((FFN_CHUNKS, d, FFN_CHUNK), BF16),
                        pltpu.VMEM((f, d), BF16),
                        pltpu.VMEM((tm, f), BF16)],
        compiler_params=_cparams(("arbitrary",)),
        name="mixer_tail_ffn",
    )(x2, g, wg, wu, wd, *tail_args)


INPROJ_N1 = 16
INPROJ_CHUNK = 512


def _slab_perm(nb):
    r = jnp.arange(FFT_N2 * nb)
    src = FFT_N2 * (r % nb) + r // nb
    return (src[:, None] == r[None, :]).astype(BF16)


def _inproj_kernel(xp_ref, xc_ref, xn_ref, g_ref, pm_ref, w_ref, b_ref, cw_ref, cb_ref, o_ref, xs_ref):
    i = pl.program_id(1)
    nb = INPROJ_N1
    tm = FFT_N2 * nb
    g = g_ref[...]
    xn = _rms(xc_ref[...], g).astype(BF16)
    xs_ref[:tm, :] = jnp.dot(pm_ref[...], xn, preferred_element_type=F32).astype(BF16)
    halo = jnp.concatenate([xp_ref[...], xn_ref[...]], axis=0)
    xs_ref[tm:, :] = _rms(halo, g).astype(BF16)

    xs = xs_ref[...]
    row = lax.broadcasted_iota(jnp.int32, (nb, 1), 0)
    inside_before = i > 0
    inside_after = i < pl.num_programs(1) - 1
    for c in range(w_ref.shape[1] // INPROJ_CHUNK):
        cols = slice(c * INPROJ_CHUNK, (c + 1) * INPROJ_CHUNK)
        p = jnp.dot(xs, w_ref[:, cols], preferred_element_type=F32) + b_ref[:, cols]
        cur = p[:tm]
        before = jnp.where(inside_before, p[tm + SUBLANES - 1:tm + SUBLANES], 0.0)
        after = jnp.where(inside_after, p[tm + SUBLANES:tm + SUBLANES + 1], 0.0)
        first = jnp.where(row == 0, before, pltpu.roll(cur[tm - nb:], 1, 0))
        last = jnp.where(row == nb - 1, after, pltpu.roll(cur[:nb], nb - 1, 0))
        dn = jnp.concatenate([first, cur[:tm - nb]], axis=0)
        up = jnp.concatenate([cur[nb:], last], axis=0)
        cw = cw_ref[:, cols]
        out = cb_ref[:, cols] + cw[0:1] * dn + cw[1:2] * cur + cw[2:3] * up
        o_ref[:, :, cols] = out.reshape(FFT_N2, nb, INPROJ_CHUNK).astype(o_ref.dtype)


def _inproj_conv(x, g, w, b, cw, cb):
    bsz, seq, d = x.shape
    n = w.shape[1]
    nb = INPROJ_N1
    tm = FFT_N2 * nb
    hb = tm // SUBLANES
    last = seq // SUBLANES - 1
    const = lambda bi, i: (0, 0)
    return pl.pallas_call(
        _inproj_kernel,
        grid=(bsz, seq // tm),
        in_specs=[
            pl.BlockSpec((None, SUBLANES, d), lambda bi, i: (bi, jnp.maximum(i * hb - 1, 0), 0)),
            pl.BlockSpec((None, tm, d), lambda bi, i: (bi, i, 0)),
            pl.BlockSpec((None, SUBLANES, d), lambda bi, i: (bi, jnp.minimum((i + 1) * hb, last), 0)),
            pl.BlockSpec((1, d), const),
            pl.BlockSpec((tm, tm), const, pipeline_mode=pl.Buffered(1)),
            pl.BlockSpec((d, n), const, pipeline_mode=pl.Buffered(1)),
            pl.BlockSpec((1, n), const),
            pl.BlockSpec((HY_SHORT, n), const),
            pl.BlockSpec((1, n), const),
        ],
        out_specs=pl.BlockSpec((None, FFT_N2, nb, n), lambda bi, i: (bi, 0, i, 0)),
        out_shape=jax.ShapeDtypeStruct((bsz, FFT_N2, seq // FFT_N2, n), BF16),
        scratch_shapes=[pltpu.VMEM((tm + 2 * SUBLANES, d), BF16)],
        compiler_params=_cparams(("parallel", "parallel")),
        name="hyena_inproj",
    )(x, x, x, g, _slab_perm(nb), w, b, cw, cb)


def _dft_mats():
    i32 = jnp.int32
    k1 = jnp.arange(FFT_K1, dtype=i32)[:, None]
    n1 = jnp.arange(FFT_N1, dtype=i32)[None, :]
    m = ((2 * k1 + 1) * n1) % (2 * FFT_N1)
    ang = m.astype(F32) * (math.pi / FFT_N1)
    f1 = jnp.concatenate([jnp.cos(ang), -jnp.sin(ang)], axis=0)
    k1 = jnp.arange(FFT_K1, dtype=i32)[:, None, None]
    k2 = jnp.arange(FFT_N2, dtype=i32)[None, :, None]
    n2 = jnp.arange(FFT_N2, dtype=i32)[None, None, :]
    m = ((2 * (k1 + FFT_N1 * k2) + 1) * n2) % (2 * FFT_N)
    ang = m.astype(F32) * (math.pi / FFT_N)
    c, s = jnp.cos(ang), jnp.sin(ang)
    gm = jnp.concatenate(
        [jnp.concatenate([c, s], axis=2), jnp.concatenate([-s, c], axis=2)], axis=1)
    return f1, gm


def _dot3(a3, b):
    hi, lo = _split_bf16(b)
    return jnp.dot(a3, jnp.concatenate([hi, hi, lo], axis=0), preferred_element_type=F32)


def _stack3_cols(a):
    hi, lo = _split_bf16(a)
    return jnp.concatenate([hi, lo, hi], axis=-1)


def _tap_lag(n):
    return jnp.where(n >= SEQ, FFT_N - n, n).astype(F32)


def _split_bf16(a):
    hi = a.astype(BF16)
    return hi, (a - hi.astype(F32)).astype(BF16)


def _filter_mlp_kernel(fq_ref, w1t_ref, w1b_ref, b1_ref, wi_ref, bi_ref, fr_ref, h3_ref, *, slabs):
    cols = slabs * FFT_N1
    c = lax.broadcasted_iota(jnp.int32, (1, cols), 1)
    n2 = pl.program_id(0) * slabs + c // FFT_N1
    pos = _tap_lag(FFT_N2 * (c % FFT_N1) + n2)
    t = pos * (1.0 / (SEQ - 1))
    ang = (2.0 * math.pi / SEQ) * pos * fq_ref[...]
    fr = fr_ref[...]
    bands = jnp.concatenate([jnp.cos(ang), -jnp.sin(ang)], axis=0)
    pre = w1t_ref[...] * t + _dot3(w1b_ref[...], bands)
    h = jnp.sin(fr * (pre + b1_ref[...]))
    for l in range(HY_INNER):
        h = jnp.sin(fr * (_dot3(wi_ref[l], h) + bi_ref[l]))
    hi, lo = _split_bf16(h)
    h3 = jnp.concatenate([hi, lo, hi], axis=0)
    for k in range(slabs):
        h3_ref[k] = h3[:, k * FFT_N1:(k + 1) * FFT_N1]


def _filter_mlp(fq, w1, b1, wi, bi, fr, slabs=8):
    w1t, w1b = w1[0:1].T, _stack3_cols(w1[1:].T)
    wi3 = _stack3_cols(jnp.swapaxes(wi, 1, 2))
    col = lambda a: a.reshape(-1, 1)
    const2 = lambda i: (0, 0)
    const3 = lambda i: (0, 0, 0)
    return pl.pallas_call(
        functools.partial(_filter_mlp_kernel, slabs=slabs),
        grid=(FFT_N2 // slabs,),
        in_specs=[
            pl.BlockSpec((HY_BANDS, 1), const2),
            pl.BlockSpec((HY_FW, 1), const2),
            pl.BlockSpec((HY_FW, 3 * 2 * HY_BANDS), const2),
            pl.BlockSpec((HY_FW, 1), const2),
            pl.BlockSpec((HY_INNER, HY_FW, 3 * HY_FW), const3),
            pl.BlockSpec((HY_INNER, HY_FW, 1), const3),
            pl.BlockSpec((HY_FW, 1), const2),
        ],
        out_specs=pl.BlockSpec((slabs, 3 * HY_FW, FFT_N1), lambda i: (i, 0, 0)),
        out_shape=jax.ShapeDtypeStruct((FFT_N2, 3 * HY_FW, FFT_N1), BF16),
        compiler_params=_cparams(("parallel",)),
        name="hyena_filter_mlp",
    )(col(fq), w1t, w1b, col(b1), wi3, bi[:, :, None], col(fr))


def _slab_store(s_ref, n2, a):
    rows = pl.ds(pl.multiple_of(n2 * SLAB, SUBLANES), 2 * FFT_K1)
    for l in range(s_ref.shape[0]):
        s_ref[l, rows, :] = a[:, l * LANES:(l + 1) * LANES]


def _slab_load(s_ref, n2):
    rows = pl.ds(pl.multiple_of(n2 * SLAB, SUBLANES), 2 * FFT_K1)
    return jnp.concatenate([s_ref[l, rows, :] for l in range(s_ref.shape[0])], axis=1).astype(BF16)


def _k1_load(s_ref, k1):
    re = pl.ds(k1, FFT_N2, stride=SLAB)
    im = pl.ds(FFT_K1 + k1, FFT_N2, stride=SLAB)
    tiles = [jnp.concatenate([s_ref[l, re, :], s_ref[l, im, :]], axis=0) for l in range(s_ref.shape[0])]
    return jnp.concatenate(tiles, axis=1).astype(BF16)


def _k1_store(s_ref, k1, b):
    for l in range(s_ref.shape[0]):
        cols = slice(l * LANES, (l + 1) * LANES)
        s_ref[l, pl.ds(k1, FFT_N2, stride=SLAB), :] = b[:FFT_N2, cols]
        s_ref[l, pl.ds(FFT_K1 + k1, FFT_N2, stride=SLAB), :] = b[FFT_N2:, cols]


def _stage_scratch(dt):
    return pltpu.VMEM((dt // LANES, FFT_N2 * SLAB, LANES), F32)


def _stage1(src_ref, f1_ref, s_ref):
    def body(n2, c):
        _slab_store(s_ref, n2, jnp.dot(f1_ref[...], src_ref[n2].astype(BF16), preferred_element_type=F32))
        return c
    lax.fori_loop(0, FFT_N2, body, 0, unroll=DFT_UNROLL)


def _filt_dft_kernel(h3_ref, w3_ref, dl_ref, f1_ref, gm_ref, kf_ref, s_ref):
    half = FFT_N1 // 2
    dt = dl_ref.shape[1]
    n1 = lax.broadcasted_iota(jnp.int32, (FFT_N1, 1), 0)
    dl = dl_ref[...]

    def body(n2, l1):
        both = lax.dot_general(h3_ref[n2], w3_ref[...], (((0,), (0,)), ((), ())),
                               preferred_element_type=F32)
        fwd = both[:half, :dt]
        bwd = both[half:, dt:]
        n = FFT_N2 * n1 + n2
        t = _tap_lag(n) * (1.0 / (SEQ - 1))
        taps = jnp.concatenate([fwd, -bwd], axis=0) * jnp.exp(-t * dl)
        taps = jnp.where(n == SEQ, 0.0, taps)
        _slab_store(s_ref, n2, jnp.dot(f1_ref[...], taps.astype(BF16), preferred_element_type=F32))
        return l1 + jnp.sum(jnp.abs(taps), axis=0, keepdims=True)
    l1 = lax.fori_loop(0, FFT_N2, body, jnp.zeros(dl.shape, F32), unroll=DFT_UNROLL)
    inv = 1.0 / l1

    def stage2(k1, c):
        x = jnp.dot(gm_ref[k1], _k1_load(s_ref, k1), preferred_element_type=F32)
        rows = pl.ds(pl.multiple_of(k1 * 2 * FFT_N2, 2 * FFT_N2), 2 * FFT_N2)
        kf_ref[rows, :] = x * inv
        return c
    lax.fori_loop(0, FFT_K1, stage2, 0, unroll=DFT_UNROLL)


def _filter_spectrum(h3, w3, dl2, f1_full, gm):
    ncol = 2 * D_MODEL
    dt = CONV_DT
    nct = ncol // dt
    tiles = lambda a: a.reshape(HY_FW, 2, nct, dt).transpose(2, 0, 1, 3).reshape(nct, HY_FW, 2 * dt)
    w3h, w3l = _split_bf16(w3)
    w3cat = jnp.concatenate([tiles(w3h), tiles(w3h), tiles(w3l)], axis=1)
    return pl.pallas_call(
        _filt_dft_kernel,
        grid=(nct,),
        in_specs=[
            pl.BlockSpec((FFT_N2, 3 * HY_FW, FFT_N1), lambda j: (0, 0, 0)),
            pl.BlockSpec((None, 3 * HY_FW, 2 * dt), lambda j: (j, 0, 0)),
            pl.BlockSpec((1, dt), lambda j: (0, j)),
            pl.BlockSpec((2 * FFT_K1, FFT_N1), lambda j: (0, 0)),
            pl.BlockSpec((FFT_K1, 2 * FFT_N2, 2 * FFT_N2), lambda j: (0, 0, 0),
                         pipeline_mode=pl.Buffered(1)),
        ],
        out_specs=pl.BlockSpec((FFT_K1 * 2 * FFT_N2, dt), lambda j: (0, j)),
        out_shape=jax.ShapeDtypeStruct((FFT_K1 * 2 * FFT_N2, ncol), F32),
        scratch_shapes=[_stage_scratch(dt)],
        compiler_params=_cparams(("parallel",)),
        name="hyena_filter_spectrum",
    )(h3, w3cat, dl2, f1_full, gm)


def _longconv_kernel(z_ref, g_ref, fb_ref, kf_ref, f1_ref, gm_ref, gh_ref, fi_ref, o_ref, s_ref):
    _stage1(z_ref, f1_ref, s_ref)

    def mid(k1, c):
        x = jnp.dot(gm_ref[k1], _k1_load(s_ref, k1), preferred_element_type=F32)
        kf = kf_ref[pl.ds(pl.multiple_of(k1 * 2 * FFT_N2, 2 * FFT_N2), 2 * FFT_N2), :]
        xr, xi = x[:FFT_N2], x[FFT_N2:]
        kr, ki = kf[:FFT_N2], kf[FFT_N2:]
        y = jnp.concatenate([xr * kr - xi * ki, xr * ki + xi * kr], axis=0).astype(BF16)
        b = jnp.dot(gh_ref[k1], y, preferred_element_type=F32)
        _k1_store(s_ref, k1, b)
        return c
    lax.fori_loop(0, FFT_K1, mid, 0, unroll=2 * DFT_UNROLL)

    fb = fb_ref[...]

    def last(n2, c):
        y = jnp.dot(fi_ref[...], _slab_load(s_ref, n2), preferred_element_type=F32)
        o_ref[n2] = (g_ref[n2] * (y + fb * z_ref[n2])).astype(o_ref.dtype)
        return c
    lax.fori_loop(0, FFT_N2, last, 0, unroll=DFT_UNROLL)


def _long_conv_gate(z, zcol, g, gcol, fb, kf, kcol, f1_half, gm, gh, fi):
    bsz = z.shape[0]
    dt = CONV_DT
    nt = D_MODEL // dt
    n1 = FFT_N1 // 2
    const3 = lambda j, bi: (0, 0, 0)
    return pl.pallas_call(
        _longconv_kernel,
        grid=(nt, bsz),
        in_specs=[
            pl.BlockSpec((None, FFT_N2, n1, dt), lambda j, bi: (bi, 0, 0, zcol + j)),
            pl.BlockSpec((None, FFT_N2, n1, dt), lambda j, bi: (bi, 0, 0, gcol + j)),
            pl.BlockSpec((1, dt), lambda j, bi: (0, j)),
            pl.BlockSpec((FFT_K1 * 2 * FFT_N2, dt), lambda j, bi: (0, kcol + j)),
            pl.BlockSpec((2 * FFT_K1, FFT_N1 // 2), lambda j, bi: (0, 0)),
            pl.BlockSpec((FFT_K1, 2 * FFT_N2, 2 * FFT_N2), const3, pipeline_mode=pl.Buffered(1)),
            pl.BlockSpec((FFT_K1, 2 * FFT_N2, 2 * FFT_N2), const3, pipeline_mode=pl.Buffered(1)),
            pl.BlockSpec((FFT_N1 // 2, 2 * FFT_K1), lambda j, bi: (0, 0)),
        ],
        out_specs=pl.BlockSpec((None, FFT_N2, n1, dt), lambda j, bi: (bi, 0, 0, j)),
        out_shape=jax.ShapeDtypeStruct((bsz, FFT_N2, n1, D_MODEL), BF16),
        scratch_shapes=[_stage_scratch(dt)],
        compiler_params=_cparams(("parallel", "arbitrary")),
        name="hyena_long_conv",
    )(z, g, fb, kf, f1_half, gm, gh, fi)


def _hyena_tail(tile, x_ref, z_ref, pm_ref, w_ref, b_ref):
    z = z_ref[...].reshape(x_ref.shape)
    a = jnp.dot(pm_ref[...], z, preferred_element_type=F32).astype(BF16)
    return x_ref[...] + jnp.dot(a, w_ref[...], preferred_element_type=F32) + b_ref[...]


def _hyena_tail_specs(tile):
    nb = FFN_TM // FFT_N2
    d = D_MODEL
    const = lambda s: (0, 0)
    return [
        pl.BlockSpec((None, FFT_N2, nb, d),
                     lambda s: (tile(s) // FFN_TILES_PER_SEQ, 0, tile(s) % FFN_TILES_PER_SEQ, 0)),
        pl.BlockSpec((FFN_TM, FFN_TM), const),
        pl.BlockSpec((d, d), const),
        pl.BlockSpec((1, d), const),
    ]


def _hyena_mixer(x, g_mix, w_in, b_in, cw, cb, fw1, fb1, fwi, fbi, fw3, freq, fbias, w_out, b_out, consts):
    f1_full, f1_half, gm, gh, fi, fq, dl2 = consts
    pc = _inproj_conv(x, g_mix[None], w_in.astype(BF16), b_in[None], cw, cb[None])
    h3 = _filter_mlp(fq, fw1, fb1, fwi, fbi, freq)
    kf = _filter_spectrum(h3, fw3, dl2, f1_full, gm)
    nt = D_MODEL // CONV_DT
    z1 = _long_conv_gate(pc, 2 * nt, pc, 0, fbias[0:1], kf, 0, f1_half, gm, gh, fi)
    z2 = _long_conv_gate(z1, 0, pc, nt, fbias[1:2], kf, nt, f1_half, gm, gh, fi)
    return (z2, _slab_perm(FFN_TM // FFT_N2).T, w_out.astype(BF16), b_out[None])


def _relbias_kernel(tab_ref, bucket_ref, band_ref, o_ref):
    bucket = bucket_ref[...]
    band = band_ref[...]
    for h in range(N_HEADS):
        acc = jnp.zeros(bucket.shape, F32)
        for k in range(REL_BUCKETS):
            acc = jnp.where(bucket == k, tab_ref[k, h], acc)
        o_ref[h] = jnp.where(band > 0, acc, -1e30)


def _rel_bias(rel_table):
    a = jnp.arange(BLOCK)[:, None]
    j = jnp.arange(3 * BLOCK)[None, :]
    rel = j - BLOCK - a
    half = REL_BUCKETS // 2
    exact = half // 2
    n = jnp.abs(rel)
    nf = jnp.maximum(n, 1).astype(F32)
    large = exact + (jnp.log(nf / exact) / math.log(REL_MAX_DIST / exact) * (half - exact)).astype(jnp.int32)
    large = jnp.minimum(large, half - 1)
    bucket = (jnp.where(rel > 0, half, 0) + jnp.where(n < exact, n, large)).astype(jnp.int32)
    band = (n <= WINDOW).astype(jnp.int32)
    return pl.pallas_call(
        _relbias_kernel,
        in_specs=[
            pl.BlockSpec(memory_space=pltpu.SMEM),
            pl.BlockSpec((BLOCK, 3 * BLOCK), lambda: (0, 0)),
            pl.BlockSpec((BLOCK, 3 * BLOCK), lambda: (0, 0)),
        ],
        out_specs=pl.BlockSpec((N_HEADS, BLOCK, 3 * BLOCK), lambda: (0, 0, 0)),
        out_shape=jax.ShapeDtypeStruct((N_HEADS, BLOCK, 3 * BLOCK), F32),
        name="rel_bias",
    )(rel_table, bucket, band)


QKV_CHUNK = 512


def _qkv_kernel(x_ref, g_ref, w_ref, gain_ref, flag_ref, bd_ref, o_ref):
    xn = _rms(x_ref[...], g_ref[...]).astype(BF16)
    for c in range(w_ref.shape[1] // QKV_CHUNK):
        cols = slice(c * QKV_CHUNK, (c + 1) * QKV_CHUNK)
        acc = jnp.dot(xn, w_ref[:, cols], preferred_element_type=F32)
        ssq = jnp.dot((acc * acc).astype(BF16), bd_ref[...], preferred_element_type=F32)
        normed = acc * lax.rsqrt(ssq * (1.0 / HEAD_DIM) + EPS) * gain_ref[:, cols]
        o_ref[:, cols] = jnp.where(flag_ref[:, cols] > 0, normed, acc).astype(o_ref.dtype)


def _qkv_proj(x2, g, w, gain, flag, tm=1024):
    m, k = x2.shape
    n = w.shape[1]
    tn = QKV_CHUNK
    eye = jnp.arange(tn)[:, None] // HEAD_DIM == jnp.arange(tn)[None, :] // HEAD_DIM
    const = lambda i: (0, 0)
    return pl.pallas_call(
        _qkv_kernel,
        grid=(m // tm,),
        in_specs=[
            pl.BlockSpec((tm, k), lambda i: (i, 0)),
            pl.BlockSpec((1, k), const),
            pl.BlockSpec((k, n), const),
            pl.BlockSpec((1, n), const),
            pl.BlockSpec((1, n), const),
            pl.BlockSpec((tn, tn), const),
        ],
        out_specs=pl.BlockSpec((tm, n), lambda i: (i, 0)),
        out_shape=jax.ShapeDtypeStruct((m, n), BF16),
        compiler_params=_cparams(("parallel",)),
        name="qkv_proj",
    )(x2, g, w, gain, flag, eye.astype(BF16))


ATTN_QBLOCKS = 2


def _attn_kernel(q_ref, kp_ref, kc_ref, kn_ref, bias0_ref, bias1_ref, sink_ref, o_ref):
    kvall = jnp.concatenate([kp_ref[...], kc_ref[...], kn_ref[...]], axis=0)
    lane = lax.broadcasted_iota(jnp.int32, (1, LANES), 1)
    keep = [(lane < HEAD_DIM).astype(F32).astype(BF16), (lane >= HEAD_DIM).astype(F32).astype(BF16)]
    ntile = N_KV // 2
    for qb, bias_ref in enumerate((bias0_ref, bias1_ref)):
        kv = kvall[qb * BLOCK:(qb + 3) * BLOCK]
        rows = slice(qb * BLOCK, (qb + 1) * BLOCK)
        for j in range(ntile):
            kt = kv[:, j * LANES:(j + 1) * LANES]
            vt = kv[:, (ntile + j) * LANES:(ntile + j + 1) * LANES]
            tiles = [GROUP * j + g for g in range(GROUP)]
            qs = jnp.concatenate([q_ref[rows, t * LANES:(t + 1) * LANES] for t in tiles], axis=0)
            acc = None
            for half in range(2):
                kh = 2 * j + half
                lg = lax.dot_general(kt * keep[half], qs, (((1,), (1,)), ((), ())),
                                     preferred_element_type=F32)
                lg = lg + bias_ref[kh]
                sink = sink_ref[kh]
                m = jnp.maximum(jnp.max(lg, axis=0, keepdims=True), sink)
                p = jnp.exp2(lg - m)
                den = jnp.sum(p, axis=0, keepdims=True) + jnp.exp2(sink - m)
                o = lax.dot_general(vt * keep[half], p.astype(BF16), (((0,), (0,)), ((), ())),
                                    preferred_element_type=F32) * (1.0 / den)
                acc = o if acc is None else acc + o
            for g, t in enumerate(tiles):
                o_ref[qb, t * LANES:(t + 1) * LANES, :] = (
                    acc[:, g * BLOCK:(g + 1) * BLOCK].astype(o_ref.dtype))


def _attention(qkv, bias_t, sink_row):
    bsz, seq, _ = qkv.shape
    nb = seq // BLOCK
    dq = N_HEADS * HEAD_DIM
    dkv = 2 * N_KV * HEAD_DIM
    kvb = dq // dkv
    qb = ATTN_QBLOCKS
    steps = nb // qb
    bias_spec = lambda variant: pl.BlockSpec((None, N_KV, 3 * BLOCK, GROUP * BLOCK),
                                             lambda b, n: (variant(n), 0, 0, 0))
    return pl.pallas_call(
        _attn_kernel,
        grid=(bsz, steps),
        in_specs=[
            pl.BlockSpec((None, qb * BLOCK, dq), lambda b, n: (b, n, 0)),
            pl.BlockSpec((None, BLOCK, dkv), lambda b, n: (b, jnp.maximum(qb * n - 1, 0), kvb)),
            pl.BlockSpec((None, qb * BLOCK, dkv), lambda b, n: (b, n, kvb)),
            pl.BlockSpec((None, BLOCK, dkv), lambda b, n: (b, jnp.minimum(qb * n + qb, nb - 1), kvb)),
            bias_spec(lambda n: jnp.where(n == 0, 0, 1)),
            bias_spec(lambda n: jnp.where(n == steps - 1, 2, 1)),
            pl.BlockSpec((N_KV, 1, GROUP * BLOCK), lambda b, n: (0, 0, 0)),
        ],
        out_specs=pl.BlockSpec((None, qb, dq, BLOCK), lambda b, n: (b, n, 0, 0)),
        out_shape=jax.ShapeDtypeStruct((bsz, nb, dq, BLOCK), BF16),
        compiler_params=_cparams(("parallel", "parallel")),
        name="window_attention",
    )(qkv, qkv, qkv, qkv, bias_t, bias_t, sink_row)


def _attn_tail(tile, x_ref, ot_ref, w_ref):
    ys = [lax.dot_general(ot_ref[k], w_ref[...], (((0,), (0,)), ((), ())), preferred_element_type=F32)
          for k in range(FFN_TM // BLOCK)]
    return x_ref[...] + jnp.concatenate(ys, axis=0)


def _attn_tail_specs(tile):
    dq = N_HEADS * HEAD_DIM
    return [
        pl.BlockSpec((FFN_TM // BLOCK, dq, BLOCK), lambda s: (tile(s), 0, 0)),
        pl.BlockSpec((dq, D_MODEL), lambda s: (0, 0)),
    ]


def _head_tile_perm():
    cols = []
    for tile in range(N_HEADS // 2):
        j, g = tile // GROUP, tile % GROUP
        for half in range(2):
            h = (2 * j + half) * GROUP + g
            cols.extend(range(h * HEAD_DIM, (h + 1) * HEAD_DIM))
    return jnp.asarray(cols, jnp.int32)


def _attn_mixer(x, g_mix, w_qkv, q_gain, k_gain, sink, w_o, rel_table):
    bsz, seq, d = x.shape
    x2 = x.reshape(bsz * seq, d)
    dq = N_HEADS * HEAD_DIM
    dk = N_KV * HEAD_DIM
    perm = _head_tile_perm()
    w = jnp.concatenate([w_qkv[:, :dq][:, perm], w_qkv[:, dq:]], axis=1).astype(BF16)
    log2e = math.log2(math.e)
    gain = jnp.concatenate([jnp.tile(q_gain * (HEAD_DIM ** -0.5 * log2e), N_HEADS), jnp.tile(k_gain, N_KV),
                            jnp.ones((dk,), F32)])[None]
    flag = jnp.concatenate([jnp.ones((dq + dk,), F32), jnp.zeros((dk,), F32)])[None]
    qkv = _qkv_proj(x2, g_mix[None], w, gain, flag)
    bias_t = _rel_bias(rel_table).reshape(N_KV, GROUP * BLOCK, 3 * BLOCK).transpose(0, 2, 1)
    key = jnp.arange(3 * BLOCK)[None, :, None]
    bias_t = jnp.stack([jnp.where(key < BLOCK, -1e30, bias_t), bias_t,
                        jnp.where(key >= 2 * BLOCK, -1e30, bias_t)]) * log2e
    sink_row = jnp.repeat(sink * log2e, BLOCK).reshape(N_KV, 1, GROUP * BLOCK)
    o_t = _attention(qkv.reshape(bsz, seq, dq + 2 * dk), bias_t, sink_row)
    return (o_t.reshape(bsz * (seq // BLOCK), dq, BLOCK), w_o[perm, :].astype(BF16))


def _pool_tail(tile, xc_ref, xp_ref, xn_ref, g_ref, w_ref, b_ref, sc_ref):
    tm = FFN_TM
    g = g_ref[...]
    rows = tm + 2 * SUBLANES
    t0 = (tile % FFN_TILES_PER_SEQ) * tm
    t = t0 - SUBLANES + lax.broadcasted_iota(jnp.int32, (rows, 1), 0)
    inside = (t >= 0) & (t < SEQ)
    xa = jnp.concatenate([_rms(xp_ref[...], g), _rms(xc_ref[...], g), _rms(xn_ref[...], g)], axis=0)
    xa = jnp.where(inside, xa, 0.0)
    tc = t[SUBLANES:SUBLANES + tm]
    outs = []
    for gi, wdt in enumerate(POOL_WINDOWS):
        r = wdt // 2
        xg = xa[:, gi * POOL_GROUP:(gi + 1) * POOL_GROUP]
        run = xg
        span = 1
        while span < 2 * r:
            run = run + pltpu.roll(run, rows - span, 0)
            span *= 2
        win = (pltpu.roll(run, r, 0) + pltpu.roll(xg, rows - r, 0))[SUBLANES:SUBLANES + tm]
        cnt = (jnp.minimum(tc + r + 1, SEQ) - jnp.maximum(tc - r, 0)).astype(F32)
        dlt = win * (1.0 / cnt) - xg[SUBLANES:SUBLANES + tm]
        outs.append(jnp.dot(dlt.astype(BF16), w_ref[gi], preferred_element_type=F32))
    y = (jnp.concatenate(outs, axis=-1) + b_ref[...]) * sc_ref[...]
    return xc_ref[...] + y


def _pool_tail_specs(tile, rows):
    d = D_MODEL
    hb = FFN_TM // SUBLANES
    last = rows // SUBLANES - 1
    const = lambda s: (0, 0)
    return [
        pl.BlockSpec((SUBLANES, d), lambda s: (jnp.maximum(tile(s) * hb - 1, 0), 0)),
        pl.BlockSpec((SUBLANES, d), lambda s: (jnp.minimum((tile(s) + 1) * hb, last), 0)),
        pl.BlockSpec((1, d), const),
        pl.BlockSpec((len(POOL_WINDOWS), POOL_GROUP, POOL_GROUP), lambda s: (0, 0, 0)),
        pl.BlockSpec((1, d), const),
        pl.BlockSpec((1, d), const),
    ]


def kernel(x, norm_mix, norm_ffn, hy_w_in, hy_b_in, hy_conv_w, hy_conv_b, hy_f_w1, hy_f_b1, hy_f_wi, hy_f_bi, hy_f_w3, hy_f_freq, hy_f_bias, hy_w_out, hy_b_out, at_w_qkv, at_q_gain, at_k_gain, at_sink, at_w_o, rel_table, pl_w, pl_b, pl_scale, ff_w_gate, ff_w_up, ff_w_down):
    bsz, seq, d = x.shape
    f1, gm = _dft_mats()
    f1_full = f1.astype(BF16)
    f1_half = f1[:, :FFT_N1 // 2].astype(BF16)
    fi = (f1[:, :FFT_N1 // 2].T * (2.0 / FFT_N)).astype(BF16)
    gh = jnp.swapaxes(gm, 1, 2).astype(BF16)
    gm = gm.astype(BF16)
    fq = jnp.linspace(1e-4, HY_BANDS - 1, HY_BANDS, dtype=F32)[None, :]
    deltas = jnp.abs(jnp.linspace(HY_MIN_DECAY, HY_MAX_DECAY, D_MODEL, dtype=F32))
    dl2 = jnp.concatenate([deltas, deltas])[None, :]
    consts = (f1_full, f1_half, gm, gh, fi, fq, dl2)

    x2 = x.reshape(bsz * seq, d)
    for i in range(DEPTH):
        kind, s = i % 3, i // 3
        x = x2.reshape(bsz, seq, d)
        if kind == 0:
            args = _hyena_mixer(x, norm_mix[i], hy_w_in[s], hy_b_in[s], hy_conv_w[s], hy_conv_b[s],
                                hy_f_w1[s], hy_f_b1[s], hy_f_wi[s], hy_f_bi[s], hy_f_w3[s],
                                hy_f_freq[s], hy_f_bias[s], hy_w_out[s], hy_b_out[s], consts)
            tail, specs = _hyena_tail, _hyena_tail_specs
        elif kind == 1:
            args = _attn_mixer(x, norm_mix[i], at_w_qkv[s], at_q_gain[s], at_k_gain[s], at_sink[s],
                               at_w_o[s], rel_table)
            tail, specs = _attn_tail, _attn_tail_specs
        else:
            args = (x2, x2, norm_mix[i][None], pl_w[s].astype(BF16), pl_b[s][None], pl_scale[s][None])
            tail, specs = _pool_tail, functools.partial(_pool_tail_specs, rows=bsz * seq)
        x2 = _ffn(x2, norm_ffn[i][None], ff_w_gate, ff_w_up, ff_w_down, i, tail, args, specs)
    return x2.reshape(bsz, seq, d)
```
